```python
import math
import jax
import jax.numpy as jnp
from jax import lax
import numpy as np

D_MODEL = 1024
BATCH = 8
SEQ = 4096
DEPTH = 2

PLE_DIM = 256
NORM_EPS = 1e-6
MASK_VALUE = -1e30

SWA_Q_HEADS = 8
SWA_KV_HEADS = 2
SWA_HEAD_DIM = 64
SWA_WINDOW = 128
ROT_DIM = SWA_HEAD_DIM // 4
ROPE_THETA = 500000.0

GLA_HEADS = 4
GLA_DK = 64
GLA_DV = 128
GLA_GATE_RANK = 16
GLA_GATE_NORM = 16.0
GLA_CHUNK = 64

GDN_HEADS = 4
GDN_DK = 128
GDN_DV = 128
GDN_CONV = 4
GDN_CHUNK = 64

N_BRANCH = 3

N_GROUPS = 4
EXPERTS_PER_GROUP = 8
N_EXPERTS = N_GROUPS * EXPERTS_PER_GROUP
EXPERT_FF = 512
TOP_K = 2
MOE_BLOCK = 128

SWA_Q = SWA_Q_HEADS * SWA_HEAD_DIM
SWA_KV = SWA_KV_HEADS * SWA_HEAD_DIM
GLA_K = GLA_HEADS * GLA_DK
GLA_V = GLA_HEADS * GLA_DV
GDN_K = GDN_HEADS * GDN_DK
GDN_V = GDN_HEADS * GDN_DV
GDN_QKV = 2 * GDN_K + GDN_V
IN_SPLITS = (SWA_Q, SWA_KV, SWA_KV,
             GLA_K, GLA_K, GLA_V, GLA_GATE_RANK, GLA_V,
             GDN_QKV, GDN_HEADS, GDN_HEADS, GDN_V,
             N_BRANCH * D_MODEL)
IN_DIM = sum(IN_SPLITS)
IN_OFFSETS = tuple(int(o) for o in np.cumsum(IN_SPLITS)[:-1])

kernel_name = 'hybrid_swa_gla_gdn_hmoe_ple'


def rmsnorm(x, g):
    xf = x.astype(jnp.float32)
    y = xf * lax.rsqrt(jnp.mean(xf * xf, axis=-1, keepdims=True) + NORM_EPS)
    return (y * g.astype(jnp.float32)).astype(x.dtype)


def l2norm(x):
    return x * lax.rsqrt(jnp.sum(x * x, axis=-1, keepdims=True) + NORM_EPS)


def rope_tables(positions):
    inv_freq = 1.0 / (ROPE_THETA ** (jnp.arange(0, ROT_DIM, 2, dtype=jnp.float32) / ROT_DIM))
    ang = positions.astype(jnp.float32)[..., None] * inv_freq
    return jnp.cos(ang)[:, :, None, :], jnp.sin(ang)[:, :, None, :]


def apply_partial_rope(x, cos, sin):
    half = ROT_DIM // 2
    x1 = x[..., :half].astype(jnp.float32)
    x2 = x[..., half:ROT_DIM].astype(jnp.float32)
    rot = jnp.concatenate([x1 * cos - x2 * sin, x2 * cos + x1 * sin], axis=-1).astype(x.dtype)
    return jnp.concatenate([rot, x[..., ROT_DIM:]], axis=-1)


def to_chunks(t, c):
    b, s = t.shape[0], t.shape[1]
    return jnp.moveaxis(t.reshape(b, s // c, c, *t.shape[2:]), 3, 1)


def from_chunks(t):
    b, h, nc, c, d = t.shape
    return jnp.moveaxis(t, 1, 3).reshape(b, nc * c, h, d)


def sliding_window_sink_attention(q, k, v, sinks):
    B, S, HQ, HD = q.shape
    HKV = k.shape[2]
    GQ = HQ // HKV
    W = SWA_WINDOW
    nb = S // W
    qb = q.reshape(B, nb, W, HKV, GQ, HD)

    def with_prev(t):
        tb = t.reshape(B, nb, W, HKV, HD)
        prev = jnp.concatenate([jnp.zeros_like(tb[:, :1]), tb[:, :-1]], axis=1)
        return jnp.concatenate([prev, tb], axis=2)

    kc, vc = with_prev(k), with_prev(v)
    s = jnp.einsum('bnqhgd,bnkhd->bnhgqk', qb, kc).astype(jnp.float32) * (HD ** -0.5)
    qi = jnp.arange(W)[:, None]
    kj = jnp.arange(2 * W)[None, :]
    band = (kj > qi) & (kj <= qi + W)
    not_first = jnp.arange(nb)[:, None, None] > 0
    mask = band[None] & (not_first | (kj >= W)[None])
    s = jnp.where(mask[None, :, None, None], s, MASK_VALUE)
    sink = sinks.astype(jnp.float32).reshape(1, 1, HKV, GQ, 1, 1)
    m = jnp.maximum(jnp.max(s, axis=-1, keepdims=True), sink)
    e = jnp.exp(s - m)
    prob = e / (jnp.sum(e, axis=-1, keepdims=True) + jnp.exp(sink - m))
    o = jnp.einsum('bnhgqk,bnkhd->bnqhgd', prob.astype(v.dtype), vc)
    return o.reshape(B, S, HQ * HD).astype(q.dtype)


def gla_attention(q, k, v, log_a):
    B, S, H, DK = q.shape
    DV = v.shape[-1]
    C = GLA_CHUNK
    q = to_chunks(q.astype(jnp.float32) * (DK ** -0.5), C)
    k = to_chunks(k.astype(jnp.float32), C)
    v = to_chunks(v.astype(jnp.float32), C)
    b = jnp.cumsum(to_chunks(log_a.astype(jnp.float32), C), axis=3)
    qe = q * jnp.exp(b)
    ke = k * jnp.exp(-b)
    causal = jnp.tril(jnp.ones((C, C), dtype=bool))
    att = jnp.where(causal, jnp.einsum('bhncd,bhnsd->bhncs', qe, ke), 0.0)
    o = jnp.einsum('bhncs,bhnse->bhnce', att, v)
    b_last = b[:, :, :, -1:]
    du = jnp.einsum('bhnsd,bhnse->bhnde', k * jnp.exp(b_last - b), v)
    decay = jnp.exp(b_last[:, :, :, 0])

    def step(state, inp):
        d, u = inp
        return d[..., None] * state + u, state

    s0 = jnp.zeros((B, H, DK, DV), jnp.float32)
    _, s_prev = lax.scan(step, s0, (jnp.moveaxis(decay, 2, 0), jnp.moveaxis(du, 2, 0)))
    o = o + jnp.einsum('bhncd,nbhde->bhnce', qe, s_prev)
    return from_chunks(o)


def gated_delta_rule(q, k, v, beta, g):
    B, S, H, DK = q.shape
    DV = v.shape[-1]
    C = GDN_CHUNK
    q = to_chunks(l2norm(q.astype(jnp.float32)) * (DK ** -0.5), C)
    k = to_chunks(l2norm(k.astype(jnp.float32)), C)
    v = to_chunks(v.astype(jnp.float32), C)
    beta = to_chunks(beta, C)
    G = jnp.cumsum(to_chunks(g, C), axis=-1)
    causal = jnp.tril(jnp.ones((C, C), dtype=bool))
    strict = jnp.tril(jnp.ones((C, C), dtype=bool), k=-1)
    diff = G[..., :, None] - G[..., None, :]
    decay = jnp.where(causal, jnp.exp(jnp.where(causal, diff, 0.0)), 0.0)
    kk = jnp.einsum('bhncd,bhnsd->bhncs', k, k)
    tmat = jnp.eye(C, dtype=jnp.float32) + jnp.where(strict, beta[..., None] * kk * decay, 0.0)
    rhs = jnp.concatenate([v * beta[..., None], k * (beta * jnp.exp(G))[..., None]], axis=-1)
    sol = lax.linalg.triangular_solve(tmat, rhs, left_side=True, lower=True, unit_diagonal=True)
    u, w = sol[..., :DV], sol[..., DV:]
    att = jnp.einsum('bhncd,bhnsd->bhncs', q, k) * decay
    q_dec = q * jnp.exp(G)[..., None]
    G_last = G[..., -1:]
    k_dec = k * jnp.exp(G_last - G)[..., None]
    chunk_decay = jnp.exp(G_last[..., 0])

    def step(state, inp):
        u_c, w_c, q_c, k_c, a_c, d_c = inp
        v_new = u_c - jnp.einsum('bhcd,bhde->bhce', w_c, state)
        o_c = jnp.einsum('bhcd,bhde->bhce', q_c, state) + jnp.einsum('bhcs,bhse->bhce', a_c, v_new)
        state = d_c[..., None, None] * state + jnp.einsum('bhcd,bhce->bhde', k_c, v_new)
        return state, o_c

    xs = (jnp.moveaxis(u, 2, 0), jnp.moveaxis(w, 2, 0), jnp.moveaxis(q_dec, 2, 0),
          jnp.moveaxis(k_dec, 2, 0), jnp.moveaxis(att, 2, 0), jnp.moveaxis(chunk_decay, 2, 0))
    s0 = jnp.zeros((B, H, DK, DV), jnp.float32)
    _, o = lax.scan(step, s0, xs)
    return from_chunks(jnp.moveaxis(o, 0, 2))


def gated_head_norm(o, g, r):
    B, S = o.shape[0], o.shape[1]
    on = rmsnorm(o, g).reshape(B, S, -1)
    return (on * jax.nn.silu(r.astype(jnp.float32))).astype(r.dtype)


def causal_depthwise_conv(x, w):
    return lax.conv_general_dilated(x, w[:, None, :], window_strides=(1,),
                                    padding=[(w.shape[0] - 1, 0)],
                                    dimension_numbers=('NWC', 'WIO', 'NWC'),
                                    feature_group_count=x.shape[-1])


def hierarchical_moe(h, w_coarse, b_coarse, w_fine, b_fine, w_gate_e, w_up_e, w_down_e):
    B, S, D = h.shape
    N = B * S
    M = N * TOP_K
    hf = h.reshape(N, D)
    group_prob = jax.nn.softmax((hf @ w_coarse + b_coarse).astype(jnp.float32), axis=-1)
    g_prob, g_idx = lax.top_k(group_prob, 1)
    fine = (hf @ w_fine + b_fine).astype(jnp.float32).reshape(N, N_GROUPS, EXPERTS_PER_GROUP)
    fine = jnp.take_along_axis(fine, g_idx[:, :, None], axis=1)[:, 0]
    e_prob, e_local = lax.top_k(jax.nn.softmax(fine, axis=-1), TOP_K)
    weight = g_prob * e_prob / jnp.sum(e_prob, axis=-1, keepdims=True)
    expert = (g_idx * EXPERTS_PER_GROUP + e_local).reshape(M).astype(jnp.int32)
    token = jnp.repeat(jnp.arange(N, dtype=jnp.int32), TOP_K)
    order = jnp.argsort(expert)
    e_sorted = expert[order]
    counts = jax.ops.segment_sum(jnp.ones((M,), jnp.int32), expert, num_segments=N_EXPERTS)
    starts = jnp.cumsum(counts) - counts
    padded = (counts + MOE_BLOCK - 1) // MOE_BLOCK * MOE_BLOCK
    pad_end = jnp.cumsum(padded)
    pad_start = pad_end - padded
    slot_sorted = pad_start[e_sorted] + jnp.arange(M, dtype=jnp.int32) - starts[e_sorted]
    slot = jnp.zeros((M,), jnp.int32).at[order].set(slot_sorted)
    n_slots = M + N_EXPERTS * MOE_BLOCK
    n_blocks = n_slots // MOE_BLOCK
    slot_token = jnp.full((n_slots,), N, jnp.int32).at[slot].set(token)
    block_start = jnp.arange(n_blocks, dtype=jnp.int32) * MOE_BLOCK
    block_expert = jnp.minimum(jnp.searchsorted(pad_end, block_start, side='right'), N_EXPERTS - 1)
    h_pad = jnp.concatenate([hf, jnp.zeros((1, D), hf.dtype)], axis=0)

    def run_block(args):
        tok, e = args
        xb = h_pad[tok]
        return (jax.nn.silu(xb @ w_gate_e[e]) * (xb @ w_up_e[e])) @ w_down_e[e]

    y_slots = lax.map(run_block, (slot_token.reshape(n_blocks, MOE_BLOCK), block_expert))
    y = y_slots.reshape(n_slots, D)[slot].reshape(N, TOP_K, D)
    out = jnp.einsum('nk,nkd->nd', weight, y.astype(jnp.float32))
    return out.reshape(B, S, D).astype(h.dtype)


def hybrid_layer(x, p_i, cos, sin, attn_norm, w_in, q_norm, k_norm, sinks, gla_wa2, gla_ba, gla_norm,
                 gdn_conv, gdn_a_log, gdn_dt_bias, gdn_norm, w_br_a, w_br_b, w_br_c, w_o,
                 ffn_norm, w_coarse, b_coarse, w_fine, b_fine, w_gate_e, w_up_e, w_down_e,
                 ple_norm, w_ple_gate, w_ple):
    B, S, D = x.shape
    h = rmsnorm(x, attn_norm)
    (a_q, a_k, a_v, b_q, b_k, b_v, b_lr, b_r, c_qkv, c_beta, c_a, c_g,
     gate_logits) = jnp.split(h @ w_in, IN_OFFSETS, axis=-1)
    a_q = apply_partial_rope(rmsnorm(a_q.reshape(B, S, SWA_Q_HEADS, SWA_HEAD_DIM), q_norm), cos, sin)
    a_k = apply_partial_rope(rmsnorm(a_k.reshape(B, S, SWA_KV_HEADS, SWA_HEAD_DIM), k_norm), cos, sin)
    y_a = sliding_window_sink_attention(a_q, a_k, a_v.reshape(B, S, SWA_KV_HEADS, SWA_HEAD_DIM), sinks)
    log_a = jax.nn.log_sigmoid((b_lr @ gla_wa2 + gla_ba).astype(jnp.float32)) / GLA_GATE_NORM
    o_b = gla_attention(b_q.reshape(B, S, GLA_HEADS, GLA_DK), b_k.reshape(B, S, GLA_HEADS, GLA_DK),
                        b_v.reshape(B, S, GLA_HEADS, GLA_DV), log_a.reshape(B, S, GLA_HEADS, GLA_DK))
    y_b = gated_head_norm(o_b, gla_norm, b_r)
    c_qkv = jax.nn.silu(causal_depthwise_conv(c_qkv, gdn_conv))
    c_q, c_k, c_v = jnp.split(c_qkv, (GDN_K, 2 * GDN_K), axis=-1)
    beta = jax.nn.sigmoid(c_beta.astype(jnp.float32))
    g = -jnp.exp(gdn_a_log.astype(jnp.float32)) * jax.nn.softplus(c_a.astype(jnp.float32) + gdn_dt_bias.astype(jnp.float32))
    o_c = gated_delta_rule(c_q.reshape(B, S, GDN_HEADS, GDN_DK), c_k.reshape(B, S, GDN_HEADS, GDN_DK),
                           c_v.reshape(B, S, GDN_HEADS, GDN_DV), beta, g)
    y_c = gated_head_norm(o_c, gdn_norm, c_g)
    gates = jax.nn.sigmoid(gate_logits.reshape(B, S, N_BRANCH, D))
    merged = (gates[:, :, 0] * (y_a @ w_br_a) + gates[:, :, 1] * (y_b @ w_br_b)
              + gates[:, :, 2] * (y_c @ w_br_c))
    x = x + merged @ w_o
    x = x + hierarchical_moe(rmsnorm(x, ffn_norm), w_coarse, b_coarse, w_fine, b_fine,
                             w_gate_e, w_up_e, w_down_e)
    x = x + jax.nn.sigmoid(rmsnorm(x, ple_norm) @ w_ple_gate) * (p_i @ w_ple)
    return x


def setup_inputs(seed: int = 0) -> dict:
    key = jax.random.key(seed)
    ks = iter(jax.random.split(key, 40))
    L = DEPTH

    def nrm(shape, scale):
        return jax.random.normal(next(ks), shape, jnp.float32) * scale

    def gain(shape):
        return 1.0 + nrm(shape, 0.02)

    x = nrm((BATCH, SEQ, D_MODEL), 1.0)
    p = nrm((L, BATCH, SEQ, PLE_DIM), 1.0)
    offset = jax.random.randint(next(ks), (BATCH, 1), 0, 1024, dtype=jnp.int32)
    positions = offset + jnp.arange(SEQ, dtype=jnp.int32)[None, :]
    attn_norm = gain((L, D_MODEL))
    w_in = nrm((L, D_MODEL, IN_DIM), D_MODEL ** -0.5)
    q_norm = gain((L, SWA_HEAD_DIM))
    k_norm = gain((L, SWA_HEAD_DIM))
    sinks = nrm((L, SWA_Q_HEADS), 1.0)
    gla_wa2 = nrm((L, GLA_GATE_RANK, GLA_K), GLA_GATE_RANK ** -0.5)
    gla_ba = nrm((L, GLA_K), 0.1)
    gla_norm = gain((L, GLA_DV))
    gdn_conv = nrm((L, GDN_CONV, GDN_QKV), GDN_CONV ** -0.5)
    gdn_a_log = jnp.log(jax.random.uniform(next(ks), (L, GDN_HEADS), jnp.float32, 1.0, 16.0))
    dt = jnp.exp(jax.random.uniform(next(ks), (L, GDN_HEADS), jnp.float32, math.log(1e-3), math.log(1e-1)))
    gdn_dt_bias = dt + jnp.log(-jnp.expm1(-dt))
    gdn_norm = gain((L, GDN_DV))
    w_br_a = nrm((L, SWA_Q, D_MODEL), SWA_Q ** -0.5)
    w_br_b = nrm((L, GLA_V, D_MODEL), GLA_V ** -0.5)
    w_br_c = nrm((L, GDN_V, D_MODEL), GDN_V ** -0.5)
    w_o = nrm((L, D_MODEL, D_MODEL), D_MODEL ** -0.5)
    ffn_norm = gain((L, D_MODEL))
    w_coarse = nrm((L, D_MODEL, N_GROUPS), D_MODEL ** -0.5)
    b_coarse = nrm((L, N_GROUPS), 0.01)
    w_fine = nrm((L, D_MODEL, N_EXPERTS), D_MODEL ** -0.5)
    b_fine = nrm((L, N_EXPERTS), 0.01)
    w_gate_e = nrm((L, N_EXPERTS, D_MODEL, EXPERT_FF), D_MODEL ** -0.5)
    w_up_e = nrm((L, N_EXPERTS, D_MODEL, EXPERT_FF), D_MODEL ** -0.5)
    w_down_e = nrm((L, N_EXPERTS, EXPERT_FF, D_MODEL), EXPERT_FF ** -0.5)
    ple_norm = gain((L, D_MODEL))
    w_ple_gate = nrm((L, D_MODEL, D_MODEL), D_MODEL ** -0.5)
    w_ple = nrm((L, PLE_DIM, D_MODEL), PLE_DIM ** -0.5)
    return {'x': x, 'p': p, 'positions': positions, 'attn_norm': attn_norm, 'w_in': w_in,
            'q_norm': q_norm, 'k_norm': k_norm, 'sinks': sinks, 'gla_wa2': gla_wa2, 'gla_ba': gla_ba,
            'gla_norm': gla_norm, 'gdn_conv': gdn_conv, 'gdn_a_log': gdn_a_log,
            'gdn_dt_bias': gdn_dt_bias, 'gdn_norm': gdn_norm, 'w_br_a': w_br_a, 'w_br_b': w_br_b,
            'w_br_c': w_br_c, 'w_o': w_o, 'ffn_norm': ffn_norm, 'w_coarse': w_coarse,
            'b_coarse': b_coarse, 'w_fine': w_fine, 'b_fine': b_fine, 'w_gate_e': w_gate_e,
            'w_up_e': w_up_e, 'w_down_e': w_down_e, 'ple_norm': ple_norm,
            'w_ple_gate': w_ple_gate, 'w_ple': w_ple}


def reference(x, p, positions, attn_norm, w_in, q_norm, k_norm, sinks, gla_wa2, gla_ba, gla_norm,
              gdn_conv, gdn_a_log, gdn_dt_bias, gdn_norm, w_br_a, w_br_b, w_br_c, w_o,
              ffn_norm, w_coarse, b_coarse, w_fine, b_fine, w_gate_e, w_up_e, w_down_e,
              ple_norm, w_ple_gate, w_ple):
    cos, sin = rope_tables(positions)
    for i in range(DEPTH):
        x = hybrid_layer(x, p[i], cos, sin, attn_norm[i], w_in[i], q_norm[i], k_norm[i], sinks[i],
                         gla_wa2[i], gla_ba[i], gla_norm[i], gdn_conv[i], gdn_a_log[i],
                         gdn_dt_bias[i], gdn_norm[i], w_br_a[i], w_br_b[i], w_br_c[i], w_o[i],
                         ffn_norm[i], w_coarse[i], b_coarse[i], w_fine[i], b_fine[i],
                         w_gate_e[i], w_up_e[i], w_down_e[i], ple_norm[i], w_ple_gate[i], w_ple[i])
    return x
```

```python
import functools
import math

import numpy as np
import jax
import jax.numpy as jnp
from jax import lax
from jax.experimental import pallas as pl
from jax.experimental.pallas import tpu as pltpu

F32 = jnp.float32
BF16 = jnp.bfloat16
I32 = jnp.int32

D_MODEL = 1024
PLE_DIM = 256
NORM_EPS = 1e-6
MASK_VALUE = -1e30

SWA_Q_HEADS = 8
SWA_KV_HEADS = 2
SWA_HEAD_DIM = 64
SWA_WINDOW = 128
ROT_DIM = SWA_HEAD_DIM // 4
ROPE_THETA = 500000.0

GLA_HEADS = 4
GLA_DK = 64
GLA_DV = 128
GLA_GATE_RANK = 16
GLA_GATE_NORM = 16.0
CHUNK = 64

GDN_HEADS = 4
GDN_DK = 128
GDN_DV = 128
GDN_CONV = 4

N_GROUPS = 4
EXPERTS_PER_GROUP = 8
N_EXPERTS = N_GROUPS * EXPERTS_PER_GROUP
EXPERT_FF = 512
TOP_K = 2

SWA_Q = SWA_Q_HEADS * SWA_HEAD_DIM
SWA_KV = SWA_KV_HEADS * SWA_HEAD_DIM
GLA_K = GLA_HEADS * GLA_DK
GLA_V = GLA_HEADS * GLA_DV
GDN_K = GDN_HEADS * GDN_DK
GDN_V = GDN_HEADS * GDN_DV
GDN_QKV = 2 * GDN_K + GDN_V
IN_SPLITS = (SWA_Q, SWA_KV, SWA_KV, GLA_K, GLA_K, GLA_V, GLA_GATE_RANK, GLA_V,
             GDN_QKV, GDN_HEADS, GDN_HEADS, GDN_V, 3 * D_MODEL)
IN_OFFSETS = tuple(int(o) for o in np.cumsum((0,) + IN_SPLITS))

LANES = 128
SMALL_LR = 0
SMALL_BETA = GLA_GATE_RANK
SMALL_A = GLA_GATE_RANK + GDN_HEADS

OUT_WIDTHS = (SWA_Q, 2 * SWA_KV, 2 * GLA_K, GLA_V, GLA_V, GDN_QKV, GDN_V, 3 * D_MODEL, LANES)

MOE_ROWS = 256
VMEM_LIMIT = 56 * 1024 * 1024


def _params(n_axes):
    return pltpu.CompilerParams(dimension_semantics=("arbitrary",) * n_axes,
                                vmem_limit_bytes=VMEM_LIMIT)


def _dot(a, b):
    return jnp.dot(a.astype(BF16), b.astype(BF16), preferred_element_type=F32)


def _dot_nt(a, b):
    return lax.dot_general(a.astype(BF16), b.astype(BF16), (((1,), (1,)), ((), ())),
                           preferred_element_type=F32)


def _dot_tn(a, b):
    return lax.dot_general(a.astype(BF16), b.astype(BF16), (((0,), (0,)), ((), ())),
                           preferred_element_type=F32)


def _split2(x):
    hi = x.astype(BF16)
    lo = (x - hi.astype(F32)).astype(BF16)
    return hi, lo


def _dot_exact_lhs(a, x):
    hi = x.astype(BF16)
    r = x - hi.astype(F32)
    mid = r.astype(BF16)
    lo = (r - mid.astype(F32)).astype(BF16)
    a = a.astype(BF16)
    return (jnp.dot(a, hi, preferred_element_type=F32) + jnp.dot(a, mid, preferred_element_type=F32)
            + jnp.dot(a, lo, preferred_element_type=F32))


def _dot_hi(a, b):
    ah, al = _split2(a)
    bh, bl = _split2(b)
    return (jnp.dot(ah, bh, preferred_element_type=F32) + jnp.dot(ah, bl, preferred_element_type=F32)
            + jnp.dot(al, bh, preferred_element_type=F32))


def _rms(x, g):
    return x * lax.rsqrt(jnp.mean(x * x, axis=-1, keepdims=True) + NORM_EPS) * g


def _sigmoid(x):
    return 1.0 / (1.0 + jnp.exp(-x))


def _silu(x):
    return x * _sigmoid(x)


def _softplus(x):
    return jnp.maximum(x, 0.0) + jnp.log(1.0 + jnp.exp(-jnp.abs(x)))


def _log_sigmoid(x):
    return -_softplus(-x)


def _iota(shape, axis):
    return lax.broadcasted_iota(I32, shape, axis)


def _chunk_tril(n, strict=False):
    r = _iota((n, n), 0)
    c = _iota((n, n), 1)
    same = (r // CHUNK) == (c // CHUNK)
    return same & ((c < r) if strict else (c <= r))


def _inproj_kernel(x_ref, g_ref, w_ref, *out_refs):
    h = _rms(x_ref[...], g_ref[...]).astype(BF16)
    off = 0
    for o_ref in out_refs:
        wd = o_ref.shape[-1]
        o_ref[...] = jnp.dot(h, w_ref[:, off:off + wd], preferred_element_type=F32)
        off += wd


def _inproj(x, g, w, tm=256):
    n = x.shape[0]
    nc = w.shape[1]
    return pl.pallas_call(
        _inproj_kernel,
        grid=(n // tm,),
        in_specs=[pl.BlockSpec((tm, D_MODEL), lambda i: (i, 0)),
                  pl.BlockSpec((1, D_MODEL), lambda i: (0, 0)),
                  pl.BlockSpec((D_MODEL, nc), lambda i: (0, 0), pipeline_mode=pl.Buffered(1))],
        out_specs=[pl.BlockSpec((tm, wd), lambda i: (i, 0)) for wd in OUT_WIDTHS],
        out_shape=[jax.ShapeDtypeStruct((n, wd), F32) for wd in OUT_WIDTHS],
        compiler_params=_params(1),
        name="inproj",
    )(x, g, w)


def _rope_kernel(pos_ref, invf_ref, cos_ref, sin_ref):
    ang = pos_ref[...].astype(F32) * invf_ref[...]
    d = _iota(ang.shape, 1) % SWA_HEAD_DIM
    half = ROT_DIM // 2
    cos_ref[...] = jnp.where(d < ROT_DIM, jnp.cos(ang), 1.0)
    s = jnp.sin(ang)
    sin_ref[...] = jnp.where(d < half, -s, jnp.where(d < ROT_DIM, s, 0.0))


def _rope_tables(positions, tm=1024):
    n = positions.size
    tm = min(tm, n)
    inv_freq = 1.0 / (ROPE_THETA ** (jnp.arange(0, ROT_DIM, 2, dtype=F32) / ROT_DIM))
    lane_freq = jnp.tile(jnp.concatenate([inv_freq, inv_freq, jnp.zeros(SWA_HEAD_DIM - ROT_DIM, F32)]),
                         LANES // SWA_HEAD_DIM)[None]
    return pl.pallas_call(
        _rope_kernel,
        grid=(n // tm,),
        in_specs=[pl.BlockSpec((tm, 1), lambda i: (i, 0)),
                  pl.BlockSpec((1, LANES), lambda i: (0, 0))],
        out_specs=[pl.BlockSpec((tm, LANES), lambda i: (i, 0))] * 2,
        out_shape=[jax.ShapeDtypeStruct((n, LANES), F32)] * 2,
        compiler_params=_params(1),
        name="rope_tables",
    )(positions.reshape(n, 1), lane_freq)


def _norm_rope(x, gain, cos, sin):
    lane = _iota(x.shape, 1)
    sq = x * x
    s0 = jnp.sum(jnp.where(lane < SWA_HEAD_DIM, sq, 0.0), axis=-1, keepdims=True)
    s1 = jnp.sum(jnp.where(lane < SWA_HEAD_DIM, 0.0, sq), axis=-1, keepdims=True)
    ms = jnp.where(lane < SWA_HEAD_DIM, s0, s1) * (1.0 / SWA_HEAD_DIM)
    xn = x * lax.rsqrt(ms + NORM_EPS) * gain
    half = ROT_DIM // 2
    up = pltpu.roll(xn, LANES - half, 1)
    down = pltpu.roll(xn, half, 1)
    partner = jnp.where((lane % SWA_HEAD_DIM) < half, up, down)
    return xn * cos + partner * sin


def _swa_kernel(sink_ref, q_ref, kvc_ref, kvp_ref, cc_ref, sc_ref, cp_ref, sp_ref, qn_ref, kn_ref, o_ref):
    j = pl.program_id(1)
    w = SWA_WINDOW
    kc = _norm_rope(kvc_ref[:, :SWA_KV], kn_ref[...], cc_ref[...], sc_ref[...])
    kp = _norm_rope(kvp_ref[:, :SWA_KV], kn_ref[...], cp_ref[...], sp_ref[...])
    k_all = jnp.concatenate([kp, kc], axis=0).astype(BF16)
    v_all = jnp.concatenate([kvp_ref[:, SWA_KV:], kvc_ref[:, SWA_KV:]], axis=0).astype(BF16)
    qi = _iota((w, 2 * w), 0)
    kj = _iota((w, 2 * w), 1)
    first_key = jnp.where(j > 0, 0, w)
    mask = (kj > qi) & (kj <= qi + w) & (kj >= first_key)
    group = SWA_Q_HEADS // SWA_KV_HEADS
    for c in range(SWA_Q // LANES):
        qr = _norm_rope(q_ref[:, c * LANES:(c + 1) * LANES], qn_ref[...], cc_ref[...], sc_ref[...])
        for hh in range(LANES // SWA_HEAD_DIM):
            h = c * (LANES // SWA_HEAD_DIM) + hh
            kvh = h // group
            ks = slice(kvh * SWA_HEAD_DIM, (kvh + 1) * SWA_HEAD_DIM)
            qh = qr[:, hh * SWA_HEAD_DIM:(hh + 1) * SWA_HEAD_DIM]
            s = _dot_nt(qh, k_all[:, ks]) * (SWA_HEAD_DIM ** -0.5)
            s = jnp.where(mask, s, MASK_VALUE)
            sink = sink_ref[h]
            m = jnp.maximum(jnp.max(s, axis=-1, keepdims=True), sink)
            e = jnp.exp(s - m)
            den = jnp.sum(e, axis=-1, keepdims=True) + jnp.exp(sink - m)
            o = _dot(e, v_all[:, ks]) / den
            o_ref[:, h * SWA_HEAD_DIM:(h + 1) * SWA_HEAD_DIM] = o


def _swa(aq, akv, cos_t, sin_t, q_norm, k_norm, sinks, batch, seq):
    n = aq.shape[0]
    w = SWA_WINDOW
    nq = seq // w
    cur = lambda b, j: (b * nq + j, 0)
    prev = lambda b, j: (b * nq + jnp.maximum(j - 1, 0), 0)
    rep = LANES // SWA_HEAD_DIM
    return pl.pallas_call(
        _swa_kernel,
        grid=(batch, nq),
        in_specs=[pl.BlockSpec(memory_space=pltpu.SMEM),
                  pl.BlockSpec((w, SWA_Q), cur),
                  pl.BlockSpec((w, 2 * SWA_KV), cur),
                  pl.BlockSpec((w, 2 * SWA_KV), prev),
                  pl.BlockSpec((w, LANES), cur),
                  pl.BlockSpec((w, LANES), cur),
                  pl.BlockSpec((w, LANES), prev),
                  pl.BlockSpec((w, LANES), prev),
                  pl.BlockSpec((1, LANES), lambda b, j: (0, 0)),
                  pl.BlockSpec((1, LANES), lambda b, j: (0, 0))],
        out_specs=pl.BlockSpec((w, SWA_Q), cur),
        out_shape=jax.ShapeDtypeStruct((n, SWA_Q), F32),
        compiler_params=_params(2),
        name="swa",
    )(sinks, aq, akv, akv, cos_t, sin_t, cos_t, sin_t,
      jnp.tile(q_norm, rep)[None], jnp.tile(k_norm, rep)[None])


def _gla_kernel(qk_ref, v_ref, r_ref, sm_ref, wa_ref, ba_ref, gn_ref, o_ref, state_ref):
    @pl.when(pl.program_id(1) == 0)
    def _():
        state_ref[...] = jnp.zeros_like(state_ref)

    tc = qk_ref.shape[0]
    la = _log_sigmoid(_dot(sm_ref[...], wa_ref[...]) + ba_ref[...]) * (1.0 / GLA_GATE_NORM)
    tri = jnp.where(_chunk_tril(tc), 1.0, 0.0).astype(BF16)
    b = _dot_exact_lhs(tri, la)
    causal = _iota((CHUNK, CHUNK), 1) <= _iota((CHUNK, CHUNK), 0)
    for c in range(tc // CHUNK):
        rows = slice(c * CHUNK, (c + 1) * CHUNK)
        bc = b[rows]
        b_last = bc[CHUNK - 1:CHUNK]
        q = qk_ref[rows, :GLA_K] * (GLA_DK ** -0.5)
        k = qk_ref[rows, GLA_K:]
        qe = q * jnp.exp(bc)
        ke = k * jnp.exp(-bc)
        kd = k * jnp.exp(b_last - bc)
        dcol = jnp.transpose(jnp.broadcast_to(jnp.exp(b_last), (GLA_DV, GLA_K)))
        for h in range(GLA_HEADS):
            ks = slice(h * GLA_DK, (h + 1) * GLA_DK)
            vs = slice(h * GLA_DV, (h + 1) * GLA_DV)
            vh = v_ref[rows, vs]
            st = state_ref[h]
            att = jnp.where(causal, _dot_nt(qe[:, ks], ke[:, ks]), 0.0)
            o = _dot(att, vh) + _dot(qe[:, ks], st)
            state_ref[h] = dcol[ks] * st + _dot_tn(kd[:, ks], vh)
            o_ref[rows, vs] = _rms(o, gn_ref[...]) * _silu(r_ref[rows, vs])


def _gla(bqk, bv, br, small, wa2, ba, gn, batch, seq, tc=256):
    n = bqk.shape[0]
    nt = seq // tc
    blk = lambda b, j: (b * nt + j, 0)
    const = lambda b, j: (0, 0)
    wa_pad = jnp.zeros((LANES, GLA_K), F32).at[SMALL_LR:SMALL_LR + GLA_GATE_RANK].set(wa2)
    return pl.pallas_call(
        _gla_kernel,
        grid=(batch, nt),
        in_specs=[pl.BlockSpec((tc, 2 * GLA_K), blk),
                  pl.BlockSpec((tc, GLA_V), blk),
                  pl.BlockSpec((tc, GLA_V), blk),
                  pl.BlockSpec((tc, LANES), blk),
                  pl.BlockSpec((LANES, GLA_K), const),
                  pl.BlockSpec((1, GLA_K), const),
                  pl.BlockSpec((1, GLA_DV), const)],
        out_specs=pl.BlockSpec((tc, GLA_V), blk),
        out_shape=jax.ShapeDtypeStruct((n, GLA_V), F32),
        scratch_shapes=[pltpu.VMEM((GLA_HEADS, GLA_DK, GLA_DV), F32)],
        compiler_params=_params(2),
        name="gla",
    )(bqk, bv, br, small, wa_pad, ba[None], gn[None])


def _l2(x):
    return x * lax.rsqrt(jnp.sum(x * x, axis=-1, keepdims=True) + NORM_EPS)


def _gdn_kernel(x_ref, xp_ref, g_ref, sm_ref, cw_ref, al_ref, dt_ref, gn_ref, o_ref, state_ref):
    first = pl.program_id(1) == 0

    @pl.when(first)
    def _():
        state_ref[...] = jnp.zeros_like(state_ref)

    tc = x_ref.shape[0]
    halo = xp_ref.shape[0]
    xprev = jnp.where(first, 0.0, xp_ref[...])
    xcat = jnp.concatenate([xprev, x_ref[...]], axis=0)
    conv = cw_ref[GDN_CONV - 1:GDN_CONV] * xcat[halo:halo + tc]
    for t in range(1, GDN_CONV):
        conv = conv + cw_ref[GDN_CONV - 1 - t:GDN_CONV - t] * xcat[halo - t:halo - t + tc]
    qkv = _silu(conv)

    sm = sm_ref[...]
    beta_all = _sigmoid(sm)
    g_all = -jnp.exp(al_ref[...]) * _softplus(sm + dt_ref[...])
    tri = jnp.where(_chunk_tril(tc), 1.0, 0.0).astype(BF16)
    big_g = _dot_exact_lhs(tri, g_all)

    r = _iota((CHUNK, CHUNK), 0)
    cidx = _iota((CHUNK, CHUNK), 1)
    causal = cidx <= r
    strict = cidx < r
    tril_c = jnp.where(causal, 1.0, 0.0).astype(BF16)
    eye = jnp.where(cidx == r, 1.0, 0.0)

    for c in range(tc // CHUNK):
        rows = slice(c * CHUNK, (c + 1) * CHUNK)
        for h in range(GDN_HEADS):
            q = _l2(qkv[rows, h * GDN_DK:(h + 1) * GDN_DK]) * (GDN_DK ** -0.5)
            k = _l2(qkv[rows, GDN_K + h * GDN_DK:GDN_K + (h + 1) * GDN_DK])
            v = qkv[rows, 2 * GDN_K + h * GDN_DV:2 * GDN_K + (h + 1) * GDN_DV]
            beta = beta_all[rows, SMALL_BETA + h:SMALL_BETA + h + 1]
            g = g_all[rows, SMALL_A + h:SMALL_A + h + 1]
            gc = big_g[rows, SMALL_A + h:SMALL_A + h + 1]
            g_last = gc[CHUNK - 1:CHUNK]
            diff = _dot_exact_lhs(tril_c, jnp.where(r > cidx, g, 0.0))
            decay = jnp.where(causal, jnp.exp(diff), 0.0)
            kk = _dot_nt(k, k)
            low = jnp.where(strict, beta * kk * decay, 0.0)
            inv = eye - low
            pw = low
            for _ in range(int(math.log2(CHUNK)) - 1):
                pw = _dot_hi(pw, pw)
                inv = inv + _dot_hi(inv, pw)
            eg = jnp.exp(gc)
            rhs = jnp.concatenate([v * beta, k * (beta * eg)], axis=-1)
            sol = _dot_hi(inv, rhs)
            u = sol[:, :GDN_DV]
            wm = sol[:, GDN_DV:]
            att = _dot_nt(q, k) * decay
            q_dec = q * eg
            k_dec = k * jnp.exp(g_last - gc)
            st = state_ref[h]
            v_new = u - _dot(wm, st)
            o = _dot(q_dec, st) + _dot(att, v_new)
            state_ref[h] = jnp.exp(g_last) * st + _dot_tn(k_dec, v_new)
            vs = slice(h * GDN_DV, (h + 1) * GDN_DV)
            o_ref[rows, vs] = _rms(o, gn_ref[...]) * _silu(g_ref[rows, vs])


def _gdn(cqkv, cg, small, conv_w, a_log, dt_bias, gn, batch, seq, tc=128, halo=8):
    n = cqkv.shape[0]
    nt = seq // tc
    blk = lambda b, j: (b * nt + j, 0)
    const = lambda b, j: (0, 0)
    prev = lambda b, j: (jnp.maximum((b * nt + j) * (tc // halo) - 1, 0), 0)
    al_row = jnp.zeros((1, LANES), F32).at[0, SMALL_A:SMALL_A + GDN_HEADS].set(a_log)
    dt_row = jnp.zeros((1, LANES), F32).at[0, SMALL_A:SMALL_A + GDN_HEADS].set(dt_bias)
    return pl.pallas_call(
        _gdn_kernel,
        grid=(batch, nt),
        in_specs=[pl.BlockSpec((tc, GDN_QKV), blk),
                  pl.BlockSpec((halo, GDN_QKV), prev),
                  pl.BlockSpec((tc, GDN_V), blk),
                  pl.BlockSpec((tc, LANES), blk),
                  pl.BlockSpec((GDN_CONV, GDN_QKV), const),
                  pl.BlockSpec((1, LANES), const),
                  pl.BlockSpec((1, LANES), const),
                  pl.BlockSpec((1, GDN_DV), const)],
        out_specs=pl.BlockSpec((tc, GDN_V), blk),
        out_shape=jax.ShapeDtypeStruct((n, GDN_V), F32),
        scratch_shapes=[pltpu.VMEM((GDN_HEADS, GDN_DK, GDN_DV), F32)],
        compiler_params=_params(2),
        name="gdn",
    )(cqkv, cqkv, cg, small, conv_w, al_row, dt_row, gn[None])


def _first_argmax(x, lane):
    m = jnp.max(x, axis=-1, keepdims=True)
    idx = jnp.min(jnp.where(x == m, lane, LANES), axis=-1, keepdims=True)
    return m, idx


def _merge_kernel(ya_ref, yb_ref, yc_ref, gt_ref, x_ref, wa_ref, wb_ref, wc_ref, wo_ref, fn_ref,
                  wr_ref, br_ref, x1_ref, h2_ref, ri_ref, rw_ref, cnt_ref, carry_ref):
    @pl.when(pl.program_id(0) == 0)
    def _():
        carry_ref[...] = jnp.zeros_like(carry_ref)

    d = D_MODEL
    merged = (_sigmoid(gt_ref[:, :d]) * _dot(ya_ref[...], wa_ref[...])
              + _sigmoid(gt_ref[:, d:2 * d]) * _dot(yb_ref[...], wb_ref[...])
              + _sigmoid(gt_ref[:, 2 * d:]) * _dot(yc_ref[...], wc_ref[...]))
    x1 = x_ref[...] + _dot(merged, wo_ref[...])
    x1_ref[...] = x1
    h2 = _rms(x1, fn_ref[...])
    h2_ref[...] = h2

    logits = _dot_hi(h2, wr_ref[...]) + br_ref[...]
    tm = logits.shape[0]
    lane = _iota((tm, LANES), 1)
    neg = -jnp.inf
    is_c = lane < N_GROUPS
    cm, g_idx = _first_argmax(jnp.where(is_c, logits, neg), lane)
    g_prob = 1.0 / jnp.sum(jnp.where(is_c, jnp.exp(logits - cm), 0.0), axis=-1, keepdims=True)
    sel = (lane >= N_GROUPS) & (((lane - N_GROUPS) // EXPERTS_PER_GROUP) == g_idx)
    fm = jnp.max(jnp.where(sel, logits, neg), axis=-1, keepdims=True)
    ef = jnp.where(sel, jnp.exp(logits - fm), 0.0)
    p1, i1 = _first_argmax(jnp.where(sel, ef, neg), lane)
    p2, i2 = _first_argmax(jnp.where(sel & (lane != i1), ef, neg), lane)
    w1 = g_prob * p1 / (p1 + p2)
    w2 = g_prob * p2 / (p1 + p2)
    e1 = i1 - N_GROUPS
    e2 = i2 - N_GROUPS

    oh = jnp.where((lane == e1) | (lane == e2 + N_EXPERTS), 1.0, 0.0)
    stril = jnp.where(_iota((tm, tm), 1) < _iota((tm, tm), 0), 1.0, 0.0).astype(BF16)
    before = jnp.dot(stril, oh.astype(BF16), preferred_element_type=F32)
    tot = jnp.sum(oh, axis=0, keepdims=True)
    tot_first = jnp.where(lane[:1] < N_EXPERTS, tot, 0.0)
    carry = carry_ref[...]
    base = carry + pltpu.roll(carry + tot_first, N_EXPERTS, 1)
    ranks = oh * (before + base)
    rank1 = jnp.sum(jnp.where(lane < N_EXPERTS, ranks, 0.0), axis=-1, keepdims=True)
    rank2 = jnp.sum(jnp.where(lane < N_EXPERTS, 0.0, ranks), axis=-1, keepdims=True)
    new_carry = carry + tot_first + pltpu.roll(tot - tot_first, LANES - N_EXPERTS, 1)
    carry_ref[...] = new_carry
    cnt_ref[...] = jnp.broadcast_to(new_carry, cnt_ref.shape).astype(I32)

    ri_ref[...] = jnp.where(lane == 0, e1, jnp.where(lane == 1, e2, jnp.where(
        lane == 2, rank1.astype(I32), jnp.where(lane == 3, rank2.astype(I32), 0))))
    rw_ref[...] = jnp.where(lane == 0, w1, jnp.where(lane == 1, w2, 0.0))


def _merge(ya, yb, yc, gates, x, wa, wb, wc, wo, fn, wr, br, tm=256):
    n = x.shape[0]
    d = D_MODEL
    blk = lambda i: (i, 0)
    const = lambda i: (0, 0)
    return pl.pallas_call(
        _merge_kernel,
        grid=(n // tm,),
        in_specs=[pl.BlockSpec((tm, SWA_Q), blk), pl.BlockSpec((tm, GLA_V), blk),
                  pl.BlockSpec((tm, GDN_V), blk), pl.BlockSpec((tm, 3 * d), blk),
                  pl.BlockSpec((tm, d), blk),
                  pl.BlockSpec((SWA_Q, d), const), pl.BlockSpec((GLA_V, d), const),
                  pl.BlockSpec((GDN_V, d), const), pl.BlockSpec((d, d), const),
                  pl.BlockSpec((1, d), const), pl.BlockSpec((d, LANES), const),
                  pl.BlockSpec((1, LANES), const)],
        out_specs=[pl.BlockSpec((tm, d), blk), pl.BlockSpec((tm, d), blk),
                   pl.BlockSpec((tm, LANES), blk), pl.BlockSpec((tm, LANES), blk),
                   pl.BlockSpec((8, LANES), const)],
        out_shape=[jax.ShapeDtypeStruct((n, d), F32), jax.ShapeDtypeStruct((n, d), F32),
                   jax.ShapeDtypeStruct((n, LANES), I32), jax.ShapeDtypeStruct((n, LANES), F32),
                   jax.ShapeDtypeStruct((8, LANES), I32)],
        scratch_shapes=[pltpu.VMEM((1, LANES), F32)],
        compiler_params=_params(1),
        name="merge_route",
    )(ya, yb, yc, gates, x, wa, wb, wc, wo, fn, wr, br)


def _dispatch_kernel(slot_ref, h_ref, zero_ref, xs_ref, sem):
    del zero_ref
    td = h_ref.shape[0]

    def row_copy(t, k):
        s = slot_ref[0, 0, TOP_K * t + k]
        return pltpu.make_async_copy(h_ref.at[pl.ds(t, 1)], xs_ref.at[pl.ds(s, 1)], sem)

    def issue(t, carry):
        for k in range(TOP_K):
            row_copy(t, k).start()
        return carry

    lax.fori_loop(0, td, issue, 0)

    def drain(t, carry):
        for k in range(TOP_K):
            row_copy(t, k).wait()
        return carry

    lax.fori_loop(0, td, drain, 0)


def _dispatch(h2, slots, n_slots, td=256):
    n = h2.shape[0]
    zeros = jnp.zeros((n_slots, D_MODEL), F32)
    return pl.pallas_call(
        _dispatch_kernel,
        grid=(n // td,),
        in_specs=[pl.BlockSpec((1, 1, TOP_K * td), lambda i: (i, 0, 0), memory_space=pltpu.SMEM),
                  pl.BlockSpec((td, D_MODEL), lambda i: (i, 0)),
                  pl.BlockSpec(memory_space=pl.ANY)],
        out_specs=pl.BlockSpec(memory_space=pl.ANY),
        out_shape=jax.ShapeDtypeStruct((n_slots, D_MODEL), F32),
        scratch_shapes=[pltpu.SemaphoreType.DMA(())],
        input_output_aliases={2: 0},
        compiler_params=_params(1),
        name="dispatch",
    )(slots.reshape(n // td, 1, TOP_K * td), h2, zeros)


def _expert_kernel(be_ref, xs_ref, wg_ref, wu_ref, wd_ref, ys_ref):
    del be_ref
    x = xs_ref[...].astype(BF16)
    act = _silu(jnp.dot(x, wg_ref[0], preferred_element_type=F32)) * jnp.dot(x, wu_ref[0], preferred_element_type=F32)
    ys_ref[...] = jnp.dot(act.astype(BF16), wd_ref[0], preferred_element_type=F32)


def _experts(xs, block_expert, wg, wu, wd):
    n_slots = xs.shape[0]
    tb = MOE_ROWS
    grid_spec = pltpu.PrefetchScalarGridSpec(
        num_scalar_prefetch=1,
        grid=(n_slots // tb,),
        in_specs=[pl.BlockSpec((tb, D_MODEL), lambda i, be: (i, 0)),
                  pl.BlockSpec((1, D_MODEL, EXPERT_FF), lambda i, be: (be[i], 0, 0)),
                  pl.BlockSpec((1, D_MODEL, EXPERT_FF), lambda i, be: (be[i], 0, 0)),
                  pl.BlockSpec((1, EXPERT_FF, D_MODEL), lambda i, be: (be[i], 0, 0))],
        out_specs=pl.BlockSpec((tb, D_MODEL), lambda i, be: (i, 0)),
    )
    return pl.pallas_call(
        _expert_kernel,
        grid_spec=grid_spec,
        out_shape=jax.ShapeDtypeStruct((n_slots, D_MODEL), F32),
        compiler_params=_params(1),
        name="experts",
    )(block_expert, xs, wg, wu, wd)


def _combine_kernel(slot_ref, x1_ref, rw_ref, p_ref, ys_ref, pn_ref, wpg_ref, wpl_ref, o_ref, ybuf, sem):
    tp = x1_ref.shape[0]

    def row_copy(t, k):
        s = slot_ref[0, 0, TOP_K * t + k]
        return pltpu.make_async_copy(ys_ref.at[pl.ds(s, 1)], ybuf.at[k, pl.ds(t, 1)], sem)

    def issue(t, carry):
        for k in range(TOP_K):
            row_copy(t, k).start()
        return carry

    lax.fori_loop(0, tp, issue, 0)

    def drain(t, carry):
        for k in range(TOP_K):
            row_copy(t, k).wait()
        return carry

    lax.fori_loop(0, tp, drain, 0)

    rw = rw_ref[...]
    x2 = x1_ref[...] + rw[:, 0:1] * ybuf[0] + rw[:, 1:2] * ybuf[1]
    h3 = _rms(x2, pn_ref[...])
    gate = _sigmoid(_dot(h3, wpg_ref[...]))
    o_ref[...] = x2 + gate * _dot(p_ref[...], wpl_ref[...])


def _combine_ple(x1, rw, slots, ys, p_i, pn, wpg, wpl, tp=256):
    n = x1.shape[0]
    d = D_MODEL
    blk = lambda i: (i, 0)
    const = lambda i: (0, 0)
    return pl.pallas_call(
        _combine_kernel,
        grid=(n // tp,),
        in_specs=[pl.BlockSpec((1, 1, TOP_K * tp), lambda i: (i, 0, 0), memory_space=pltpu.SMEM),
                  pl.BlockSpec((tp, d), blk), pl.BlockSpec((tp, LANES), blk),
                  pl.BlockSpec((tp, PLE_DIM), blk),
                  pl.BlockSpec(memory_space=pl.ANY),
                  pl.BlockSpec((1, d), const), pl.BlockSpec((d, d), const),
                  pl.BlockSpec((PLE_DIM, d), const)],
        out_specs=pl.BlockSpec((tp, d), blk),
        out_shape=jax.ShapeDtypeStruct((n, d), F32),
        scratch_shapes=[pltpu.VMEM((TOP_K, tp, d), F32), pltpu.SemaphoreType.DMA(())],
        compiler_params=_params(1),
        name="combine_ple",
    )(slots.reshape(n // tp, 1, TOP_K * tp), x1, rw, p_i, ys, pn, wpg, wpl)


def _pack_w_in(w_in):
    o = IN_OFFSETS
    sec = lambda i: w_in[:, o[i]:o[i + 1]]
    pad = jnp.zeros((D_MODEL, LANES - GLA_GATE_RANK - 2 * GDN_HEADS), w_in.dtype)
    cols = [sec(0), sec(1), sec(2), sec(3), sec(4), sec(5), sec(7), sec(8), sec(11), sec(12),
            sec(6), sec(9), sec(10), pad]
    return jnp.concatenate(cols, axis=1).astype(BF16)


def _layer(x, p_i, cos_t, sin_t, batch, seq, attn_norm, w_in, q_norm, k_norm, sinks, gla_wa2, gla_ba,
           gla_norm, gdn_conv, gdn_a_log, gdn_dt_bias, gdn_norm, w_br_a, w_br_b, w_br_c, w_o,
           ffn_norm, w_coarse, b_coarse, w_fine, b_fine, w_gate_e, w_up_e, w_down_e,
           ple_norm, w_ple_gate, w_ple):
    n = x.shape[0]
    aq, akv, bqk, bv, br, cqkv, cg, gates, small = _inproj(x, attn_norm[None], _pack_w_in(w_in))
    ya = _swa(aq, akv, cos_t, sin_t, q_norm, k_norm, sinks, batch, seq)
    yb = _gla(bqk, bv, br, small, gla_wa2, gla_ba, gla_norm, batch, seq)
    yc = _gdn(cqkv, cg, small, gdn_conv, gdn_a_log, gdn_dt_bias, gdn_norm, batch, seq)

    wr = jnp.zeros((D_MODEL, LANES), F32).at[:, :N_GROUPS].set(w_coarse)
    wr = wr.at[:, N_GROUPS:N_GROUPS + N_EXPERTS].set(w_fine)
    brow = jnp.zeros((1, LANES), F32).at[0, :N_GROUPS].set(b_coarse)
    brow = brow.at[0, N_GROUPS:N_GROUPS + N_EXPERTS].set(b_fine)
    x1, h2, ri, rw, cnt = _merge(ya, yb, yc, gates, x, w_br_a.astype(BF16), w_br_b.astype(BF16),
                                 w_br_c.astype(BF16), w_o.astype(BF16), ffn_norm[None], wr, brow)

    counts = cnt[0, :N_EXPERTS]
    padded = (counts + MOE_ROWS - 1) // MOE_ROWS * MOE_ROWS
    pad_end = jnp.cumsum(padded)
    pad_start = pad_end - padded
    slots = (pad_start[ri[:, :TOP_K]] + ri[:, TOP_K:2 * TOP_K]).astype(I32)
    n_slots = n * TOP_K + N_EXPERTS * MOE_ROWS
    block_start = jnp.arange(n_slots // MOE_ROWS, dtype=I32) * MOE_ROWS
    block_expert = jnp.minimum(jnp.searchsorted(pad_end, block_start, side='right'),
                               N_EXPERTS - 1).astype(I32)

    xs = _dispatch(h2, slots, n_slots)
    ys = _experts(xs, block_expert, w_gate_e.astype(BF16), w_up_e.astype(BF16), w_down_e.astype(BF16))
    return _combine_ple(x1, rw, slots, ys, p_i, ple_norm[None], w_ple_gate.astype(BF16),
                        w_ple.astype(BF16))


def kernel(x, p, positions, attn_norm, w_in, q_norm, k_norm, sinks, gla_wa2, gla_ba, gla_norm, gdn_conv, gdn_a_log, gdn_dt_bias, gdn_norm, w_br_a, w_br_b, w_br_c, w_o, ffn_norm, w_coarse, b_coarse, w_fine, b_fine, w_gate_e, w_up_e, w_down_e, ple_norm, w_ple_gate, w_ple):
    batch, seq, d = x.shape
    n = batch * seq
    depth = p.shape[0]
    cos_t, sin_t = _rope_tables(positions)
    xf = x.reshape(n, d)
    pf = p.reshape(depth, n, p.shape[-1])
    per_layer = (attn_norm, w_in, q_norm, k_norm, sinks, gla_wa2, gla_ba, gla_norm, gdn_conv, gdn_a_log,
                 gdn_dt_bias, gdn_norm, w_br_a, w_br_b, w_br_c, w_o, ffn_norm, w_coarse, b_coarse,
                 w_fine, b_fine, w_gate_e, w_up_e, w_down_e, ple_norm, w_ple_gate, w_ple)
    for i in range(depth):
        xf = _layer(xf, pf[i], cos_t, sin_t, batch, seq, *[a[i] for a in per_layer])
    return xf.reshape(batch, seq, d)
```

```python
import functools
import math

import numpy as np
import jax
import jax.numpy as jnp
from jax import lax
from jax.experimental import pallas as pl
from jax.experimental.pallas import tpu as pltpu

F32 = jnp.float32
BF16 = jnp.bfloat16
I32 = jnp.int32

D_MODEL = 1024
PLE_DIM = 256
NORM_EPS = 1e-6
MASK_VALUE = -1e30

SWA_Q_HEADS = 8
SWA_KV_HEADS = 2
SWA_HEAD_DIM = 64
SWA_WINDOW = 128
ROT_DIM = SWA_HEAD_DIM // 4
ROPE_THETA = 500000.0

GLA_HEADS = 4
GLA_DK = 64
GLA_DV = 128
GLA_GATE_RANK = 16
GLA_GATE_NORM = 16.0
CHUNK = 64

GDN_HEADS = 4
GDN_DK = 128
GDN_DV = 128
GDN_CONV = 4

N_GROUPS = 4
EXPERTS_PER_GROUP = 8
N_EXPERTS = N_GROUPS * EXPERTS_PER_GROUP
EXPERT_FF = 512
TOP_K = 2

SWA_Q = SWA_Q_HEADS * SWA_HEAD_DIM
SWA_KV = SWA_KV_HEADS * SWA_HEAD_DIM
GLA_K = GLA_HEADS * GLA_DK
GLA_V = GLA_HEADS * GLA_DV
GDN_K = GDN_HEADS * GDN_DK
GDN_V = GDN_HEADS * GDN_DV
GDN_QKV = 2 * GDN_K + GDN_V
IN_SPLITS = (SWA_Q, SWA_KV, SWA_KV, GLA_K, GLA_K, GLA_V, GLA_GATE_RANK, GLA_V,
             GDN_QKV, GDN_HEADS, GDN_HEADS, GDN_V, 3 * D_MODEL)
IN_OFFSETS = tuple(int(o) for o in np.cumsum((0,) + IN_SPLITS))

LANES = 128
SMALL_LR = 0
SMALL_BETA = GLA_GATE_RANK
SMALL_A = GLA_GATE_RANK + GDN_HEADS

OUT_WIDTHS = (SWA_Q, 2 * SWA_KV, 2 * GLA_K, GLA_V, GLA_V, GDN_QKV, GDN_V, 3 * D_MODEL, LANES)

MOE_ROWS = 256
VMEM_LIMIT = 56 * 1024 * 1024


def _params(n_axes):
    return pltpu.CompilerParams(dimension_semantics=("arbitrary",) * n_axes,
                                vmem_limit_bytes=VMEM_LIMIT)


def _dot(a, b):
    return jnp.dot(a.astype(BF16), b.astype(BF16), preferred_element_type=F32)


def _dot_nt(a, b):
    return lax.dot_general(a.astype(BF16), b.astype(BF16), (((1,), (1,)), ((), ())),
                           preferred_element_type=F32)


def _dot_tn(a, b):
    return lax.dot_general(a.astype(BF16), b.astype(BF16), (((0,), (0,)), ((), ())),
                           preferred_element_type=F32)


def _split2(x):
    hi = x.astype(BF16)
    lo = (x - hi.astype(F32)).astype(BF16)
    return hi, lo


def _dot_exact_lhs(a, x):
    hi = x.astype(BF16)
    r = x - hi.astype(F32)
    mid = r.astype(BF16)
    lo = (r - mid.astype(F32)).astype(BF16)
    a = a.astype(BF16)
    return (jnp.dot(a, hi, preferred_element_type=F32) + jnp.dot(a, mid, preferred_element_type=F32)
            + jnp.dot(a, lo, preferred_element_type=F32))


def _dot_hi(a, b):
    ah, al = _split2(a)
    bh, bl = _split2(b)
    return (jnp.dot(ah, bh, preferred_element_type=F32) + jnp.dot(ah, bl, preferred_element_type=F32)
            + jnp.dot(al, bh, preferred_element_type=F32))


def _rms(x, g):
    return x * lax.rsqrt(jnp.mean(x * x, axis=-1, keepdims=True) + NORM_EPS) * g


def _sigmoid(x):
    return 1.0 / (1.0 + jnp.exp(-x))


def _silu(x):
    return x * _sigmoid(x)


def _softplus(x):
    return jnp.maximum(x, 0.0) + jnp.log(1.0 + jnp.exp(-jnp.abs(x)))


def _log_sigmoid(x):
    return -_softplus(-x)


def _iota(shape, axis):
    return lax.broadcasted_iota(I32, shape, axis)


def _chunk_tril(n, strict=False):
    r = _iota((n, n), 0)
    c = _iota((n, n), 1)
    same = (r // CHUNK) == (c // CHUNK)
    return same & ((c < r) if strict else (c <= r))


def _inproj_kernel(x_ref, g_ref, w_ref, *out_refs):
    h = _rms(x_ref[...], g_ref[...]).astype(BF16)
    off = 0
    for o_ref in out_refs:
        wd = o_ref.shape[-1]
        o_ref[...] = jnp.dot(h, w_ref[:, off:off + wd], preferred_element_type=F32)
        off += wd


def _inproj(x, g, w, tm=256):
    n = x.shape[0]
    nc = w.shape[1]
    return pl.pallas_call(
        _inproj_kernel,
        grid=(n // tm,),
        in_specs=[pl.BlockSpec((tm, D_MODEL), lambda i: (i, 0)),
                  pl.BlockSpec((1, D_MODEL), lambda i: (0, 0)),
                  pl.BlockSpec((D_MODEL, nc), lambda i: (0, 0), pipeline_mode=pl.Buffered(1))],
        out_specs=[pl.BlockSpec((tm, wd), lambda i: (i, 0)) for wd in OUT_WIDTHS],
        out_shape=[jax.ShapeDtypeStruct((n, wd), F32) for wd in OUT_WIDTHS],
        compiler_params=_params(1),
        name="inproj",
    )(x, g, w)


def _rope_kernel(pos_ref, invf_ref, cos_ref, sin_ref):
    ang = pos_ref[...].astype(F32) * invf_ref[...]
    d = _iota(ang.shape, 1) % SWA_HEAD_DIM
    half = ROT_DIM // 2
    cos_ref[...] = jnp.where(d < ROT_DIM, jnp.cos(ang), 1.0)
    s = jnp.sin(ang)
    sin_ref[...] = jnp.where(d < half, -s, jnp.where(d < ROT_DIM, s, 0.0))


def _rope_tables(positions, tm=1024):
    n = positions.size
    tm = min(tm, n)
    inv_freq = 1.0 / (ROPE_THETA ** (jnp.arange(0, ROT_DIM, 2, dtype=F32) / ROT_DIM))
    lane_freq = jnp.tile(jnp.concatenate([inv_freq, inv_freq, jnp.zeros(SWA_HEAD_DIM - ROT_DIM, F32)]),
                         LANES // SWA_HEAD_DIM)[None]
    return pl.pallas_call(
        _rope_kernel,
        grid=(n // tm,),
        in_specs=[pl.BlockSpec((tm, 1), lambda i: (i, 0)),
                  pl.BlockSpec((1, LANES), lambda i: (0, 0))],
        out_specs=[pl.BlockSpec((tm, LANES), lambda i: (i, 0))] * 2,
        out_shape=[jax.ShapeDtypeStruct((n, LANES), F32)] * 2,
        compiler_params=_params(1),
        name="rope_tables",
    )(positions.reshape(n, 1), lane_freq)


def _norm_rope(x, gain, cos, sin):
    lane = _iota(x.shape, 1)
    sq = x * x
    s0 = jnp.sum(jnp.where(lane < SWA_HEAD_DIM, sq, 0.0), axis=-1, keepdims=True)
    s1 = jnp.sum(jnp.where(lane < SWA_HEAD_DIM, 0.0, sq), axis=-1, keepdims=True)
    ms = jnp.where(lane < SWA_HEAD_DIM, s0, s1) * (1.0 / SWA_HEAD_DIM)
    xn = x * lax.rsqrt(ms + NORM_EPS) * gain
    half = ROT_DIM // 2
    up = pltpu.roll(xn, LANES - half, 1)
    down = pltpu.roll(xn, half, 1)
    partner = jnp.where((lane % SWA_HEAD_DIM) < half, up, down)
    return xn * cos + partner * sin


def _swa_kernel(sink_ref, q_ref, kvc_ref, kvp_ref, cc_ref, sc_ref, cp_ref, sp_ref, qn_ref, kn_ref, o_ref):
    j = pl.program_id(1)
    w = SWA_WINDOW
    kc = _norm_rope(kvc_ref[:, :SWA_KV], kn_ref[...], cc_ref[...], sc_ref[...])
    kp = _norm_rope(kvp_ref[:, :SWA_KV], kn_ref[...], cp_ref[...], sp_ref[...])
    k_all = jnp.concatenate([kp, kc], axis=0).astype(BF16)
    v_all = jnp.concatenate([kvp_ref[:, SWA_KV:], kvc_ref[:, SWA_KV:]], axis=0).astype(BF16)
    qi = _iota((w, 2 * w), 0)
    kj = _iota((w, 2 * w), 1)
    first_key = jnp.where(j > 0, 0, w)
    mask = (kj > qi) & (kj <= qi + w) & (kj >= first_key)
    group = SWA_Q_HEADS // SWA_KV_HEADS
    for c in range(SWA_Q // LANES):
        qr = _norm_rope(q_ref[:, c * LANES:(c + 1) * LANES], qn_ref[...], cc_ref[...], sc_ref[...])
        for hh in range(LANES // SWA_HEAD_DIM):
            h = c * (LANES // SWA_HEAD_DIM) + hh
            kvh = h // group
            ks = slice(kvh * SWA_HEAD_DIM, (kvh + 1) * SWA_HEAD_DIM)
            qh = qr[:, hh * SWA_HEAD_DIM:(hh + 1) * SWA_HEAD_DIM]
            s = _dot_nt(qh, k_all[:, ks]) * (SWA_HEAD_DIM ** -0.5)
            s = jnp.where(mask, s, MASK_VALUE)
            sink = sink_ref[h]
            m = jnp.maximum(jnp.max(s, axis=-1, keepdims=True), sink)
            e = jnp.exp(s - m)
            den = jnp.sum(e, axis=-1, keepdims=True) + jnp.exp(sink - m)
            o = _dot(e, v_all[:, ks]) / den
            o_ref[:, h * SWA_HEAD_DIM:(h + 1) * SWA_HEAD_DIM] = o


def _swa(aq, akv, cos_t, sin_t, q_norm, k_norm, sinks, batch, seq):
    n = aq.shape[0]
    w = SWA_WINDOW
    nq = seq // w
    cur = lambda b, j: (b * nq + j, 0)
    prev = lambda b, j: (b * nq + jnp.maximum(j - 1, 0), 0)
    rep = LANES // SWA_HEAD_DIM
    return pl.pallas_call(
        _swa_kernel,
        grid=(batch, nq),
        in_specs=[pl.BlockSpec(memory_space=pltpu.SMEM),
                  pl.BlockSpec((w, SWA_Q), cur),
                  pl.BlockSpec((w, 2 * SWA_KV), cur),
                  pl.BlockSpec((w, 2 * SWA_KV), prev),
                  pl.BlockSpec((w, LANES), cur),
                  pl.BlockSpec((w, LANES), cur),
                  pl.BlockSpec((w, LANES), prev),
                  pl.BlockSpec((w, LANES), prev),
                  pl.BlockSpec((1, LANES), lambda b, j: (0, 0)),
                  pl.BlockSpec((1, LANES), lambda b, j: (0, 0))],
        out_specs=pl.BlockSpec((w, SWA_Q), cur),
        out_shape=jax.ShapeDtypeStruct((n, SWA_Q), F32),
        compiler_params=_params(2),
        name="swa",
    )(sinks, aq, akv, akv, cos_t, sin_t, cos_t, sin_t,
      jnp.tile(q_norm, rep)[None], jnp.tile(k_norm, rep)[None])


def _gla_kernel(qk_ref, v_ref, r_ref, sm_ref, wa_ref, ba_ref, gn_ref, o_ref, state_ref):
    @pl.when(pl.program_id(1) == 0)
    def _():
        state_ref[...] = jnp.zeros_like(state_ref)

    tc = qk_ref.shape[0]
    la = _log_sigmoid(_dot(sm_ref[...], wa_ref[...]) + ba_ref[...]) * (1.0 / GLA_GATE_NORM)
    tri = jnp.where(_chunk_tril(tc), 1.0, 0.0).astype(BF16)
    b = _dot_exact_lhs(tri, la)
    causal = _iota((CHUNK, CHUNK), 1) <= _iota((CHUNK, CHUNK), 0)
    for c in range(tc // CHUNK):
        rows = slice(c * CHUNK, (c + 1) * CHUNK)
        bc = b[rows]
        b_last = bc[CHUNK - 1:CHUNK]
        q = qk_ref[rows, :GLA_K] * (GLA_DK ** -0.5)
        k = qk_ref[rows, GLA_K:]
        qe = q * jnp.exp(bc)
        ke = k * jnp.exp(-bc)
        kd = k * jnp.exp(b_last - bc)
        dcol = jnp.transpose(jnp.broadcast_to(jnp.exp(b_last), (GLA_DV, GLA_K)))
        for h in range(GLA_HEADS):
            ks = slice(h * GLA_DK, (h + 1) * GLA_DK)
            vs = slice(h * GLA_DV, (h + 1) * GLA_DV)
            vh = v_ref[rows, vs]
            st = state_ref[h]
            att = jnp.where(causal, _dot_nt(qe[:, ks], ke[:, ks]), 0.0)
            o = _dot(att, vh) + _dot(qe[:, ks], st)
            state_ref[h] = dcol[ks] * st + _dot_tn(kd[:, ks], vh)
            o_ref[rows, vs] = _rms(o, gn_ref[...]) * _silu(r_ref[rows, vs])


def _gla(bqk, bv, br, small, wa2, ba, gn, batch, seq, tc=256):
    n = bqk.shape[0]
    nt = seq // tc
    blk = lambda b, j: (b * nt + j, 0)
    const = lambda b, j: (0, 0)
    wa_pad = jnp.zeros((LANES, GLA_K), F32).at[SMALL_LR:SMALL_LR + GLA_GATE_RANK].set(wa2)
    return pl.pallas_call(
        _gla_kernel,
        grid=(batch, nt),
        in_specs=[pl.BlockSpec((tc, 2 * GLA_K), blk),
                  pl.BlockSpec((tc, GLA_V), blk),
                  pl.BlockSpec((tc, GLA_V), blk),
                  pl.BlockSpec((tc, LANES), blk),
                  pl.BlockSpec((LANES, GLA_K), const),
                  pl.BlockSpec((1, GLA_K), const),
                  pl.BlockSpec((1, GLA_DV), const)],
        out_specs=pl.BlockSpec((tc, GLA_V), blk),
        out_shape=jax.ShapeDtypeStruct((n, GLA_V), F32),
        scratch_shapes=[pltpu.VMEM((GLA_HEADS, GLA_DK, GLA_DV), F32)],
        compiler_params=_params(2),
        name="gla",
    )(bqk, bv, br, small, wa_pad, ba[None], gn[None])


def _l2(x):
    return x * lax.rsqrt(jnp.sum(x * x, axis=-1, keepdims=True) + NORM_EPS)


def _gdn_kernel(x_ref, xp_ref, g_ref, sm_ref, cw_ref, al_ref, dt_ref, gn_ref, o_ref, state_ref):
    first = pl.program_id(1) == 0

    @pl.when(first)
    def _():
        state_ref[...] = jnp.zeros_like(state_ref)

    tc = x_ref.shape[0]
    halo = xp_ref.shape[0]
    xprev = jnp.where(first, 0.0, xp_ref[...])
    xcat = jnp.concatenate([xprev, x_ref[...]], axis=0)
    conv = cw_ref[GDN_CONV - 1:GDN_CONV] * xcat[halo:halo + tc]
    for t in range(1, GDN_CONV):
        conv = conv + cw_ref[GDN_CONV - 1 - t:GDN_CONV - t] * xcat[halo - t:halo - t + tc]
    qkv = _silu(conv)

    sm = sm_ref[...]
    beta_all = _sigmoid(sm)
    g_all = -jnp.exp(al_ref[...]) * _softplus(sm + dt_ref[...])
    tri = jnp.where(_chunk_tril(tc), 1.0, 0.0).astype(BF16)
    big_g = _dot_exact_lhs(tri, g_all)
    g_rows = jnp.transpose(big_g)

    r = _iota((CHUNK, CHUNK), 0)
    cidx = _iota((CHUNK, CHUNK), 1)
    causal = cidx <= r
    strict = cidx < r
    eye = jnp.where(cidx == r, 1.0, 0.0)

    units = [(c, h) for c in range(tc // CHUNK) for h in range(GDN_HEADS)]
    qs, ks, egs, g_lasts, gcs, decays, lows, rhss = [], [], [], [], [], [], [], []
    for c, h in units:
        rows = slice(c * CHUNK, (c + 1) * CHUNK)
        q = _l2(qkv[rows, h * GDN_DK:(h + 1) * GDN_DK]) * (GDN_DK ** -0.5)
        k = _l2(qkv[rows, GDN_K + h * GDN_DK:GDN_K + (h + 1) * GDN_DK])
        v = qkv[rows, 2 * GDN_K + h * GDN_DV:2 * GDN_K + (h + 1) * GDN_DV]
        beta = beta_all[rows, SMALL_BETA + h:SMALL_BETA + h + 1]
        gc = big_g[rows, SMALL_A + h:SMALL_A + h + 1]
        g_row = g_rows[SMALL_A + h:SMALL_A + h + 1, c * CHUNK:(c + 1) * CHUNK]
        decay = jnp.where(causal, jnp.exp(jnp.where(causal, gc - g_row, 0.0)), 0.0)
        eg = jnp.exp(gc)
        qs.append(q)
        ks.append(k)
        egs.append(eg)
        gcs.append(gc)
        g_lasts.append(gc[CHUNK - 1:CHUNK])
        decays.append(decay)
        lows.append(jnp.where(strict, beta * _dot_nt(k, k) * decay, 0.0))
        rhss.append(jnp.concatenate([v * beta, k * (beta * eg)], axis=-1))
    invs = [eye - low for low in lows]
    pws = lows
    for _ in range(int(math.log2(CHUNK)) - 1):
        pws = [_dot(pw, pw) for pw in pws]
        invs = [inv + _dot(inv, pw) for inv, pw in zip(invs, pws)]
    sols = [_dot(inv, rhs) for inv, rhs in zip(invs, rhss)]
    atts = [_dot_nt(q, k) * decay for q, k, decay in zip(qs, ks, decays)]

    for i, (c, h) in enumerate(units):
        rows = slice(c * CHUNK, (c + 1) * CHUNK)
        u = sols[i][:, :GDN_DV]
        wm = sols[i][:, GDN_DV:]
        q_dec = qs[i] * egs[i]
        k_dec = ks[i] * jnp.exp(g_lasts[i] - gcs[i])
        st = state_ref[h]
        v_new = u - _dot(wm, st)
        o = _dot(q_dec, st) + _dot(atts[i], v_new)
        state_ref[h] = jnp.exp(g_lasts[i]) * st + _dot_tn(k_dec, v_new)
        vs = slice(h * GDN_DV, (h + 1) * GDN_DV)
        o_ref[rows, vs] = _rms(o, gn_ref[...]) * _silu(g_ref[rows, vs])


def _gdn(cqkv, cg, small, conv_w, a_log, dt_bias, gn, batch, seq, tc=256, halo=8):
    n = cqkv.shape[0]
    nt = seq // tc
    blk = lambda b, j: (b * nt + j, 0)
    const = lambda b, j: (0, 0)
    prev = lambda b, j: (jnp.maximum((b * nt + j) * (tc // halo) - 1, 0), 0)
    al_row = jnp.zeros((1, LANES), F32).at[0, SMALL_A:SMALL_A + GDN_HEADS].set(a_log)
    dt_row = jnp.zeros((1, LANES), F32).at[0, SMALL_A:SMALL_A + GDN_HEADS].set(dt_bias)
    return pl.pallas_call(
        _gdn_kernel,
        grid=(batch, nt),
        in_specs=[pl.BlockSpec((tc, GDN_QKV), blk),
                  pl.BlockSpec((halo, GDN_QKV), prev),
                  pl.BlockSpec((tc, GDN_V), blk),
                  pl.BlockSpec((tc, LANES), blk),
                  pl.BlockSpec((GDN_CONV, GDN_QKV), const),
                  pl.BlockSpec((1, LANES), const),
                  pl.BlockSpec((1, LANES), const),
                  pl.BlockSpec((1, GDN_DV), const)],
        out_specs=pl.BlockSpec((tc, GDN_V), blk),
        out_shape=jax.ShapeDtypeStruct((n, GDN_V), F32),
        scratch_shapes=[pltpu.VMEM((GDN_HEADS, GDN_DK, GDN_DV), F32)],
        compiler_params=_params(2),
        name="gdn",
    )(cqkv, cqkv, cg, small, conv_w, al_row, dt_row, gn[None])


def _first_argmax(x, lane):
    m = jnp.max(x, axis=-1, keepdims=True)
    idx = jnp.min(jnp.where(x == m, lane, LANES), axis=-1, keepdims=True)
    return m, idx


def _merge_kernel(ya_ref, yb_ref, yc_ref, gt_ref, x_ref, wa_ref, wb_ref, wc_ref, wo_ref, fn_ref,
                  wr_ref, br_ref, x1_ref, h2_ref, ri_ref, rw_ref, cnt_ref, carry_ref):
    @pl.when(pl.program_id(0) == 0)
    def _():
        carry_ref[...] = jnp.zeros_like(carry_ref)

    d = D_MODEL
    merged = (_sigmoid(gt_ref[:, :d]) * _dot(ya_ref[...], wa_ref[...])
              + _sigmoid(gt_ref[:, d:2 * d]) * _dot(yb_ref[...], wb_ref[...])
              + _sigmoid(gt_ref[:, 2 * d:]) * _dot(yc_ref[...], wc_ref[...]))
    x1 = x_ref[...] + _dot(merged, wo_ref[...])
    x1_ref[...] = x1
    h2 = _rms(x1, fn_ref[...])
    h2_ref[...] = h2

    logits = _dot_hi(h2, wr_ref[...]) + br_ref[...]
    tm = logits.shape[0]
    lane = _iota((tm, LANES), 1)
    neg = -jnp.inf
    is_c = lane < N_GROUPS
    cm, g_idx = _first_argmax(jnp.where(is_c, logits, neg), lane)
    g_prob = 1.0 / jnp.sum(jnp.where(is_c, jnp.exp(logits - cm), 0.0), axis=-1, keepdims=True)
    sel = (lane >= N_GROUPS) & (((lane - N_GROUPS) // EXPERTS_PER_GROUP) == g_idx)
    fm = jnp.max(jnp.where(sel, logits, neg), axis=-1, keepdims=True)
    ef = jnp.where(sel, jnp.exp(logits - fm), 0.0)
    p1, i1 = _first_argmax(jnp.where(sel, ef, neg), lane)
    p2, i2 = _first_argmax(jnp.where(sel & (lane != i1), ef, neg), lane)
    w1 = g_prob * p1 / (p1 + p2)
    w2 = g_prob * p2 / (p1 + p2)
    e1 = i1 - N_GROUPS
    e2 = i2 - N_GROUPS

    oh = jnp.where((lane == e1) | (lane == e2 + N_EXPERTS), 1.0, 0.0)
    stril = jnp.where(_iota((tm, tm), 1) < _iota((tm, tm), 0), 1.0, 0.0).astype(BF16)
    before = jnp.dot(stril, oh.astype(BF16), preferred_element_type=F32)
    tot = jnp.sum(oh, axis=0, keepdims=True)
    tot_first = jnp.where(lane[:1] < N_EXPERTS, tot, 0.0)
    carry = carry_ref[...]
    base = carry + pltpu.roll(carry + tot_first, N_EXPERTS, 1)
    ranks = oh * (before + base)
    rank1 = jnp.sum(jnp.where(lane < N_EXPERTS, ranks, 0.0), axis=-1, keepdims=True)
    rank2 = jnp.sum(jnp.where(lane < N_EXPERTS, 0.0, ranks), axis=-1, keepdims=True)
    new_carry = carry + tot_first + pltpu.roll(tot - tot_first, LANES - N_EXPERTS, 1)
    carry_ref[...] = new_carry
    cnt_ref[...] = jnp.broadcast_to(new_carry, cnt_ref.shape).astype(I32)

    ri_ref[...] = jnp.where(lane == 0, e1, jnp.where(lane == 1, e2, jnp.where(
        lane == 2, rank1.astype(I32), jnp.where(lane == 3, rank2.astype(I32), 0))))
    rw_ref[...] = jnp.where(lane == 0, w1, jnp.where(lane == 1, w2, 0.0))


def _merge(ya, yb, yc, gates, x, wa, wb, wc, wo, fn, wr, br, tm=256):
    n = x.shape[0]
    d = D_MODEL
    blk = lambda i: (i, 0)
    const = lambda i: (0, 0)
    return pl.pallas_call(
        _merge_kernel,
        grid=(n // tm,),
        in_specs=[pl.BlockSpec((tm, SWA_Q), blk), pl.BlockSpec((tm, GLA_V), blk),
                  pl.BlockSpec((tm, GDN_V), blk), pl.BlockSpec((tm, 3 * d), blk),
                  pl.BlockSpec((tm, d), blk),
                  pl.BlockSpec((SWA_Q, d), const), pl.BlockSpec((GLA_V, d), const),
                  pl.BlockSpec((GDN_V, d), const), pl.BlockSpec((d, d), const),
                  pl.BlockSpec((1, d), const), pl.BlockSpec((d, LANES), const),
                  pl.BlockSpec((1, LANES), const)],
        out_specs=[pl.BlockSpec((tm, d), blk), pl.BlockSpec((tm, d), blk),
                   pl.BlockSpec((tm, LANES), blk), pl.BlockSpec((tm, LANES), blk),
                   pl.BlockSpec((8, LANES), const)],
        out_shape=[jax.ShapeDtypeStruct((n, d), F32), jax.ShapeDtypeStruct((n, d), F32),
                   jax.ShapeDtypeStruct((n, LANES), I32), jax.ShapeDtypeStruct((n, LANES), F32),
                   jax.ShapeDtypeStruct((8, LANES), I32)],
        scratch_shapes=[pltpu.VMEM((1, LANES), F32)],
        compiler_params=_params(1),
        name="merge_route",
    )(ya, yb, yc, gates, x, wa, wb, wc, wo, fn, wr, br)


def _dispatch_kernel(slot_ref, h_ref, zero_ref, xs_ref, sems):
    del zero_ref
    i = pl.program_id(0)
    td = slot_ref.shape[-1] // TOP_K

    def rows(step, start):
        sem = sems.at[step % 2]

        def body(t, carry):
            for k in range(TOP_K):
                s = slot_ref[0, 0, TOP_K * t + k]
                cp = pltpu.make_async_copy(h_ref.at[pl.ds(step * td + t, 1)], xs_ref.at[pl.ds(s, 1)], sem)
                if start:
                    cp.start()
                else:
                    cp.wait()
            return carry

        lax.fori_loop(0, td, body, 0, unroll=4 if start else 8)

    rows(i, True)

    @pl.when(i > 0)
    def _():
        rows(i - 1, False)

    @pl.when(i == pl.num_programs(0) - 1)
    def _():
        rows(i, False)


def _dispatch(h2, slots, n_slots, td=256):
    n = h2.shape[0]
    zeros = jnp.zeros((n_slots, D_MODEL), F32)
    return pl.pallas_call(
        _dispatch_kernel,
        grid=(n // td,),
        in_specs=[pl.BlockSpec((1, 1, TOP_K * td), lambda i: (i, 0, 0), memory_space=pltpu.SMEM),
                  pl.BlockSpec(memory_space=pl.ANY),
                  pl.BlockSpec(memory_space=pl.ANY)],
        out_specs=pl.BlockSpec(memory_space=pl.ANY),
        out_shape=jax.ShapeDtypeStruct((n_slots, D_MODEL), F32),
        scratch_shapes=[pltpu.SemaphoreType.DMA((2,))],
        input_output_aliases={2: 0},
        compiler_params=_params(1),
        name="dispatch",
    )(slots.reshape(n // td, 1, TOP_K * td), h2, zeros)


def _expert_kernel(be_ref, xs_ref, wg_ref, wu_ref, wd_ref, ys_ref, wg_s, wu_s, wd_s):
    i = pl.program_id(0)
    new_expert = jnp.logical_or(i == 0, be_ref[i] != be_ref[jnp.maximum(i - 1, 0)])

    @pl.when(new_expert)
    def _():
        wg_s[...] = wg_ref[0].astype(BF16)
        wu_s[...] = wu_ref[0].astype(BF16)
        wd_s[...] = wd_ref[0].astype(BF16)

    x = xs_ref[...].astype(BF16)
    act = _silu(jnp.dot(x, wg_s[...], preferred_element_type=F32)) * jnp.dot(x, wu_s[...], preferred_element_type=F32)
    ys_ref[...] = jnp.dot(act.astype(BF16), wd_s[...], preferred_element_type=F32)


def _experts(xs, block_expert, wg, wu, wd):
    n_slots = xs.shape[0]
    tb = MOE_ROWS
    grid_spec = pltpu.PrefetchScalarGridSpec(
        num_scalar_prefetch=1,
        grid=(n_slots // tb,),
        in_specs=[pl.BlockSpec((tb, D_MODEL), lambda i, be: (i, 0)),
                  pl.BlockSpec((1, D_MODEL, EXPERT_FF), lambda i, be: (be[i], 0, 0)),
                  pl.BlockSpec((1, D_MODEL, EXPERT_FF), lambda i, be: (be[i], 0, 0)),
                  pl.BlockSpec((1, EXPERT_FF, D_MODEL), lambda i, be: (be[i], 0, 0))],
        out_specs=pl.BlockSpec((tb, D_MODEL), lambda i, be: (i, 0)),
        scratch_shapes=[pltpu.VMEM((D_MODEL, EXPERT_FF), BF16), pltpu.VMEM((D_MODEL, EXPERT_FF), BF16),
                        pltpu.VMEM((EXPERT_FF, D_MODEL), BF16)],
    )
    return pl.pallas_call(
        _expert_kernel,
        grid_spec=grid_spec,
        out_shape=jax.ShapeDtypeStruct((n_slots, D_MODEL), F32),
        compiler_params=_params(1),
        name="experts",
    )(block_expert, xs, wg, wu, wd)


def _combine_kernel(slot_ref, next_slot_ref, x1_ref, rw_ref, p_ref, ys_ref, pn_ref, wpg_ref, wpl_ref, o_ref,
                    ybuf, sems):
    i = pl.program_id(0)
    tp = x1_ref.shape[0]
    cur = i % 2

    def rows(sref, buf, start):
        def body(t, carry):
            for k in range(TOP_K):
                s = sref[0, 0, TOP_K * t + k]
                cp = pltpu.make_async_copy(ys_ref.at[pl.ds(s, 1)], ybuf.at[buf, k, pl.ds(t, 1)], sems.at[buf])
                if start:
                    cp.start()
                else:
                    cp.wait()
            return carry

        lax.fori_loop(0, tp, body, 0, unroll=4 if start else 8)

    @pl.when(i == 0)
    def _():
        rows(slot_ref, 0, True)

    @pl.when(i + 1 < pl.num_programs(0))
    def _():
        rows(next_slot_ref, 1 - cur, True)

    rows(slot_ref, cur, False)

    rw = rw_ref[...]
    x2 = x1_ref[...] + rw[:, 0:1] * ybuf[cur, 0] + rw[:, 1:2] * ybuf[cur, 1]
    h3 = _rms(x2, pn_ref[...])
    gate = _sigmoid(_dot(h3, wpg_ref[...]))
    o_ref[...] = x2 + gate * _dot(p_ref[...], wpl_ref[...])


def _combine_ple(x1, rw, slots, ys, p_i, pn, wpg, wpl, tp=256):
    n = x1.shape[0]
    d = D_MODEL
    blk = lambda i: (i, 0)
    const = lambda i: (0, 0)
    nsteps = n // tp
    slot_blocks = slots.reshape(nsteps, 1, TOP_K * tp)
    return pl.pallas_call(
        _combine_kernel,
        grid=(nsteps,),
        in_specs=[pl.BlockSpec((1, 1, TOP_K * tp), lambda i: (i, 0, 0), memory_space=pltpu.SMEM),
                  pl.BlockSpec((1, 1, TOP_K * tp), lambda i: (jnp.minimum(i + 1, nsteps - 1), 0, 0),
                               memory_space=pltpu.SMEM),
                  pl.BlockSpec((tp, d), blk), pl.BlockSpec((tp, LANES), blk),
                  pl.BlockSpec((tp, PLE_DIM), blk),
                  pl.BlockSpec(memory_space=pl.ANY),
                  pl.BlockSpec((1, d), const), pl.BlockSpec((d, d), const),
                  pl.BlockSpec((PLE_DIM, d), const)],
        out_specs=pl.BlockSpec((tp, d), blk),
        out_shape=jax.ShapeDtypeStruct((n, d), F32),
        scratch_shapes=[pltpu.VMEM((2, TOP_K, tp, d), F32), pltpu.SemaphoreType.DMA((2,))],
        compiler_params=_params(1),
        name="combine_ple",
    )(slot_blocks, slot_blocks, x1, rw, p_i, ys, pn, wpg, wpl)


def _pack_w_in(w_in):
    o = IN_OFFSETS
    sec = lambda i: w_in[:, o[i]:o[i + 1]]
    pad = jnp.zeros((D_MODEL, LANES - GLA_GATE_RANK - 2 * GDN_HEADS), w_in.dtype)
    cols = [sec(0), sec(1), sec(2), sec(3), sec(4), sec(5), sec(7), sec(8), sec(11), sec(12),
            sec(6), sec(9), sec(10), pad]
    return jnp.concatenate(cols, axis=1).astype(BF16)


def _layer(x, p_i, cos_t, sin_t, batch, seq, attn_norm, w_in, q_norm, k_norm, sinks, gla_wa2, gla_ba,
           gla_norm, gdn_conv, gdn_a_log, gdn_dt_bias, gdn_norm, w_br_a, w_br_b, w_br_c, w_o,
           ffn_norm, w_coarse, b_coarse, w_fine, b_fine, w_gate_e, w_up_e, w_down_e,
           ple_norm, w_ple_gate, w_ple):
    n = x.shape[0]
    aq, akv, bqk, bv, br, cqkv, cg, gates, small = _inproj(x, attn_norm[None], _pack_w_in(w_in))
    ya = _swa(aq, akv, cos_t, sin_t, q_norm, k_norm, sinks, batch, seq)
    yb = _gla(bqk, bv, br, small, gla_wa2, gla_ba, gla_norm, batch, seq)
    yc = _gdn(cqkv, cg, small, gdn_conv, gdn_a_log, gdn_dt_bias, gdn_norm, batch, seq)

    wr = jnp.zeros((D_MODEL, LANES), F32).at[:, :N_GROUPS].set(w_coarse)
    wr = wr.at[:, N_GROUPS:N_GROUPS + N_EXPERTS].set(w_fine)
    brow = jnp.zeros((1, LANES), F32).at[0, :N_GROUPS].set(b_coarse)
    brow = brow.at[0, N_GROUPS:N_GROUPS + N_EXPERTS].set(b_fine)
    x1, h2, ri, rw, cnt = _merge(ya, yb, yc, gates, x, w_br_a.astype(BF16), w_br_b.astype(BF16),
                                 w_br_c.astype(BF16), w_o.astype(BF16), ffn_norm[None], wr, brow)

    counts = cnt[0, :N_EXPERTS]
    padded = (counts + MOE_ROWS - 1) // MOE_ROWS * MOE_ROWS
    pad_end = jnp.cumsum(padded)
    pad_start = pad_end - padded
    slots = (pad_start[ri[:, :TOP_K]] + ri[:, TOP_K:2 * TOP_K]).astype(I32)
    n_slots = n * TOP_K + N_EXPERTS * MOE_ROWS
    block_start = jnp.arange(n_slots // MOE_ROWS, dtype=I32) * MOE_ROWS
    block_expert = jnp.minimum(jnp.sum((pad_end[None, :] <= block_start[:, None]).astype(I32), axis=1),
                               N_EXPERTS - 1).astype(I32)

    xs = _dispatch(h2, slots, n_slots)
    ys = _experts(xs, block_expert, w_gate_e, w_up_e, w_down_e)
    return _combine_ple(x1, rw, slots, ys, p_i, ple_norm[None], w_ple_gate.astype(BF16),
                        w_ple.astype(BF16))


def kernel(x, p, positions, attn_norm, w_in, q_norm, k_norm, sinks, gla_wa2, gla_ba, gla_norm, gdn_conv, gdn_a_log, gdn_dt_bias, gdn_norm, w_br_a, w_br_b, w_br_c, w_o, ffn_norm, w_coarse, b_coarse, w_fine, b_fine, w_gate_e, w_up_e, w_down_e, ple_norm, w_ple_gate, w_ple):
    batch, seq, d = x.shape
    n = batch * seq
    depth = p.shape[0]
    cos_t, sin_t = _rope_tables(positions)
    xf = x.reshape(n, d)
    pf = p.reshape(depth, n, p.shape[-1])
    per_layer = (attn_norm, w_in, q_norm, k_norm, sinks, gla_wa2, gla_ba, gla_norm, gdn_conv, gdn_a_log,
                 gdn_dt_bias, gdn_norm, w_br_a, w_br_b, w_br_c, w_o, ffn_norm, w_coarse, b_coarse,
                 w_fine, b_fine, w_gate_e, w_up_e, w_down_e, ple_norm, w_ple_gate, w_ple)
    for i in range(depth):
        xf = _layer(xf, pf[i], cos_t, sin_t, batch, seq, *[a[i] for a in per_layer])
    return xf.reshape(batch, seq, d)
```

```python
import functools
import math

import numpy as np
import jax
import jax.numpy as jnp
from jax import lax
from jax.experimental import pallas as pl
from jax.experimental.pallas import tpu as pltpu

F32 = jnp.float32
BF16 = jnp.bfloat16
I32 = jnp.int32

D_MODEL = 1024
PLE_DIM = 256
NORM_EPS = 1e-6
MASK_VALUE = -1e30

SWA_Q_HEADS = 8
SWA_KV_HEADS = 2
SWA_HEAD_DIM = 64
SWA_WINDOW = 128
ROT_DIM = SWA_HEAD_DIM // 4
ROPE_THETA = 500000.0

GLA_HEADS = 4
GLA_DK = 64
GLA_DV = 128
GLA_GATE_RANK = 16
GLA_GATE_NORM = 16.0
CHUNK = 64

GDN_HEADS = 4
GDN_DK = 128
GDN_DV = 128
GDN_CONV = 4

N_GROUPS = 4
EXPERTS_PER_GROUP = 8
N_EXPERTS = N_GROUPS * EXPERTS_PER_GROUP
EXPERT_FF = 512
TOP_K = 2

SWA_Q = SWA_Q_HEADS * SWA_HEAD_DIM
SWA_KV = SWA_KV_HEADS * SWA_HEAD_DIM
GLA_K = GLA_HEADS * GLA_DK
GLA_V = GLA_HEADS * GLA_DV
GDN_K = GDN_HEADS * GDN_DK
GDN_V = GDN_HEADS * GDN_DV
GDN_QKV = 2 * GDN_K + GDN_V
IN_SPLITS = (SWA_Q, SWA_KV, SWA_KV, GLA_K, GLA_K, GLA_V, GLA_GATE_RANK, GLA_V,
             GDN_QKV, GDN_HEADS, GDN_HEADS, GDN_V, 3 * D_MODEL)
IN_OFFSETS = tuple(int(o) for o in np.cumsum((0,) + IN_SPLITS))

LANES = 128
SMALL_LR = 0
SMALL_BETA = GLA_GATE_RANK
SMALL_A = GLA_GATE_RANK + GDN_HEADS

OUT_WIDTHS = (SWA_Q, 2 * SWA_KV, 2 * GLA_K, GLA_V, GLA_V, GDN_QKV, GDN_V, 3 * D_MODEL, LANES)

MOE_ROWS = 256
VMEM_LIMIT = 56 * 1024 * 1024


def _params(n_axes):
    return pltpu.CompilerParams(dimension_semantics=("arbitrary",) * n_axes,
                                vmem_limit_bytes=VMEM_LIMIT)


def _dot(a, b):
    return jnp.dot(a.astype(BF16), b.astype(BF16), preferred_element_type=F32)


def _dot_nt(a, b):
    return lax.dot_general(a.astype(BF16), b.astype(BF16), (((1,), (1,)), ((), ())),
                           preferred_element_type=F32)


def _dot_tn(a, b):
    return lax.dot_general(a.astype(BF16), b.astype(BF16), (((0,), (0,)), ((), ())),
                           preferred_element_type=F32)


def _split2(x):
    hi = x.astype(BF16)
    lo = (x - hi.astype(F32)).astype(BF16)
    return hi, lo


def _dot_exact_lhs(a, x):
    hi = x.astype(BF16)
    r = x - hi.astype(F32)
    mid = r.astype(BF16)
    lo = (r - mid.astype(F32)).astype(BF16)
    a = a.astype(BF16)
    return (jnp.dot(a, hi, preferred_element_type=F32) + jnp.dot(a, mid, preferred_element_type=F32)
            + jnp.dot(a, lo, preferred_element_type=F32))


def _dot_hi(a, b):
    ah, al = _split2(a)
    bh, bl = _split2(b)
    return (jnp.dot(ah, bh, preferred_element_type=F32) + jnp.dot(ah, bl, preferred_element_type=F32)
            + jnp.dot(al, bh, preferred_element_type=F32))


def _rms(x, g):
    return x * lax.rsqrt(jnp.mean(x * x, axis=-1, keepdims=True) + NORM_EPS) * g


def _sigmoid(x):
    return 1.0 / (1.0 + jnp.exp(-x))


def _silu(x):
    return x * _sigmoid(x)


def _softplus(x):
    return jnp.maximum(x, 0.0) + jnp.log(1.0 + jnp.exp(-jnp.abs(x)))


def _log_sigmoid(x):
    return -_softplus(-x)


def _iota(shape, axis):
    return lax.broadcasted_iota(I32, shape, axis)


def _chunk_tril(n, strict=False):
    r = _iota((n, n), 0)
    c = _iota((n, n), 1)
    same = (r // CHUNK) == (c // CHUNK)
    return same & ((c < r) if strict else (c <= r))


def _inproj_kernel(x_ref, g_ref, w_ref, *out_refs):
    h = _rms(x_ref[...], g_ref[...]).astype(BF16)
    off = 0
    for o_ref in out_refs:
        wd = o_ref.shape[-1]
        o_ref[...] = jnp.dot(h, w_ref[:, off:off + wd], preferred_element_type=F32)
        off += wd


def _inproj(x, g, w, tm=256):
    n = x.shape[0]
    nc = w.shape[1]
    return pl.pallas_call(
        _inproj_kernel,
        grid=(n // tm,),
        in_specs=[pl.BlockSpec((tm, D_MODEL), lambda i: (i, 0)),
                  pl.BlockSpec((1, D_MODEL), lambda i: (0, 0)),
                  pl.BlockSpec((D_MODEL, nc), lambda i: (0, 0), pipeline_mode=pl.Buffered(1))],
        out_specs=[pl.BlockSpec((tm, wd), lambda i: (i, 0)) for wd in OUT_WIDTHS],
        out_shape=[jax.ShapeDtypeStruct((n, wd), F32) for wd in OUT_WIDTHS],
        compiler_params=_params(1),
        name="inproj",
    )(x, g, w)


def _rope_kernel(pos_ref, invf_ref, cos_ref, sin_ref):
    ang = pos_ref[...].astype(F32) * invf_ref[...]
    d = _iota(ang.shape, 1) % SWA_HEAD_DIM
    half = ROT_DIM // 2
    cos_ref[...] = jnp.where(d < ROT_DIM, jnp.cos(ang), 1.0)
    s = jnp.sin(ang)
    sin_ref[...] = jnp.where(d < half, -s, jnp.where(d < ROT_DIM, s, 0.0))


def _rope_tables(positions, tm=1024):
    n = positions.size
    tm = min(tm, n)
    inv_freq = 1.0 / (ROPE_THETA ** (jnp.arange(0, ROT_DIM, 2, dtype=F32) / ROT_DIM))
    lane_freq = jnp.tile(jnp.concatenate([inv_freq, inv_freq, jnp.zeros(SWA_HEAD_DIM - ROT_DIM, F32)]),
                         LANES // SWA_HEAD_DIM)[None]
    return pl.pallas_call(
        _rope_kernel,
        grid=(n // tm,),
        in_specs=[pl.BlockSpec((tm, 1), lambda i: (i, 0)),
                  pl.BlockSpec((1, LANES), lambda i: (0, 0))],
        out_specs=[pl.BlockSpec((tm, LANES), lambda i: (i, 0))] * 2,
        out_shape=[jax.ShapeDtypeStruct((n, LANES), F32)] * 2,
        compiler_params=_params(1),
        name="rope_tables",
    )(positions.reshape(n, 1), lane_freq)


def _norm_rope(x, gain, cos, sin):
    lane = _iota(x.shape, 1)
    sq = x * x
    s0 = jnp.sum(jnp.where(lane < SWA_HEAD_DIM, sq, 0.0), axis=-1, keepdims=True)
    s1 = jnp.sum(jnp.where(lane < SWA_HEAD_DIM, 0.0, sq), axis=-1, keepdims=True)
    ms = jnp.where(lane < SWA_HEAD_DIM, s0, s1) * (1.0 / SWA_HEAD_DIM)
    xn = x * lax.rsqrt(ms + NORM_EPS) * gain
    half = ROT_DIM // 2
    up = pltpu.roll(xn, LANES - half, 1)
    down = pltpu.roll(xn, half, 1)
    partner = jnp.where((lane % SWA_HEAD_DIM) < half, up, down)
    return xn * cos + partner * sin


def _swa_kernel(sink_ref, q_ref, kv_ref, cos_ref, sin_ref, qn_ref, kn_ref, o_ref, kprev_ref, vprev_ref):
    j = pl.program_id(1)
    w = SWA_WINDOW
    hd = SWA_HEAD_DIM
    group = SWA_Q_HEADS // SWA_KV_HEADS

    @pl.when(j == 0)
    def _():
        kprev_ref[...] = jnp.zeros_like(kprev_ref)
        vprev_ref[...] = jnp.zeros_like(vprev_ref)

    cos = cos_ref[...]
    sin = sin_ref[...]
    kc = _norm_rope(kv_ref[:, :SWA_KV], kn_ref[...], cos, sin).astype(BF16)
    vc = kv_ref[:, SWA_KV:].astype(BF16)
    k_all = jnp.concatenate([kprev_ref[...], kc], axis=0)
    v_all = jnp.concatenate([vprev_ref[...], vc], axis=0)
    kprev_ref[...] = kc
    vprev_ref[...] = vc

    rows = group * w
    qi = _iota((rows, 2 * w), 0) % w
    kj = _iota((rows, 2 * w), 1)
    first_key = jnp.where(j > 0, 0, w)
    mask = (kj > qi) & (kj <= qi + w) & (kj >= first_key)
    head_of_row = _iota((rows, 1), 0) // w
    qrs = [_norm_rope(q_ref[:, c * LANES:(c + 1) * LANES], qn_ref[...], cos, sin)
           for c in range(SWA_Q // LANES)]
    per_block = LANES // hd
    qgs, sinks = [], []
    for g in range(SWA_KV_HEADS):
        heads = range(g * group, (g + 1) * group)
        qgs.append(jnp.concatenate(
            [qrs[h // per_block][:, (h % per_block) * hd:(h % per_block + 1) * hd] for h in heads], axis=0))
        sk = jnp.full((rows, 1), sink_ref[g * group], F32)
        for i in range(1, group):
            sk = jnp.where(head_of_row == i, sink_ref[g * group + i], sk)
        sinks.append(sk)
    ss = [jnp.where(mask, _dot_nt(qgs[g], k_all[:, g * hd:(g + 1) * hd]) * (hd ** -0.5), MASK_VALUE)
          for g in range(SWA_KV_HEADS)]
    ms = [jnp.maximum(jnp.max(s, axis=-1, keepdims=True), sk) for s, sk in zip(ss, sinks)]
    es = [jnp.exp(s - m) for s, m in zip(ss, ms)]
    dens = [jnp.sum(e, axis=-1, keepdims=True) + jnp.exp(sk - m) for e, sk, m in zip(es, sinks, ms)]
    for g in range(SWA_KV_HEADS):
        og = _dot(es[g], v_all[:, g * hd:(g + 1) * hd]) / dens[g]
        for i in range(group):
            h = g * group + i
            o_ref[:, h * hd:(h + 1) * hd] = og[i * w:(i + 1) * w]


def _swa(aq, akv, cos_t, sin_t, q_norm, k_norm, sinks, batch, seq):
    n = aq.shape[0]
    w = SWA_WINDOW
    nq = seq // w
    cur = lambda b, j: (b * nq + j, 0)
    rep = LANES // SWA_HEAD_DIM
    return pl.pallas_call(
        _swa_kernel,
        grid=(batch, nq),
        in_specs=[pl.BlockSpec(memory_space=pltpu.SMEM),
                  pl.BlockSpec((w, SWA_Q), cur),
                  pl.BlockSpec((w, 2 * SWA_KV), cur),
                  pl.BlockSpec((w, LANES), cur),
                  pl.BlockSpec((w, LANES), cur),
                  pl.BlockSpec((1, LANES), lambda b, j: (0, 0)),
                  pl.BlockSpec((1, LANES), lambda b, j: (0, 0))],
        out_specs=pl.BlockSpec((w, SWA_Q), cur),
        out_shape=jax.ShapeDtypeStruct((n, SWA_Q), F32),
        scratch_shapes=[pltpu.VMEM((w, SWA_KV), BF16), pltpu.VMEM((w, SWA_KV), BF16)],
        compiler_params=_params(2),
        name="swa",
    )(sinks, aq, akv, cos_t, sin_t, jnp.tile(q_norm, rep)[None], jnp.tile(k_norm, rep)[None])


def _gla_kernel(qk_ref, v_ref, r_ref, sm_ref, wa_ref, ba_ref, gn_ref, o_ref, state_ref):
    @pl.when(pl.program_id(1) == 0)
    def _():
        state_ref[...] = jnp.zeros_like(state_ref)

    tc = qk_ref.shape[0]
    la = _log_sigmoid(_dot(sm_ref[...], wa_ref[...]) + ba_ref[...]) * (1.0 / GLA_GATE_NORM)
    tri = jnp.where(_chunk_tril(tc), 1.0, 0.0).astype(BF16)
    b = _dot_exact_lhs(tri, la)
    causal = _iota((CHUNK, CHUNK), 1) <= _iota((CHUNK, CHUNK), 0)
    n_chunks = tc // CHUNK
    qes, dcols, intra, upd = [], [], {}, {}
    for c in range(n_chunks):
        rows = slice(c * CHUNK, (c + 1) * CHUNK)
        bc = b[rows]
        b_last = bc[CHUNK - 1:CHUNK]
        k = qk_ref[rows, GLA_K:]
        qe = qk_ref[rows, :GLA_K] * (GLA_DK ** -0.5) * jnp.exp(bc)
        ke = k * jnp.exp(-bc)
        kd = k * jnp.exp(b_last - bc)
        qes.append(qe)
        dcols.append(jnp.transpose(jnp.broadcast_to(jnp.exp(b_last), (GLA_DV, GLA_K))))
        for h in range(GLA_HEADS):
            ks = slice(h * GLA_DK, (h + 1) * GLA_DK)
            vh = v_ref[rows, h * GLA_DV:(h + 1) * GLA_DV]
            att = jnp.where(causal, _dot_nt(qe[:, ks], ke[:, ks]), 0.0)
            intra[c, h] = _dot(att, vh)
            upd[c, h] = _dot_tn(kd[:, ks], vh)
    for h in range(GLA_HEADS):
        ks = slice(h * GLA_DK, (h + 1) * GLA_DK)
        vs = slice(h * GLA_DV, (h + 1) * GLA_DV)
        st = state_ref[h]
        for c in range(n_chunks):
            rows = slice(c * CHUNK, (c + 1) * CHUNK)
            o = intra[c, h] + _dot(qes[c][:, ks], st)
            st = dcols[c][ks] * st + upd[c, h]
            o_ref[rows, vs] = _rms(o, gn_ref[...]) * _silu(r_ref[rows, vs])
        state_ref[h] = st


def _gla(bqk, bv, br, small, wa2, ba, gn, batch, seq, tc=256):
    n = bqk.shape[0]
    nt = seq // tc
    blk = lambda b, j: (b * nt + j, 0)
    const = lambda b, j: (0, 0)
    wa_pad = jnp.zeros((LANES, GLA_K), F32).at[SMALL_LR:SMALL_LR + GLA_GATE_RANK].set(wa2)
    return pl.pallas_call(
        _gla_kernel,
        grid=(batch, nt),
        in_specs=[pl.BlockSpec((tc, 2 * GLA_K), blk),
                  pl.BlockSpec((tc, GLA_V), blk),
                  pl.BlockSpec((tc, GLA_V), blk),
                  pl.BlockSpec((tc, LANES), blk),
                  pl.BlockSpec((LANES, GLA_K), const),
                  pl.BlockSpec((1, GLA_K), const),
                  pl.BlockSpec((1, GLA_DV), const)],
        out_specs=pl.BlockSpec((tc, GLA_V), blk),
        out_shape=jax.ShapeDtypeStruct((n, GLA_V), F32),
        scratch_shapes=[pltpu.VMEM((GLA_HEADS, GLA_DK, GLA_DV), F32)],
        compiler_params=_params(2),
        name="gla",
    )(bqk, bv, br, small, wa_pad, ba[None], gn[None])


def _l2(x):
    return x * lax.rsqrt(jnp.sum(x * x, axis=-1, keepdims=True) + NORM_EPS)


def _gdn_kernel(x_ref, xp_ref, g_ref, sm_ref, cw_ref, al_ref, dt_ref, gn_ref, o_ref, state_ref):
    first = pl.program_id(1) == 0

    @pl.when(first)
    def _():
        state_ref[...] = jnp.zeros_like(state_ref)

    tc = x_ref.shape[0]
    halo = xp_ref.shape[0]
    xprev = jnp.where(first, 0.0, xp_ref[...])
    xcat = jnp.concatenate([xprev, x_ref[...]], axis=0)
    conv = cw_ref[GDN_CONV - 1:GDN_CONV] * xcat[halo:halo + tc]
    for t in range(1, GDN_CONV):
        conv = conv + cw_ref[GDN_CONV - 1 - t:GDN_CONV - t] * xcat[halo - t:halo - t + tc]
    qkv = _silu(conv)

    sm = sm_ref[...]
    beta_all = _sigmoid(sm)
    g_all = -jnp.exp(al_ref[...]) * _softplus(sm + dt_ref[...])
    tri = jnp.where(_chunk_tril(tc), 1.0, 0.0).astype(BF16)
    big_g = _dot_exact_lhs(tri, g_all)
    g_rows = jnp.transpose(big_g)

    r = _iota((CHUNK, CHUNK), 0)
    cidx = _iota((CHUNK, CHUNK), 1)
    causal = cidx <= r
    strict = cidx < r
    eye = jnp.where(cidx == r, 1.0, 0.0)

    units = [(c, h) for c in range(tc // CHUNK) for h in range(GDN_HEADS)]
    qs, ks, egs, g_lasts, gcs, decays, lows, rhss = [], [], [], [], [], [], [], []
    for c, h in units:
        rows = slice(c * CHUNK, (c + 1) * CHUNK)
        q = _l2(qkv[rows, h * GDN_DK:(h + 1) * GDN_DK]) * (GDN_DK ** -0.5)
        k = _l2(qkv[rows, GDN_K + h * GDN_DK:GDN_K + (h + 1) * GDN_DK])
        v = qkv[rows, 2 * GDN_K + h * GDN_DV:2 * GDN_K + (h + 1) * GDN_DV]
        beta = beta_all[rows, SMALL_BETA + h:SMALL_BETA + h + 1]
        gc = big_g[rows, SMALL_A + h:SMALL_A + h + 1]
        g_row = g_rows[SMALL_A + h:SMALL_A + h + 1, c * CHUNK:(c + 1) * CHUNK]
        decay = jnp.where(causal, jnp.exp(jnp.where(causal, gc - g_row, 0.0)), 0.0)
        eg = jnp.exp(gc)
        qs.append(q)
        ks.append(k)
        egs.append(eg)
        gcs.append(gc)
        g_lasts.append(gc[CHUNK - 1:CHUNK])
        decays.append(decay)
        lows.append(jnp.where(strict, beta * _dot_nt(k, k) * decay, 0.0))
        rhss.append(jnp.concatenate([v * beta, k * (beta * eg)], axis=-1))
    invs = [eye - low for low in lows]
    pws = lows
    for _ in range(int(math.log2(CHUNK)) - 1):
        pws = [_dot(pw, pw) for pw in pws]
        invs = [inv + _dot(inv, pw) for inv, pw in zip(invs, pws)]
    sols = [_dot(inv, rhs) for inv, rhs in zip(invs, rhss)]
    atts = [_dot_nt(q, k) * decay for q, k, decay in zip(qs, ks, decays)]

    for i, (c, h) in enumerate(units):
        rows = slice(c * CHUNK, (c + 1) * CHUNK)
        u = sols[i][:, :GDN_DV]
        wm = sols[i][:, GDN_DV:]
        q_dec = qs[i] * egs[i]
        k_dec = ks[i] * jnp.exp(g_lasts[i] - gcs[i])
        st = state_ref[h]
        v_new = u - _dot(wm, st)
        o = _dot(q_dec, st) + _dot(atts[i], v_new)
        state_ref[h] = jnp.exp(g_lasts[i]) * st + _dot_tn(k_dec, v_new)
        vs = slice(h * GDN_DV, (h + 1) * GDN_DV)
        o_ref[rows, vs] = _rms(o, gn_ref[...]) * _silu(g_ref[rows, vs])


def _gdn(cqkv, cg, small, conv_w, a_log, dt_bias, gn, batch, seq, tc=256, halo=8):
    n = cqkv.shape[0]
    nt = seq // tc
    blk = lambda b, j: (b * nt + j, 0)
    const = lambda b, j: (0, 0)
    prev = lambda b, j: (jnp.maximum((b * nt + j) * (tc // halo) - 1, 0), 0)
    al_row = jnp.zeros((1, LANES), F32).at[0, SMALL_A:SMALL_A + GDN_HEADS].set(a_log)
    dt_row = jnp.zeros((1, LANES), F32).at[0, SMALL_A:SMALL_A + GDN_HEADS].set(dt_bias)
    return pl.pallas_call(
        _gdn_kernel,
        grid=(batch, nt),
        in_specs=[pl.BlockSpec((tc, GDN_QKV), blk),
                  pl.BlockSpec((halo, GDN_QKV), prev),
                  pl.BlockSpec((tc, GDN_V), blk),
                  pl.BlockSpec((tc, LANES), blk),
                  pl.BlockSpec((GDN_CONV, GDN_QKV), const),
                  pl.BlockSpec((1, LANES), const),
                  pl.BlockSpec((1, LANES), const),
                  pl.BlockSpec((1, GDN_DV), const)],
        out_specs=pl.BlockSpec((tc, GDN_V), blk),
        out_shape=jax.ShapeDtypeStruct((n, GDN_V), F32),
        scratch_shapes=[pltpu.VMEM((GDN_HEADS, GDN_DK, GDN_DV), F32)],
        compiler_params=_params(2),
        name="gdn",
    )(cqkv, cqkv, cg, small, conv_w, al_row, dt_row, gn[None])


def _first_argmax(x, lane):
    m = jnp.max(x, axis=-1, keepdims=True)
    idx = jnp.min(jnp.where(x == m, lane, LANES), axis=-1, keepdims=True)
    return m, idx


def _merge_kernel(ya_ref, yb_ref, yc_ref, gt_ref, x_ref, wa_ref, wb_ref, wc_ref, wo_ref, fn_ref,
                  wr_ref, br_ref, x1_ref, h2_ref, ri_ref, rw_ref, cnt_ref, carry_ref):
    @pl.when(pl.program_id(0) == 0)
    def _():
        carry_ref[...] = jnp.zeros_like(carry_ref)

    d = D_MODEL
    merged = (_sigmoid(gt_ref[:, :d]) * _dot(ya_ref[...], wa_ref[...])
              + _sigmoid(gt_ref[:, d:2 * d]) * _dot(yb_ref[...], wb_ref[...])
              + _sigmoid(gt_ref[:, 2 * d:]) * _dot(yc_ref[...], wc_ref[...]))
    x1 = x_ref[...] + _dot(merged, wo_ref[...])
    x1_ref[...] = x1
    h2 = _rms(x1, fn_ref[...])
    h2_ref[...] = h2

    logits = _dot_hi(h2, wr_ref[...]) + br_ref[...]
    tm = logits.shape[0]
    lane = _iota((tm, LANES), 1)
    neg = -jnp.inf
    is_c = lane < N_GROUPS
    cm, g_idx = _first_argmax(jnp.where(is_c, logits, neg), lane)
    g_prob = 1.0 / jnp.sum(jnp.where(is_c, jnp.exp(logits - cm), 0.0), axis=-1, keepdims=True)
    sel = (lane >= N_GROUPS) & (((lane - N_GROUPS) // EXPERTS_PER_GROUP) == g_idx)
    fm = jnp.max(jnp.where(sel, logits, neg), axis=-1, keepdims=True)
    ef = jnp.where(sel, jnp.exp(logits - fm), 0.0)
    p1, i1 = _first_argmax(jnp.where(sel, ef, neg), lane)
    p2, i2 = _first_argmax(jnp.where(sel & (lane != i1), ef, neg), lane)
    w1 = g_prob * p1 / (p1 + p2)
    w2 = g_prob * p2 / (p1 + p2)
    e1 = i1 - N_GROUPS
    e2 = i2 - N_GROUPS

    oh = jnp.where((lane == e1) | (lane == e2 + N_EXPERTS), 1.0, 0.0)
    stril = jnp.where(_iota((tm, tm), 1) < _iota((tm, tm), 0), 1.0, 0.0).astype(BF16)
    before = jnp.dot(stril, oh.astype(BF16), preferred_element_type=F32)
    tot = jnp.sum(oh, axis=0, keepdims=True)
    tot_first = jnp.where(lane[:1] < N_EXPERTS, tot, 0.0)
    carry = carry_ref[...]
    base = carry + pltpu.roll(carry + tot_first, N_EXPERTS, 1)
    ranks = oh * (before + base)
    rank1 = jnp.sum(jnp.where(lane < N_EXPERTS, ranks, 0.0), axis=-1, keepdims=True)
    rank2 = jnp.sum(jnp.where(lane < N_EXPERTS, 0.0, ranks), axis=-1, keepdims=True)
    new_carry = carry + tot_first + pltpu.roll(tot - tot_first, LANES - N_EXPERTS, 1)
    carry_ref[...] = new_carry
    cnt_ref[...] = jnp.broadcast_to(new_carry, cnt_ref.shape).astype(I32)

    ri_ref[...] = jnp.where(lane == 0, e1, jnp.where(lane == 1, e2, jnp.where(
        lane == 2, rank1.astype(I32), jnp.where(lane == 3, rank2.astype(I32), 0))))
    rw_ref[...] = jnp.where(lane == 0, w1, jnp.where(lane == 1, w2, 0.0))


def _merge(ya, yb, yc, gates, x, wa, wb, wc, wo, fn, wr, br, tm=256):
    n = x.shape[0]
    d = D_MODEL
    blk = lambda i: (i, 0)
    const = lambda i: (0, 0)
    return pl.pallas_call(
        _merge_kernel,
        grid=(n // tm,),
        in_specs=[pl.BlockSpec((tm, SWA_Q), blk), pl.BlockSpec((tm, GLA_V), blk),
                  pl.BlockSpec((tm, GDN_V), blk), pl.BlockSpec((tm, 3 * d), blk),
                  pl.BlockSpec((tm, d), blk),
                  pl.BlockSpec((SWA_Q, d), const), pl.BlockSpec((GLA_V, d), const),
                  pl.BlockSpec((GDN_V, d), const), pl.BlockSpec((d, d), const),
                  pl.BlockSpec((1, d), const), pl.BlockSpec((d, LANES), const),
                  pl.BlockSpec((1, LANES), const)],
        out_specs=[pl.BlockSpec((tm, d), blk), pl.BlockSpec((tm, d), blk),
                   pl.BlockSpec((tm, LANES), blk), pl.BlockSpec((tm, LANES), blk),
                   pl.BlockSpec((8, LANES), const)],
        out_shape=[jax.ShapeDtypeStruct((n, d), F32), jax.ShapeDtypeStruct((n, d), F32),
                   jax.ShapeDtypeStruct((n, LANES), I32), jax.ShapeDtypeStruct((n, LANES), F32),
                   jax.ShapeDtypeStruct((8, LANES), I32)],
        scratch_shapes=[pltpu.VMEM((1, LANES), F32)],
        compiler_params=_params(1),
        name="merge_route",
    )(ya, yb, yc, gates, x, wa, wb, wc, wo, fn, wr, br)


def _dispatch_kernel(slot_ref, h_ref, zero_ref, xs_ref, sem):
    del zero_ref
    td = h_ref.shape[0]

    def rows(start):
        def body(t, carry):
            for k in range(TOP_K):
                s = slot_ref[0, 0, TOP_K * t + k]
                cp = pltpu.make_async_copy(h_ref.at[pl.ds(t, 1)], xs_ref.at[pl.ds(s, 1)], sem)
                if start:
                    cp.start()
                else:
                    cp.wait()
            return carry

        lax.fori_loop(0, td, body, 0, unroll=4 if start else 8)

    rows(True)
    rows(False)


def _dispatch(h2, slots, n_slots, td=1024):
    n = h2.shape[0]
    td = min(td, n)
    zeros = jnp.zeros((n_slots, D_MODEL), F32)
    return pl.pallas_call(
        _dispatch_kernel,
        grid=(n // td,),
        in_specs=[pl.BlockSpec((1, 1, TOP_K * td), lambda i: (i, 0, 0), memory_space=pltpu.SMEM),
                  pl.BlockSpec((td, D_MODEL), lambda i: (i, 0)),
                  pl.BlockSpec(memory_space=pl.ANY)],
        out_specs=pl.BlockSpec(memory_space=pl.ANY),
        out_shape=jax.ShapeDtypeStruct((n_slots, D_MODEL), F32),
        scratch_shapes=[pltpu.SemaphoreType.DMA(())],
        input_output_aliases={2: 0},
        compiler_params=_params(1),
        name="dispatch",
    )(slots.reshape(n // td, 1, TOP_K * td), h2, zeros)


def _expert_kernel(be_ref, xs_ref, wg_ref, wu_ref, wd_ref, ys_ref, wg_s, wu_s, wd_s):
    i = pl.program_id(0)
    new_expert = jnp.logical_or(i == 0, be_ref[i] != be_ref[jnp.maximum(i - 1, 0)])

    @pl.when(new_expert)
    def _():
        wg_s[...] = wg_ref[0].astype(BF16)
        wu_s[...] = wu_ref[0].astype(BF16)
        wd_s[...] = wd_ref[0].astype(BF16)

    x = xs_ref[...].astype(BF16)
    act = _silu(jnp.dot(x, wg_s[...], preferred_element_type=F32)) * jnp.dot(x, wu_s[...], preferred_element_type=F32)
    ys_ref[...] = jnp.dot(act.astype(BF16), wd_s[...], preferred_element_type=F32)


def _experts(xs, block_expert, wg, wu, wd):
    n_slots = xs.shape[0]
    tb = MOE_ROWS
    grid_spec = pltpu.PrefetchScalarGridSpec(
        num_scalar_prefetch=1,
        grid=(n_slots // tb,),
        in_specs=[pl.BlockSpec((tb, D_MODEL), lambda i, be: (i, 0)),
                  pl.BlockSpec((1, D_MODEL, EXPERT_FF), lambda i, be: (be[i], 0, 0)),
                  pl.BlockSpec((1, D_MODEL, EXPERT_FF), lambda i, be: (be[i], 0, 0)),
                  pl.BlockSpec((1, EXPERT_FF, D_MODEL), lambda i, be: (be[i], 0, 0))],
        out_specs=pl.BlockSpec((tb, D_MODEL), lambda i, be: (i, 0)),
        scratch_shapes=[pltpu.VMEM((D_MODEL, EXPERT_FF), BF16), pltpu.VMEM((D_MODEL, EXPERT_FF), BF16),
                        pltpu.VMEM((EXPERT_FF, D_MODEL), BF16)],
    )
    return pl.pallas_call(
        _expert_kernel,
        grid_spec=grid_spec,
        out_shape=jax.ShapeDtypeStruct((n_slots, D_MODEL), F32),
        compiler_params=_params(1),
        name="experts",
    )(block_expert, xs, wg, wu, wd)


def _combine_kernel(slot_ref, next_slot_ref, x1_ref, rw_ref, p_ref, ys_ref, pn_ref, wpg_ref, wpl_ref, o_ref,
                    ybuf, sems):
    i = pl.program_id(0)
    tp = x1_ref.shape[0]
    cur = i % 2

    def rows(sref, buf, start):
        def body(t, carry):
            for k in range(TOP_K):
                s = sref[0, 0, TOP_K * t + k]
                cp = pltpu.make_async_copy(ys_ref.at[pl.ds(s, 1)], ybuf.at[buf, k, pl.ds(t, 1)], sems.at[buf])
                if start:
                    cp.start()
                else:
                    cp.wait()
            return carry

        lax.fori_loop(0, tp, body, 0, unroll=4 if start else 8)

    @pl.when(i == 0)
    def _():
        rows(slot_ref, 0, True)

    @pl.when(i + 1 < pl.num_programs(0))
    def _():
        rows(next_slot_ref, 1 - cur, True)

    rows(slot_ref, cur, False)

    rw = rw_ref[...]
    x2 = x1_ref[...] + rw[:, 0:1] * ybuf[cur, 0] + rw[:, 1:2] * ybuf[cur, 1]
    h3 = _rms(x2, pn_ref[...])
    gate = _sigmoid(_dot(h3, wpg_ref[...]))
    o_ref[...] = x2 + gate * _dot(p_ref[...], wpl_ref[...])


def _combine_ple(x1, rw, slots, ys, p_i, pn, wpg, wpl, tp=256):
    n = x1.shape[0]
    d = D_MODEL
    blk = lambda i: (i, 0)
    const = lambda i: (0, 0)
    nsteps = n // tp
    slot_blocks = slots.reshape(nsteps, 1, TOP_K * tp)
    return pl.pallas_call(
        _combine_kernel,
        grid=(nsteps,),
        in_specs=[pl.BlockSpec((1, 1, TOP_K * tp), lambda i: (i, 0, 0), memory_space=pltpu.SMEM),
                  pl.BlockSpec((1, 1, TOP_K * tp), lambda i: (jnp.minimum(i + 1, nsteps - 1), 0, 0),
                               memory_space=pltpu.SMEM),
                  pl.BlockSpec((tp, d), blk), pl.BlockSpec((tp, LANES), blk),
                  pl.BlockSpec((tp, PLE_DIM), blk),
                  pl.BlockSpec(memory_space=pl.ANY),
                  pl.BlockSpec((1, d), const), pl.BlockSpec((d, d), const),
                  pl.BlockSpec((PLE_DIM, d), const)],
        out_specs=pl.BlockSpec((tp, d), blk),
        out_shape=jax.ShapeDtypeStruct((n, d), F32),
        scratch_shapes=[pltpu.VMEM((2, TOP_K, tp, d), F32), pltpu.SemaphoreType.DMA((2,))],
        compiler_params=_params(1),
        name="combine_ple",
    )(slot_blocks, slot_blocks, x1, rw, p_i, ys, pn, wpg, wpl)


def _pack_w_in(w_in):
    o = IN_OFFSETS
    sec = lambda i: w_in[:, o[i]:o[i + 1]]
    pad = jnp.zeros((D_MODEL, LANES - GLA_GATE_RANK - 2 * GDN_HEADS), w_in.dtype)
    cols = [sec(0), sec(1), sec(2), sec(3), sec(4), sec(5), sec(7), sec(8), sec(11), sec(12),
            sec(6), sec(9), sec(10), pad]
    return jnp.concatenate(cols, axis=1).astype(BF16)


def _layer(x, p_i, cos_t, sin_t, batch, seq, attn_norm, w_in, q_norm, k_norm, sinks, gla_wa2, gla_ba,
           gla_norm, gdn_conv, gdn_a_log, gdn_dt_bias, gdn_norm, w_br_a, w_br_b, w_br_c, w_o,
           ffn_norm, w_coarse, b_coarse, w_fine, b_fine, w_gate_e, w_up_e, w_down_e,
           ple_norm, w_ple_gate, w_ple):
    n = x.shape[0]
    aq, akv, bqk, bv, br, cqkv, cg, gates, small = _inproj(x, attn_norm[None], _pack_w_in(w_in))
    ya = _swa(aq, akv, cos_t, sin_t, q_norm, k_norm, sinks, batch, seq)
    yb = _gla(bqk, bv, br, small, gla_wa2, gla_ba, gla_norm, batch, seq)
    yc = _gdn(cqkv, cg, small, gdn_conv, gdn_a_log, gdn_dt_bias, gdn_norm, batch, seq)

    wr = jnp.zeros((D_MODEL, LANES), F32).at[:, :N_GROUPS].set(w_coarse)
    wr = wr.at[:, N_GROUPS:N_GROUPS + N_EXPERTS].set(w_fine)
    brow = jnp.zeros((1, LANES), F32).at[0, :N_GROUPS].set(b_coarse)
    brow = brow.at[0, N_GROUPS:N_GROUPS + N_EXPERTS].set(b_fine)
    x1, h2, ri, rw, cnt = _merge(ya, yb, yc, gates, x, w_br_a.astype(BF16), w_br_b.astype(BF16),
                                 w_br_c.astype(BF16), w_o.astype(BF16), ffn_norm[None], wr, brow)

    counts = cnt[0, :N_EXPERTS]
    padded = (counts + MOE_ROWS - 1) // MOE_ROWS * MOE_ROWS
    pad_end = jnp.cumsum(padded)
    pad_start = pad_end - padded
    slots = (pad_start[ri[:, :TOP_K]] + ri[:, TOP_K:2 * TOP_K]).astype(I32)
    n_slots = n * TOP_K + N_EXPERTS * MOE_ROWS
    block_start = jnp.arange(n_slots // MOE_ROWS, dtype=I32) * MOE_ROWS
    block_expert = jnp.minimum(jnp.sum((pad_end[None, :] <= block_start[:, None]).astype(I32), axis=1),
                               N_EXPERTS - 1).astype(I32)

    xs = _dispatch(h2, slots, n_slots)
    ys = _experts(xs, block_expert, w_gate_e, w_up_e, w_down_e)
    return _combine_ple(x1, rw, slots, ys, p_i, ple_norm[None], w_ple_gate.astype(BF16),
                        w_ple.astype(BF16))


def kernel(x, p, positions, attn_norm, w_in, q_norm, k_norm, sinks, gla_wa2, gla_ba, gla_norm, gdn_conv, gdn_a_log, gdn_dt_bias, gdn_norm, w_br_a, w_br_b, w_br_c, w_o, ffn_norm, w_coarse, b_coarse, w_fine, b_fine, w_gate_e, w_up_e, w_down_e, ple_norm, w_ple_gate, w_ple):
    batch, seq, d = x.shape
    n = batch * seq
    depth = p.shape[0]
    cos_t, sin_t = _rope_tables(positions)
    xf = x.reshape(n, d)
    pf = p.reshape(depth, n, p.shape[-1])
    per_layer = (attn_norm, w_in, q_norm, k_norm, sinks, gla_wa2, gla_ba, gla_norm, gdn_conv, gdn_a_log,
                 gdn_dt_bias, gdn_norm, w_br_a, w_br_b, w_br_c, w_o, ffn_norm, w_coarse, b_coarse,
                 w_fine, b_fine, w_gate_e, w_up_e, w_down_e, ple_norm, w_ple_gate, w_ple)
    for i in range(depth):
        xf = _layer(xf, pf[i], cos_t, sin_t, batch, seq, *[a[i] for a in per_layer])
    return xf.reshape(batch, seq, d)
```

```python
import functools
import math

import numpy as np
import jax
import jax.numpy as jnp
from jax import lax
from jax.experimental import pallas as pl
from jax.experimental.pallas import tpu as pltpu

F32 = jnp.float32
BF16 = jnp.bfloat16
I32 = jnp.int32

D_MODEL = 1024
PLE_DIM = 256
NORM_EPS = 1e-6
MASK_VALUE = -1e30

SWA_Q_HEADS = 8
SWA_KV_HEADS = 2
SWA_HEAD_DIM = 64
SWA_WINDOW = 128
ROT_DIM = SWA_HEAD_DIM // 4
ROPE_THETA = 500000.0

GLA_HEADS = 4
GLA_DK = 64
GLA_DV = 128
GLA_GATE_RANK = 16
GLA_GATE_NORM = 16.0
CHUNK = 64

GDN_HEADS = 4
GDN_DK = 128
GDN_DV = 128
GDN_CONV = 4

N_GROUPS = 4
EXPERTS_PER_GROUP = 8
N_EXPERTS = N_GROUPS * EXPERTS_PER_GROUP
EXPERT_FF = 512
TOP_K = 2

SWA_Q = SWA_Q_HEADS * SWA_HEAD_DIM
SWA_KV = SWA_KV_HEADS * SWA_HEAD_DIM
GLA_K = GLA_HEADS * GLA_DK
GLA_V = GLA_HEADS * GLA_DV
GDN_K = GDN_HEADS * GDN_DK
GDN_V = GDN_HEADS * GDN_DV
GDN_QKV = 2 * GDN_K + GDN_V
IN_SPLITS = (SWA_Q, SWA_KV, SWA_KV, GLA_K, GLA_K, GLA_V, GLA_GATE_RANK, GLA_V,
             GDN_QKV, GDN_HEADS, GDN_HEADS, GDN_V, 3 * D_MODEL)
IN_OFFSETS = tuple(int(o) for o in np.cumsum((0,) + IN_SPLITS))

LANES = 128
SMALL_LR = 0
SMALL_BETA = GLA_GATE_RANK
SMALL_A = GLA_GATE_RANK + GDN_HEADS

OUT_WIDTHS = (SWA_Q, 2 * SWA_KV, 2 * GLA_K, GLA_V, GLA_V, GDN_QKV, GDN_V, 3 * D_MODEL, LANES)

MOE_ROWS = 256
VMEM_LIMIT = 56 * 1024 * 1024


def _params(n_axes):
    return pltpu.CompilerParams(dimension_semantics=("arbitrary",) * n_axes,
                                vmem_limit_bytes=VMEM_LIMIT)


def _dot(a, b):
    return jnp.dot(a.astype(BF16), b.astype(BF16), preferred_element_type=F32)


def _dot_nt(a, b):
    return lax.dot_general(a.astype(BF16), b.astype(BF16), (((1,), (1,)), ((), ())),
                           preferred_element_type=F32)


def _dot_tn(a, b):
    return lax.dot_general(a.astype(BF16), b.astype(BF16), (((0,), (0,)), ((), ())),
                           preferred_element_type=F32)


def _split2(x):
    hi = x.astype(BF16)
    lo = (x - hi.astype(F32)).astype(BF16)
    return hi, lo


def _dot_exact_lhs(a, x):
    hi = x.astype(BF16)
    r = x - hi.astype(F32)
    mid = r.astype(BF16)
    lo = (r - mid.astype(F32)).astype(BF16)
    a = a.astype(BF16)
    return (jnp.dot(a, hi, preferred_element_type=F32) + jnp.dot(a, mid, preferred_element_type=F32)
            + jnp.dot(a, lo, preferred_element_type=F32))


def _dot_hi(a, b):
    ah, al = _split2(a)
    bh, bl = _split2(b)
    return (jnp.dot(ah, bh, preferred_element_type=F32) + jnp.dot(ah, bl, preferred_element_type=F32)
            + jnp.dot(al, bh, preferred_element_type=F32))


def _rms(x, g):
    return x * lax.rsqrt(jnp.mean(x * x, axis=-1, keepdims=True) + NORM_EPS) * g


def _sigmoid(x):
    return 1.0 / (1.0 + jnp.exp(-x))


def _silu(x):
    return x * _sigmoid(x)


def _softplus(x):
    return jnp.maximum(x, 0.0) + jnp.log(1.0 + jnp.exp(-jnp.abs(x)))


def _log_sigmoid(x):
    return -_softplus(-x)


def _iota(shape, axis):
    return lax.broadcasted_iota(I32, shape, axis)


SUBLANES = 8
ROW_TILES = D_MODEL // LANES


def _store_token_tiles(ref, x):
    t = x.shape[0]
    for s in range(ROW_TILES):
        ref[pl.ds(s, t, stride=ROW_TILES), :] = x[:, s * LANES:(s + 1) * LANES]


def _load_token_tiles(ref, t):
    return jnp.concatenate([ref[pl.ds(s, t, stride=ROW_TILES), :] for s in range(ROW_TILES)], axis=-1)


def _chunk_tril(n, strict=False):
    r = _iota((n, n), 0)
    c = _iota((n, n), 1)
    same = (r // CHUNK) == (c // CHUNK)
    return same & ((c < r) if strict else (c <= r))


def _inproj_kernel(x_ref, g_ref, w_ref, *out_refs):
    h = _rms(x_ref[...], g_ref[...]).astype(BF16)
    off = 0
    for o_ref in out_refs:
        wd = o_ref.shape[-1]
        o_ref[...] = jnp.dot(h, w_ref[:, off:off + wd], preferred_element_type=F32)
        off += wd


def _inproj(x, g, w, tm=256):
    n = x.shape[0]
    nc = w.shape[1]
    return pl.pallas_call(
        _inproj_kernel,
        grid=(n // tm,),
        in_specs=[pl.BlockSpec((tm, D_MODEL), lambda i: (i, 0)),
                  pl.BlockSpec((1, D_MODEL), lambda i: (0, 0)),
                  pl.BlockSpec((D_MODEL, nc), lambda i: (0, 0), pipeline_mode=pl.Buffered(1))],
        out_specs=[pl.BlockSpec((tm, wd), lambda i: (i, 0)) for wd in OUT_WIDTHS],
        out_shape=[jax.ShapeDtypeStruct((n, wd), F32) for wd in OUT_WIDTHS],
        compiler_params=_params(1),
        name="inproj",
    )(x, g, w)


def _rope_kernel(pos_ref, invf_ref, cos_ref, sin_ref):
    ang = pos_ref[...].astype(F32) * invf_ref[...]
    d = _iota(ang.shape, 1) % SWA_HEAD_DIM
    half = ROT_DIM // 2
    cos_ref[...] = jnp.where(d < ROT_DIM, jnp.cos(ang), 1.0)
    s = jnp.sin(ang)
    sin_ref[...] = jnp.where(d < half, -s, jnp.where(d < ROT_DIM, s, 0.0))


def _rope_tables(positions, tm=1024):
    n = positions.size
    tm = min(tm, n)
    inv_freq = 1.0 / (ROPE_THETA ** (jnp.arange(0, ROT_DIM, 2, dtype=F32) / ROT_DIM))
    lane_freq = jnp.tile(jnp.concatenate([inv_freq, inv_freq, jnp.zeros(SWA_HEAD_DIM - ROT_DIM, F32)]),
                         LANES // SWA_HEAD_DIM)[None]
    return pl.pallas_call(
        _rope_kernel,
        grid=(n // tm,),
        in_specs=[pl.BlockSpec((tm, 1), lambda i: (i, 0)),
                  pl.BlockSpec((1, LANES), lambda i: (0, 0))],
        out_specs=[pl.BlockSpec((tm, LANES), lambda i: (i, 0))] * 2,
        out_shape=[jax.ShapeDtypeStruct((n, LANES), F32)] * 2,
        compiler_params=_params(1),
        name="rope_tables",
    )(positions.reshape(n, 1), lane_freq)


def _norm_rope(x, gain, cos, sin):
    lane = _iota(x.shape, 1)
    sq = x * x
    s0 = jnp.sum(jnp.where(lane < SWA_HEAD_DIM, sq, 0.0), axis=-1, keepdims=True)
    s1 = jnp.sum(jnp.where(lane < SWA_HEAD_DIM, 0.0, sq), axis=-1, keepdims=True)
    ms = jnp.where(lane < SWA_HEAD_DIM, s0, s1) * (1.0 / SWA_HEAD_DIM)
    xn = x * lax.rsqrt(ms + NORM_EPS) * gain
    half = ROT_DIM // 2
    up = pltpu.roll(xn, LANES - half, 1)
    down = pltpu.roll(xn, half, 1)
    partner = jnp.where((lane % SWA_HEAD_DIM) < half, up, down)
    return xn * cos + partner * sin


def _swa_kernel(sink_ref, q_ref, kv_ref, cos_ref, sin_ref, qn_ref, kn_ref, o_ref, kprev_ref, vprev_ref):
    j = pl.program_id(1)
    w = SWA_WINDOW
    hd = SWA_HEAD_DIM
    group = SWA_Q_HEADS // SWA_KV_HEADS

    @pl.when(j == 0)
    def _():
        kprev_ref[...] = jnp.zeros_like(kprev_ref)
        vprev_ref[...] = jnp.zeros_like(vprev_ref)

    cos = cos_ref[...]
    sin = sin_ref[...]
    kc = _norm_rope(kv_ref[:, :SWA_KV], kn_ref[...], cos, sin).astype(BF16)
    vc = kv_ref[:, SWA_KV:].astype(BF16)
    k_all = jnp.concatenate([kprev_ref[...], kc], axis=0)
    v_all = jnp.concatenate([vprev_ref[...], vc], axis=0)
    kprev_ref[...] = kc
    vprev_ref[...] = vc

    rows = group * w
    qi = _iota((rows, 2 * w), 0) % w
    kj = _iota((rows, 2 * w), 1)
    first_key = jnp.where(j > 0, 0, w)
    mask = (kj > qi) & (kj <= qi + w) & (kj >= first_key)
    head_of_row = _iota((rows, 1), 0) // w
    qrs = [_norm_rope(q_ref[:, c * LANES:(c + 1) * LANES], qn_ref[...], cos, sin)
           for c in range(SWA_Q // LANES)]
    per_block = LANES // hd
    qgs, sinks = [], []
    for g in range(SWA_KV_HEADS):
        heads = range(g * group, (g + 1) * group)
        qgs.append(jnp.concatenate(
            [qrs[h // per_block][:, (h % per_block) * hd:(h % per_block + 1) * hd] for h in heads], axis=0))
        sk = jnp.full((rows, 1), sink_ref[g * group], F32)
        for i in range(1, group):
            sk = jnp.where(head_of_row == i, sink_ref[g * group + i], sk)
        sinks.append(sk)
    ss = [jnp.where(mask, _dot_nt(qgs[g], k_all[:, g * hd:(g + 1) * hd]) * (hd ** -0.5), MASK_VALUE)
          for g in range(SWA_KV_HEADS)]
    ms = [jnp.maximum(jnp.max(s, axis=-1, keepdims=True), sk) for s, sk in zip(ss, sinks)]
    es = [jnp.exp(s - m) for s, m in zip(ss, ms)]
    dens = [jnp.sum(e, axis=-1, keepdims=True) + jnp.exp(sk - m) for e, sk, m in zip(es, sinks, ms)]
    for g in range(SWA_KV_HEADS):
        og = _dot(es[g], v_all[:, g * hd:(g + 1) * hd]) / dens[g]
        for i in range(group):
            h = g * group + i
            o_ref[:, h * hd:(h + 1) * hd] = og[i * w:(i + 1) * w]


def _swa(aq, akv, cos_t, sin_t, q_norm, k_norm, sinks, batch, seq):
    n = aq.shape[0]
    w = SWA_WINDOW
    nq = seq // w
    cur = lambda b, j: (b * nq + j, 0)
    rep = LANES // SWA_HEAD_DIM
    return pl.pallas_call(
        _swa_kernel,
        grid=(batch, nq),
        in_specs=[pl.BlockSpec(memory_space=pltpu.SMEM),
                  pl.BlockSpec((w, SWA_Q), cur),
                  pl.BlockSpec((w, 2 * SWA_KV), cur),
                  pl.BlockSpec((w, LANES), cur),
                  pl.BlockSpec((w, LANES), cur),
                  pl.BlockSpec((1, LANES), lambda b, j: (0, 0)),
                  pl.BlockSpec((1, LANES), lambda b, j: (0, 0))],
        out_specs=pl.BlockSpec((w, SWA_Q), cur),
        out_shape=jax.ShapeDtypeStruct((n, SWA_Q), F32),
        scratch_shapes=[pltpu.VMEM((w, SWA_KV), BF16), pltpu.VMEM((w, SWA_KV), BF16)],
        compiler_params=_params(2),
        name="swa",
    )(sinks, aq, akv, cos_t, sin_t, jnp.tile(q_norm, rep)[None], jnp.tile(k_norm, rep)[None])


def _gla_kernel(qk_ref, v_ref, r_ref, sm_ref, wa_ref, ba_ref, gn_ref, o_ref, state_ref):
    @pl.when(pl.program_id(1) == 0)
    def _():
        state_ref[...] = jnp.zeros_like(state_ref)

    tc = qk_ref.shape[0]
    la = _log_sigmoid(_dot(sm_ref[...], wa_ref[...]) + ba_ref[...]) * (1.0 / GLA_GATE_NORM)
    tri = jnp.where(_chunk_tril(tc), 1.0, 0.0).astype(BF16)
    b = _dot_exact_lhs(tri, la)
    causal = _iota((CHUNK, CHUNK), 1) <= _iota((CHUNK, CHUNK), 0)
    n_chunks = tc // CHUNK
    qes, dcols, intra, upd = [], [], {}, {}
    for c in range(n_chunks):
        rows = slice(c * CHUNK, (c + 1) * CHUNK)
        bc = b[rows]
        b_last = bc[CHUNK - 1:CHUNK]
        k = qk_ref[rows, GLA_K:]
        qe = qk_ref[rows, :GLA_K] * (GLA_DK ** -0.5) * jnp.exp(bc)
        ke = k * jnp.exp(-bc)
        kd = k * jnp.exp(b_last - bc)
        qes.append(qe)
        dcols.append(jnp.transpose(jnp.broadcast_to(jnp.exp(b_last), (GLA_DV, GLA_K))))
        for h in range(GLA_HEADS):
            ks = slice(h * GLA_DK, (h + 1) * GLA_DK)
            vh = v_ref[rows, h * GLA_DV:(h + 1) * GLA_DV]
            att = jnp.where(causal, _dot_nt(qe[:, ks], ke[:, ks]), 0.0)
            intra[c, h] = _dot(att, vh)
            upd[c, h] = _dot_tn(kd[:, ks], vh)
    for h in range(GLA_HEADS):
        ks = slice(h * GLA_DK, (h + 1) * GLA_DK)
        vs = slice(h * GLA_DV, (h + 1) * GLA_DV)
        st = state_ref[h]
        for c in range(n_chunks):
            rows = slice(c * CHUNK, (c + 1) * CHUNK)
            o = intra[c, h] + _dot(qes[c][:, ks], st)
            st = dcols[c][ks] * st + upd[c, h]
            o_ref[rows, vs] = _rms(o, gn_ref[...]) * _silu(r_ref[rows, vs])
        state_ref[h] = st


def _gla(bqk, bv, br, small, wa2, ba, gn, batch, seq, tc=256):
    n = bqk.shape[0]
    nt = seq // tc
    blk = lambda b, j: (b * nt + j, 0)
    const = lambda b, j: (0, 0)
    wa_pad = jnp.zeros((LANES, GLA_K), F32).at[SMALL_LR:SMALL_LR + GLA_GATE_RANK].set(wa2)
    return pl.pallas_call(
        _gla_kernel,
        grid=(batch, nt),
        in_specs=[pl.BlockSpec((tc, 2 * GLA_K), blk),
                  pl.BlockSpec((tc, GLA_V), blk),
                  pl.BlockSpec((tc, GLA_V), blk),
                  pl.BlockSpec((tc, LANES), blk),
                  pl.BlockSpec((LANES, GLA_K), const),
                  pl.BlockSpec((1, GLA_K), const),
                  pl.BlockSpec((1, GLA_DV), const)],
        out_specs=pl.BlockSpec((tc, GLA_V), blk),
        out_shape=jax.ShapeDtypeStruct((n, GLA_V), F32),
        scratch_shapes=[pltpu.VMEM((GLA_HEADS, GLA_DK, GLA_DV), F32)],
        compiler_params=_params(2),
        name="gla",
    )(bqk, bv, br, small, wa_pad, ba[None], gn[None])


def _l2(x):
    return x * lax.rsqrt(jnp.sum(x * x, axis=-1, keepdims=True) + NORM_EPS)


def _gdn_kernel(x_ref, xp_ref, g_ref, sm_ref, cw_ref, al_ref, dt_ref, gn_ref, o_ref, state_ref):
    first = pl.program_id(1) == 0

    @pl.when(first)
    def _():
        state_ref[...] = jnp.zeros_like(state_ref)

    tc = x_ref.shape[0]
    halo = xp_ref.shape[0]
    xprev = jnp.where(first, 0.0, xp_ref[...])
    xcat = jnp.concatenate([xprev, x_ref[...]], axis=0)
    conv = cw_ref[GDN_CONV - 1:GDN_CONV] * xcat[halo:halo + tc]
    for t in range(1, GDN_CONV):
        conv = conv + cw_ref[GDN_CONV - 1 - t:GDN_CONV - t] * xcat[halo - t:halo - t + tc]
    qkv = _silu(conv)

    sm = sm_ref[...]
    beta_all = _sigmoid(sm)
    g_all = -jnp.exp(al_ref[...]) * _softplus(sm + dt_ref[...])
    tri = jnp.where(_chunk_tril(tc), 1.0, 0.0).astype(BF16)
    big_g = _dot_exact_lhs(tri, g_all)
    g_rows = jnp.transpose(big_g)

    r = _iota((CHUNK, CHUNK), 0)
    cidx = _iota((CHUNK, CHUNK), 1)
    causal = cidx <= r
    strict = cidx < r
    eye = jnp.where(cidx == r, 1.0, 0.0)

    units = [(c, h) for c in range(tc // CHUNK) for h in range(GDN_HEADS)]
    qs, ks, egs, g_lasts, gcs, decays, lows, rhss = [], [], [], [], [], [], [], []
    for c, h in units:
        rows = slice(c * CHUNK, (c + 1) * CHUNK)
        q = _l2(qkv[rows, h * GDN_DK:(h + 1) * GDN_DK]) * (GDN_DK ** -0.5)
        k = _l2(qkv[rows, GDN_K + h * GDN_DK:GDN_K + (h + 1) * GDN_DK])
        v = qkv[rows, 2 * GDN_K + h * GDN_DV:2 * GDN_K + (h + 1) * GDN_DV]
        beta = beta_all[rows, SMALL_BETA + h:SMALL_BETA + h + 1]
        gc = big_g[rows, SMALL_A + h:SMALL_A + h + 1]
        g_row = g_rows[SMALL_A + h:SMALL_A + h + 1, c * CHUNK:(c + 1) * CHUNK]
        decay = jnp.where(causal, jnp.exp(jnp.where(causal, gc - g_row, 0.0)), 0.0)
        eg = jnp.exp(gc)
        qs.append(q)
        ks.append(k)
        egs.append(eg)
        gcs.append(gc)
        g_lasts.append(gc[CHUNK - 1:CHUNK])
        decays.append(decay)
        lows.append(jnp.where(strict, beta * _dot_nt(k, k) * decay, 0.0))
        rhss.append(jnp.concatenate([v * beta, k * (beta * eg)], axis=-1))
    invs = [eye - low for low in lows]
    pws = lows
    for _ in range(int(math.log2(CHUNK)) - 1):
        pws = [_dot(pw, pw) for pw in pws]
        invs = [inv + _dot(inv, pw) for inv, pw in zip(invs, pws)]
    sols = [_dot(inv, rhs) for inv, rhs in zip(invs, rhss)]
    atts = [_dot_nt(q, k) * decay for q, k, decay in zip(qs, ks, decays)]

    for i, (c, h) in enumerate(units):
        rows = slice(c * CHUNK, (c + 1) * CHUNK)
        u = sols[i][:, :GDN_DV]
        wm = sols[i][:, GDN_DV:]
        q_dec = qs[i] * egs[i]
        k_dec = ks[i] * jnp.exp(g_lasts[i] - gcs[i])
        st = state_ref[h]
        v_new = u - _dot(wm, st)
        o = _dot(q_dec, st) + _dot(atts[i], v_new)
        state_ref[h] = jnp.exp(g_lasts[i]) * st + _dot_tn(k_dec, v_new)
        vs = slice(h * GDN_DV, (h + 1) * GDN_DV)
        o_ref[rows, vs] = _rms(o, gn_ref[...]) * _silu(g_ref[rows, vs])


def _gdn(cqkv, cg, small, conv_w, a_log, dt_bias, gn, batch, seq, tc=256, halo=8):
    n = cqkv.shape[0]
    nt = seq // tc
    blk = lambda b, j: (b * nt + j, 0)
    const = lambda b, j: (0, 0)
    prev = lambda b, j: (jnp.maximum((b * nt + j) * (tc // halo) - 1, 0), 0)
    al_row = jnp.zeros((1, LANES), F32).at[0, SMALL_A:SMALL_A + GDN_HEADS].set(a_log)
    dt_row = jnp.zeros((1, LANES), F32).at[0, SMALL_A:SMALL_A + GDN_HEADS].set(dt_bias)
    return pl.pallas_call(
        _gdn_kernel,
        grid=(batch, nt),
        in_specs=[pl.BlockSpec((tc, GDN_QKV), blk),
                  pl.BlockSpec((halo, GDN_QKV), prev),
                  pl.BlockSpec((tc, GDN_V), blk),
                  pl.BlockSpec((tc, LANES), blk),
                  pl.BlockSpec((GDN_CONV, GDN_QKV), const),
                  pl.BlockSpec((1, LANES), const),
                  pl.BlockSpec((1, LANES), const),
                  pl.BlockSpec((1, GDN_DV), const)],
        out_specs=pl.BlockSpec((tc, GDN_V), blk),
        out_shape=jax.ShapeDtypeStruct((n, GDN_V), F32),
        scratch_shapes=[pltpu.VMEM((GDN_HEADS, GDN_DK, GDN_DV), F32)],
        compiler_params=_params(2),
        name="gdn",
    )(cqkv, cqkv, cg, small, conv_w, al_row, dt_row, gn[None])


def _first_argmax(x, lane):
    m = jnp.max(x, axis=-1, keepdims=True)
    idx = jnp.min(jnp.where(x == m, lane, LANES), axis=-1, keepdims=True)
    return m, idx


def _merge_kernel(ya_ref, yb_ref, yc_ref, gt_ref, x_ref, wa_ref, wb_ref, wc_ref, wo_ref, fn_ref,
                  wr_ref, br_ref, x1_ref, h2_ref, ri_ref, rw_ref, cnt_ref, carry_ref):
    @pl.when(pl.program_id(0) == 0)
    def _():
        carry_ref[...] = jnp.zeros_like(carry_ref)

    d = D_MODEL
    merged = (_sigmoid(gt_ref[:, :d]) * _dot(ya_ref[...], wa_ref[...])
              + _sigmoid(gt_ref[:, d:2 * d]) * _dot(yb_ref[...], wb_ref[...])
              + _sigmoid(gt_ref[:, 2 * d:]) * _dot(yc_ref[...], wc_ref[...]))
    x1 = x_ref[...] + _dot(merged, wo_ref[...])
    x1_ref[...] = x1
    h2 = _rms(x1, fn_ref[...])
    _store_token_tiles(h2_ref, h2)

    logits = _dot_hi(h2, wr_ref[...]) + br_ref[...]
    tm = logits.shape[0]
    lane = _iota((tm, LANES), 1)
    neg = -jnp.inf
    is_c = lane < N_GROUPS
    cm, g_idx = _first_argmax(jnp.where(is_c, logits, neg), lane)
    g_prob = 1.0 / jnp.sum(jnp.where(is_c, jnp.exp(logits - cm), 0.0), axis=-1, keepdims=True)
    sel = (lane >= N_GROUPS) & (((lane - N_GROUPS) // EXPERTS_PER_GROUP) == g_idx)
    fm = jnp.max(jnp.where(sel, logits, neg), axis=-1, keepdims=True)
    ef = jnp.where(sel, jnp.exp(logits - fm), 0.0)
    p1, i1 = _first_argmax(jnp.where(sel, ef, neg), lane)
    p2, i2 = _first_argmax(jnp.where(sel & (lane != i1), ef, neg), lane)
    w1 = g_prob * p1 / (p1 + p2)
    w2 = g_prob * p2 / (p1 + p2)
    e1 = i1 - N_GROUPS
    e2 = i2 - N_GROUPS

    oh = jnp.where((lane == e1) | (lane == e2 + N_EXPERTS), 1.0, 0.0)
    stril = jnp.where(_iota((tm, tm), 1) < _iota((tm, tm), 0), 1.0, 0.0).astype(BF16)
    before = jnp.dot(stril, oh.astype(BF16), preferred_element_type=F32)
    tot = jnp.sum(oh, axis=0, keepdims=True)
    tot_first = jnp.where(lane[:1] < N_EXPERTS, tot, 0.0)
    carry = carry_ref[...]
    base = carry + pltpu.roll(carry + tot_first, N_EXPERTS, 1)
    ranks = oh * (before + base)
    rank1 = jnp.sum(jnp.where(lane < N_EXPERTS, ranks, 0.0), axis=-1, keepdims=True)
    rank2 = jnp.sum(jnp.where(lane < N_EXPERTS, 0.0, ranks), axis=-1, keepdims=True)
    new_carry = carry + tot_first + pltpu.roll(tot - tot_first, LANES - N_EXPERTS, 1)
    carry_ref[...] = new_carry
    cnt_ref[...] = jnp.broadcast_to(new_carry, cnt_ref.shape).astype(I32)

    ri_ref[...] = jnp.where(lane == 0, e1, jnp.where(lane == 1, e2, jnp.where(
        lane == 2, rank1.astype(I32), jnp.where(lane == 3, rank2.astype(I32), 0))))
    rw_ref[...] = jnp.where(lane == 0, w1, jnp.where(lane == 1, w2, 0.0))


def _merge(ya, yb, yc, gates, x, wa, wb, wc, wo, fn, wr, br, tm=256):
    n = x.shape[0]
    d = D_MODEL
    blk = lambda i: (i, 0)
    const = lambda i: (0, 0)
    return pl.pallas_call(
        _merge_kernel,
        grid=(n // tm,),
        in_specs=[pl.BlockSpec((tm, SWA_Q), blk), pl.BlockSpec((tm, GLA_V), blk),
                  pl.BlockSpec((tm, GDN_V), blk), pl.BlockSpec((tm, 3 * d), blk),
                  pl.BlockSpec((tm, d), blk),
                  pl.BlockSpec((SWA_Q, d), const), pl.BlockSpec((GLA_V, d), const),
                  pl.BlockSpec((GDN_V, d), const), pl.BlockSpec((d, d), const),
                  pl.BlockSpec((1, d), const), pl.BlockSpec((d, LANES), const),
                  pl.BlockSpec((1, LANES), const)],
        out_specs=[pl.BlockSpec((tm, d), blk), pl.BlockSpec((tm * SUBLANES, LANES), blk),
                   pl.BlockSpec((tm, LANES), blk), pl.BlockSpec((tm, LANES), blk),
                   pl.BlockSpec((8, LANES), const)],
        out_shape=[jax.ShapeDtypeStruct((n, d), F32), jax.ShapeDtypeStruct((n * SUBLANES, LANES), F32),
                   jax.ShapeDtypeStruct((n, LANES), I32), jax.ShapeDtypeStruct((n, LANES), F32),
                   jax.ShapeDtypeStruct((8, LANES), I32)],
        scratch_shapes=[pltpu.VMEM((1, LANES), F32)],
        compiler_params=_params(1),
        name="merge_route",
    )(ya, yb, yc, gates, x, wa, wb, wc, wo, fn, wr, br)


def _dispatch_kernel(slot_ref, h_ref, zero_ref, xs_ref, sem):
    del zero_ref
    td = h_ref.shape[0] // SUBLANES

    def rows(start):
        def body(t, carry):
            src = h_ref.at[pl.ds(pl.multiple_of(t * SUBLANES, SUBLANES), SUBLANES)]
            for k in range(TOP_K):
                s = slot_ref[0, 0, TOP_K * t + k]
                cp = pltpu.make_async_copy(
                    src, xs_ref.at[pl.ds(pl.multiple_of(s * SUBLANES, SUBLANES), SUBLANES)], sem)
                if start:
                    cp.start()
                else:
                    cp.wait()
            return carry

        lax.fori_loop(0, td, body, 0, unroll=4 if start else 8)

    rows(True)
    rows(False)


def _dispatch(h2, slots, n_slots, td=1024):
    n = h2.shape[0] // SUBLANES
    td = min(td, n)
    zeros = jnp.zeros((n_slots * SUBLANES, LANES), F32)
    return pl.pallas_call(
        _dispatch_kernel,
        grid=(n // td,),
        in_specs=[pl.BlockSpec((1, 1, TOP_K * td), lambda i: (i, 0, 0), memory_space=pltpu.SMEM),
                  pl.BlockSpec((td * SUBLANES, LANES), lambda i: (i, 0)),
                  pl.BlockSpec(memory_space=pl.ANY)],
        out_specs=pl.BlockSpec(memory_space=pl.ANY),
        out_shape=jax.ShapeDtypeStruct((n_slots * SUBLANES, LANES), F32),
        scratch_shapes=[pltpu.SemaphoreType.DMA(())],
        input_output_aliases={2: 0},
        compiler_params=_params(1),
        name="dispatch",
    )(slots.reshape(n // td, 1, TOP_K * td), h2, zeros)


def _expert_kernel(be_ref, xs_ref, wg_ref, wu_ref, wd_ref, ys_ref, wg_s, wu_s, wd_s):
    i = pl.program_id(0)
    new_expert = jnp.logical_or(i == 0, be_ref[i] != be_ref[jnp.maximum(i - 1, 0)])

    @pl.when(new_expert)
    def _():
        wg_s[...] = wg_ref[0].astype(BF16)
        wu_s[...] = wu_ref[0].astype(BF16)
        wd_s[...] = wd_ref[0].astype(BF16)

    tb = xs_ref.shape[0] // SUBLANES
    x = _load_token_tiles(xs_ref, tb).astype(BF16)
    act = _silu(jnp.dot(x, wg_s[...], preferred_element_type=F32)) * jnp.dot(x, wu_s[...], preferred_element_type=F32)
    _store_token_tiles(ys_ref, jnp.dot(act.astype(BF16), wd_s[...], preferred_element_type=F32))


def _experts(xs, block_expert, wg, wu, wd):
    n_slots = xs.shape[0] // SUBLANES
    tb = MOE_ROWS
    grid_spec = pltpu.PrefetchScalarGridSpec(
        num_scalar_prefetch=1,
        grid=(n_slots // tb,),
        in_specs=[pl.BlockSpec((tb * SUBLANES, LANES), lambda i, be: (i, 0)),
                  pl.BlockSpec((1, D_MODEL, EXPERT_FF), lambda i, be: (be[i], 0, 0)),
                  pl.BlockSpec((1, D_MODEL, EXPERT_FF), lambda i, be: (be[i], 0, 0)),
                  pl.BlockSpec((1, EXPERT_FF, D_MODEL), lambda i, be: (be[i], 0, 0))],
        out_specs=pl.BlockSpec((tb * SUBLANES, LANES), lambda i, be: (i, 0)),
        scratch_shapes=[pltpu.VMEM((D_MODEL, EXPERT_FF), BF16), pltpu.VMEM((D_MODEL, EXPERT_FF), BF16),
                        pltpu.VMEM((EXPERT_FF, D_MODEL), BF16)],
    )
    return pl.pallas_call(
        _expert_kernel,
        grid_spec=grid_spec,
        out_shape=jax.ShapeDtypeStruct((n_slots * SUBLANES, LANES), F32),
        compiler_params=_params(1),
        name="experts",
    )(block_expert, xs, wg, wu, wd)


def _combine_kernel(slot_ref, next_slot_ref, x1_ref, rw_ref, p_ref, ys_ref, pn_ref, wpg_ref, wpl_ref, o_ref,
                    ybuf, sems):
    i = pl.program_id(0)
    tp = x1_ref.shape[0]
    cur = i % 2

    def rows(sref, buf, start):
        def body(t, carry):
            for k in range(TOP_K):
                s = sref[0, 0, TOP_K * t + k]
                cp = pltpu.make_async_copy(
                    ys_ref.at[pl.ds(pl.multiple_of(s * SUBLANES, SUBLANES), SUBLANES)],
                    ybuf.at[buf, k, pl.ds(pl.multiple_of(t * SUBLANES, SUBLANES), SUBLANES)], sems.at[buf])
                if start:
                    cp.start()
                else:
                    cp.wait()
            return carry

        lax.fori_loop(0, tp, body, 0, unroll=4 if start else 8)

    @pl.when(i == 0)
    def _():
        rows(slot_ref, 0, True)

    @pl.when(i + 1 < pl.num_programs(0))
    def _():
        rows(next_slot_ref, 1 - cur, True)

    rows(slot_ref, cur, False)

    rw = rw_ref[...]
    x2 = (x1_ref[...] + rw[:, 0:1] * _load_token_tiles(ybuf.at[cur, 0], tp)
          + rw[:, 1:2] * _load_token_tiles(ybuf.at[cur, 1], tp))
    h3 = _rms(x2, pn_ref[...])
    gate = _sigmoid(_dot(h3, wpg_ref[...]))
    o_ref[...] = x2 + gate * _dot(p_ref[...], wpl_ref[...])


def _combine_ple(x1, rw, slots, ys, p_i, pn, wpg, wpl, tp=256):
    n = x1.shape[0]
    d = D_MODEL
    blk = lambda i: (i, 0)
    const = lambda i: (0, 0)
    nsteps = n // tp
    slot_blocks = slots.reshape(nsteps, 1, TOP_K * tp)
    return pl.pallas_call(
        _combine_kernel,
        grid=(nsteps,),
        in_specs=[pl.BlockSpec((1, 1, TOP_K * tp), lambda i: (i, 0, 0), memory_space=pltpu.SMEM),
                  pl.BlockSpec((1, 1, TOP_K * tp), lambda i: (jnp.minimum(i + 1, nsteps - 1), 0, 0),
                               memory_space=pltpu.SMEM),
                  pl.BlockSpec((tp, d), blk), pl.BlockSpec((tp, LANES), blk),
                  pl.BlockSpec((tp, PLE_DIM), blk),
                  pl.BlockSpec(memory_space=pl.ANY),
                  pl.BlockSpec((1, d), const), pl.BlockSpec((d, d), const),
                  pl.BlockSpec((PLE_DIM, d), const)],
        out_specs=pl.BlockSpec((tp, d), blk),
        out_shape=jax.ShapeDtypeStruct((n, d), F32),
        scratch_shapes=[pltpu.VMEM((2, TOP_K, tp * SUBLANES, LANES), F32), pltpu.SemaphoreType.DMA((2,))],
        compiler_params=_params(1),
        name="combine_ple",
    )(slot_blocks, slot_blocks, x1, rw, p_i, ys, pn, wpg, wpl)


def _pack_w_in(w_in):
    o = IN_OFFSETS
    w = w_in.astype(BF16)
    pad = jnp.zeros((D_MODEL, LANES - GLA_GATE_RANK - 2 * GDN_HEADS), BF16)
    cols = [w[:, :o[6]], w[:, o[7]:o[9]], w[:, o[11]:], w[:, o[6]:o[7]], w[:, o[9]:o[11]], pad]
    return jnp.concatenate(cols, axis=1)


def _layer(x, p_i, cos_t, sin_t, batch, seq, attn_norm, w_in, q_norm, k_norm, sinks, gla_wa2, gla_ba,
           gla_norm, gdn_conv, gdn_a_log, gdn_dt_bias, gdn_norm, w_br_a, w_br_b, w_br_c, w_o,
           ffn_norm, w_coarse, b_coarse, w_fine, b_fine, w_gate_e, w_up_e, w_down_e,
           ple_norm, w_ple_gate, w_ple):
    n = x.shape[0]
    aq, akv, bqk, bv, br, cqkv, cg, gates, small = _inproj(x, attn_norm[None], _pack_w_in(w_in))
    ya = _swa(aq, akv, cos_t, sin_t, q_norm, k_norm, sinks, batch, seq)
    yb = _gla(bqk, bv, br, small, gla_wa2, gla_ba, gla_norm, batch, seq)
    yc = _gdn(cqkv, cg, small, gdn_conv, gdn_a_log, gdn_dt_bias, gdn_norm, batch, seq)

    wr = jnp.zeros((D_MODEL, LANES), F32).at[:, :N_GROUPS].set(w_coarse)
    wr = wr.at[:, N_GROUPS:N_GROUPS + N_EXPERTS].set(w_fine)
    brow = jnp.zeros((1, LANES), F32).at[0, :N_GROUPS].set(b_coarse)
    brow = brow.at[0, N_GROUPS:N_GROUPS + N_EXPERTS].set(b_fine)
    x1, h2, ri, rw, cnt = _merge(ya, yb, yc, gates, x, w_br_a.astype(BF16), w_br_b.astype(BF16),
                                 w_br_c.astype(BF16), w_o.astype(BF16), ffn_norm[None], wr, brow)

    counts = cnt[0, :N_EXPERTS]
    padded = (counts + MOE_ROWS - 1) // MOE_ROWS * MOE_ROWS
    pad_end = jnp.cumsum(padded)
    pad_start = pad_end - padded
    slots = (pad_start[ri[:, :TOP_K]] + ri[:, TOP_K:2 * TOP_K]).astype(I32)
    n_slots = n * TOP_K + N_EXPERTS * MOE_ROWS
    block_start = jnp.arange(n_slots // MOE_ROWS, dtype=I32) * MOE_ROWS
    block_expert = jnp.minimum(jnp.sum((pad_end[None, :] <= block_start[:, None]).astype(I32), axis=1),
                               N_EXPERTS - 1).astype(I32)

    xs = _dispatch(h2, slots, n_slots)
    ys = _experts(xs, block_expert, w_gate_e, w_up_e, w_down_e)
    return _combine_ple(x1, rw, slots, ys, p_i, ple_norm[None], w_ple_gate.astype(BF16),
                        w_ple.astype(BF16))


def kernel(x, p, positions, attn_norm, w_in, q_norm, k_norm, sinks, gla_wa2, gla_ba, gla_norm, gdn_conv, gdn_a_log, gdn_dt_bias, gdn_norm, w_br_a, w_br_b, w_br_c, w_o, ffn_norm, w_coarse, b_coarse, w_fine, b_fine, w_gate_e, w_up_e, w_down_e, ple_norm, w_ple_gate, w_ple):
    batch, seq, d = x.shape
    n = batch * seq
    depth = p.shape[0]
    cos_t, sin_t = _rope_tables(positions)
    xf = x.reshape(n, d)
    pf = p.reshape(depth, n, p.shape[-1])
    per_layer = (attn_norm, w_in, q_norm, k_norm, sinks, gla_wa2, gla_ba, gla_norm, gdn_conv, gdn_a_log,
                 gdn_dt_bias, gdn_norm, w_br_a, w_br_b, w_br_c, w_o, ffn_norm, w_coarse, b_coarse,
                 w_fine, b_fine, w_gate_e, w_up_e, w_down_e, ple_norm, w_ple_gate, w_ple)
    for i in range(depth):
        xf = _layer(xf, pf[i], cos_t, sin_t, batch, seq, *[a[i] for a in per_layer])
    return xf.reshape(batch, seq, d)
```

```python
import functools
import math

import numpy as np
import jax
import jax.numpy as jnp
from jax import lax
from jax.experimental import pallas as pl
from jax.experimental.pallas import tpu as pltpu

F32 = jnp.float32
BF16 = jnp.bfloat16
I32 = jnp.int32

D_MODEL = 1024
PLE_DIM = 256
NORM_EPS = 1e-6
MASK_VALUE = -1e30

SWA_Q_HEADS = 8
SWA_KV_HEADS = 2
SWA_HEAD_DIM = 64
SWA_WINDOW = 128
ROT_DIM = SWA_HEAD_DIM // 4
ROPE_THETA = 500000.0

GLA_HEADS = 4
GLA_DK = 64
GLA_DV = 128
GLA_GATE_RANK = 16
GLA_GATE_NORM = 16.0
CHUNK = 64

GDN_HEADS = 4
GDN_DK = 128
GDN_DV = 128
GDN_CONV = 4

N_GROUPS = 4
EXPERTS_PER_GROUP = 8
N_EXPERTS = N_GROUPS * EXPERTS_PER_GROUP
EXPERT_FF = 512
TOP_K = 2

SWA_Q = SWA_Q_HEADS * SWA_HEAD_DIM
SWA_KV = SWA_KV_HEADS * SWA_HEAD_DIM
GLA_K = GLA_HEADS * GLA_DK
GLA_V = GLA_HEADS * GLA_DV
GDN_K = GDN_HEADS * GDN_DK
GDN_V = GDN_HEADS * GDN_DV
GDN_QKV = 2 * GDN_K + GDN_V
IN_SPLITS = (SWA_Q, SWA_KV, SWA_KV, GLA_K, GLA_K, GLA_V, GLA_GATE_RANK, GLA_V,
             GDN_QKV, GDN_HEADS, GDN_HEADS, GDN_V, 3 * D_MODEL)
IN_OFFSETS = tuple(int(o) for o in np.cumsum((0,) + IN_SPLITS))

LANES = 128
SMALL_LR = 0
SMALL_BETA = GLA_GATE_RANK
SMALL_A = GLA_GATE_RANK + GDN_HEADS

OUT_WIDTHS = (SWA_Q, 2 * SWA_KV, 2 * GLA_K, GLA_V, GLA_V, GDN_QKV, GDN_V, 3 * D_MODEL, LANES)

MOE_ROWS = 256
VMEM_LIMIT = 56 * 1024 * 1024


def _params(n_axes):
    return pltpu.CompilerParams(dimension_semantics=("arbitrary",) * n_axes,
                                vmem_limit_bytes=VMEM_LIMIT)


def _dot(a, b):
    return jnp.dot(a.astype(BF16), b.astype(BF16), preferred_element_type=F32)


def _dot_nt(a, b):
    return lax.dot_general(a.astype(BF16), b.astype(BF16), (((1,), (1,)), ((), ())),
                           preferred_element_type=F32)


def _dot_tn(a, b):
    return lax.dot_general(a.astype(BF16), b.astype(BF16), (((0,), (0,)), ((), ())),
                           preferred_element_type=F32)


def _split2(x):
    hi = x.astype(BF16)
    lo = (x - hi.astype(F32)).astype(BF16)
    return hi, lo


def _dot_exact_lhs(a, x):
    hi = x.astype(BF16)
    r = x - hi.astype(F32)
    mid = r.astype(BF16)
    lo = (r - mid.astype(F32)).astype(BF16)
    a = a.astype(BF16)
    return (jnp.dot(a, hi, preferred_element_type=F32) + jnp.dot(a, mid, preferred_element_type=F32)
            + jnp.dot(a, lo, preferred_element_type=F32))


def _dot_hi(a, b):
    ah, al = _split2(a)
    bh, bl = _split2(b)
    return (jnp.dot(ah, bh, preferred_element_type=F32) + jnp.dot(ah, bl, preferred_element_type=F32)
            + jnp.dot(al, bh, preferred_element_type=F32))


def _rms(x, g):
    return x * lax.rsqrt(jnp.mean(x * x, axis=-1, keepdims=True) + NORM_EPS) * g


def _sigmoid(x):
    return 0.5 * jnp.tanh(0.5 * x) + 0.5


def _silu(x):
    return x * _sigmoid(x)


def _softplus(x):
    return jnp.maximum(x, 0.0) + jnp.log(1.0 + jnp.exp(-jnp.abs(x)))


def _log_sigmoid(x):
    return -_softplus(-x)


def _iota(shape, axis):
    return lax.broadcasted_iota(I32, shape, axis)


SUBLANES = 8
ROW_TILES = D_MODEL // LANES


def _store_token_tiles(ref, x):
    t = x.shape[0]
    for s in range(ROW_TILES):
        ref[pl.ds(s, t, stride=ROW_TILES), :] = x[:, s * LANES:(s + 1) * LANES]


def _load_token_tiles(ref, t):
    return jnp.concatenate([ref[pl.ds(s, t, stride=ROW_TILES), :] for s in range(ROW_TILES)], axis=-1)


def _chunk_tril(n, strict=False):
    r = _iota((n, n), 0)
    c = _iota((n, n), 1)
    same = (r // CHUNK) == (c // CHUNK)
    return same & ((c < r) if strict else (c <= r))


def _inproj_kernel(x_ref, g_ref, w_ref, *out_refs):
    h = _rms(x_ref[...], g_ref[...]).astype(BF16)
    off = 0
    for o_ref in out_refs:
        wd = o_ref.shape[-1]
        o_ref[...] = jnp.dot(h, w_ref[:, off:off + wd], preferred_element_type=F32)
        off += wd


def _inproj(x, g, w, tm=256):
    n = x.shape[0]
    nc = w.shape[1]
    return pl.pallas_call(
        _inproj_kernel,
        grid=(n // tm,),
        in_specs=[pl.BlockSpec((tm, D_MODEL), lambda i: (i, 0)),
                  pl.BlockSpec((1, D_MODEL), lambda i: (0, 0)),
                  pl.BlockSpec((D_MODEL, nc), lambda i: (0, 0), pipeline_mode=pl.Buffered(1))],
        out_specs=[pl.BlockSpec((tm, wd), lambda i: (i, 0)) for wd in OUT_WIDTHS],
        out_shape=[jax.ShapeDtypeStruct((n, wd), F32) for wd in OUT_WIDTHS],
        compiler_params=_params(1),
        name="inproj",
    )(x, g, w)


def _rope_kernel(pos_ref, invf_ref, cos_ref, sin_ref):
    ang = pos_ref[...].astype(F32) * invf_ref[...]
    d = _iota(ang.shape, 1) % SWA_HEAD_DIM
    half = ROT_DIM // 2
    cos_ref[...] = jnp.where(d < ROT_DIM, jnp.cos(ang), 1.0)
    s = jnp.sin(ang)
    sin_ref[...] = jnp.where(d < half, -s, jnp.where(d < ROT_DIM, s, 0.0))


def _rope_tables(positions, tm=1024):
    n = positions.size
    tm = min(tm, n)
    inv_freq = 1.0 / (ROPE_THETA ** (jnp.arange(0, ROT_DIM, 2, dtype=F32) / ROT_DIM))
    lane_freq = jnp.tile(jnp.concatenate([inv_freq, inv_freq, jnp.zeros(SWA_HEAD_DIM - ROT_DIM, F32)]),
                         LANES // SWA_HEAD_DIM)[None]
    return pl.pallas_call(
        _rope_kernel,
        grid=(n // tm,),
        in_specs=[pl.BlockSpec((tm, 1), lambda i: (i, 0)),
                  pl.BlockSpec((1, LANES), lambda i: (0, 0))],
        out_specs=[pl.BlockSpec((tm, LANES), lambda i: (i, 0))] * 2,
        out_shape=[jax.ShapeDtypeStruct((n, LANES), F32)] * 2,
        compiler_params=_params(1),
        name="rope_tables",
    )(positions.reshape(n, 1), lane_freq)


def _norm_rope(x, gain, cos, sin):
    lane = _iota(x.shape, 1)
    sq = x * x
    s0 = jnp.sum(jnp.where(lane < SWA_HEAD_DIM, sq, 0.0), axis=-1, keepdims=True)
    s1 = jnp.sum(jnp.where(lane < SWA_HEAD_DIM, 0.0, sq), axis=-1, keepdims=True)
    ms = jnp.where(lane < SWA_HEAD_DIM, s0, s1) * (1.0 / SWA_HEAD_DIM)
    xn = x * lax.rsqrt(ms + NORM_EPS) * gain
    half = ROT_DIM // 2
    up = pltpu.roll(xn, LANES - half, 1)
    down = pltpu.roll(xn, half, 1)
    partner = jnp.where((lane % SWA_HEAD_DIM) < half, up, down)
    return xn * cos + partner * sin


def _swa_kernel(sink_ref, q_ref, kv_ref, cos_ref, sin_ref, qn_ref, kn_ref, o_ref, kprev_ref, vprev_ref):
    j = pl.program_id(1)
    w = SWA_WINDOW
    hd = SWA_HEAD_DIM
    group = SWA_Q_HEADS // SWA_KV_HEADS

    @pl.when(j == 0)
    def _():
        kprev_ref[...] = jnp.zeros_like(kprev_ref)
        vprev_ref[...] = jnp.zeros_like(vprev_ref)

    cos = cos_ref[...]
    sin = sin_ref[...]
    kc = _norm_rope(kv_ref[:, :SWA_KV], kn_ref[...], cos, sin).astype(BF16)
    vc = kv_ref[:, SWA_KV:].astype(BF16)
    k_all = jnp.concatenate([kprev_ref[...], kc], axis=0)
    v_all = jnp.concatenate([vprev_ref[...], vc], axis=0)
    kprev_ref[...] = kc
    vprev_ref[...] = vc

    rows = group * w
    qi = _iota((rows, 2 * w), 0) % w
    kj = _iota((rows, 2 * w), 1)
    first_key = jnp.where(j > 0, 0, w)
    mask = (kj > qi) & (kj <= qi + w) & (kj >= first_key)
    head_of_row = _iota((rows, 1), 0) // w
    qrs = [_norm_rope(q_ref[:, c * LANES:(c + 1) * LANES], qn_ref[...], cos, sin)
           for c in range(SWA_Q // LANES)]
    per_block = LANES // hd
    qgs, sinks = [], []
    for g in range(SWA_KV_HEADS):
        heads = range(g * group, (g + 1) * group)
        qgs.append(jnp.concatenate(
            [qrs[h // per_block][:, (h % per_block) * hd:(h % per_block + 1) * hd] for h in heads], axis=0))
        sk = jnp.full((rows, 1), sink_ref[g * group], F32)
        for i in range(1, group):
            sk = jnp.where(head_of_row == i, sink_ref[g * group + i], sk)
        sinks.append(sk)
    ss = [jnp.where(mask, _dot_nt(qgs[g], k_all[:, g * hd:(g + 1) * hd]) * (hd ** -0.5), MASK_VALUE)
          for g in range(SWA_KV_HEADS)]
    ms = [jnp.maximum(jnp.max(s, axis=-1, keepdims=True), sk) for s, sk in zip(ss, sinks)]
    es = [jnp.exp(s - m) for s, m in zip(ss, ms)]
    dens = [jnp.sum(e, axis=-1, keepdims=True) + jnp.exp(sk - m) for e, sk, m in zip(es, sinks, ms)]
    for g in range(SWA_KV_HEADS):
        og = _dot(es[g], v_all[:, g * hd:(g + 1) * hd]) / dens[g]
        for i in range(group):
            h = g * group + i
            o_ref[:, h * hd:(h + 1) * hd] = og[i * w:(i + 1) * w]


def _swa(aq, akv, cos_t, sin_t, q_norm, k_norm, sinks, batch, seq):
    n = aq.shape[0]
    w = SWA_WINDOW
    nq = seq // w
    cur = lambda b, j: (b * nq + j, 0)
    rep = LANES // SWA_HEAD_DIM
    return pl.pallas_call(
        _swa_kernel,
        grid=(batch, nq),
        in_specs=[pl.BlockSpec(memory_space=pltpu.SMEM),
                  pl.BlockSpec((w, SWA_Q), cur),
                  pl.BlockSpec((w, 2 * SWA_KV), cur),
                  pl.BlockSpec((w, LANES), cur),
                  pl.BlockSpec((w, LANES), cur),
                  pl.BlockSpec((1, LANES), lambda b, j: (0, 0)),
                  pl.BlockSpec((1, LANES), lambda b, j: (0, 0))],
        out_specs=pl.BlockSpec((w, SWA_Q), cur),
        out_shape=jax.ShapeDtypeStruct((n, SWA_Q), F32),
        scratch_shapes=[pltpu.VMEM((w, SWA_KV), BF16), pltpu.VMEM((w, SWA_KV), BF16)],
        compiler_params=_params(2),
        name="swa",
    )(sinks, aq, akv, cos_t, sin_t, jnp.tile(q_norm, rep)[None], jnp.tile(k_norm, rep)[None])


def _gla_kernel(qk_ref, v_ref, r_ref, sm_ref, wa_ref, ba_ref, gn_ref, o_ref, state_ref):
    @pl.when(pl.program_id(1) == 0)
    def _():
        state_ref[...] = jnp.zeros_like(state_ref)

    tc = qk_ref.shape[0]
    la = _log_sigmoid(_dot(sm_ref[...], wa_ref[...]) + ba_ref[...]) * (1.0 / GLA_GATE_NORM)
    tri = jnp.where(_chunk_tril(tc), 1.0, 0.0).astype(BF16)
    b = _dot_exact_lhs(tri, la)
    causal = _iota((CHUNK, CHUNK), 1) <= _iota((CHUNK, CHUNK), 0)
    n_chunks = tc // CHUNK
    qes, dcols, intra, upd = [], [], {}, {}
    for c in range(n_chunks):
        rows = slice(c * CHUNK, (c + 1) * CHUNK)
        bc = b[rows]
        b_last = bc[CHUNK - 1:CHUNK]
        k = qk_ref[rows, GLA_K:]
        qe = qk_ref[rows, :GLA_K] * (GLA_DK ** -0.5) * jnp.exp(bc)
        ke = k * jnp.exp(-bc)
        kd = k * jnp.exp(b_last - bc)
        qes.append(qe)
        dcols.append(jnp.transpose(jnp.broadcast_to(jnp.exp(b_last), (GLA_DV, GLA_K))))
        for h in range(GLA_HEADS):
            ks = slice(h * GLA_DK, (h + 1) * GLA_DK)
            vh = v_ref[rows, h * GLA_DV:(h + 1) * GLA_DV]
            att = jnp.where(causal, _dot_nt(qe[:, ks], ke[:, ks]), 0.0)
            intra[c, h] = _dot(att, vh)
            upd[c, h] = _dot_tn(kd[:, ks], vh)
    for h in range(GLA_HEADS):
        ks = slice(h * GLA_DK, (h + 1) * GLA_DK)
        vs = slice(h * GLA_DV, (h + 1) * GLA_DV)
        st = state_ref[h]
        for c in range(n_chunks):
            rows = slice(c * CHUNK, (c + 1) * CHUNK)
            o = intra[c, h] + _dot(qes[c][:, ks], st)
            st = dcols[c][ks] * st + upd[c, h]
            o_ref[rows, vs] = _rms(o, gn_ref[...]) * _silu(r_ref[rows, vs])
        state_ref[h] = st


def _gla(bqk, bv, br, small, wa2, ba, gn, batch, seq, tc=256):
    n = bqk.shape[0]
    nt = seq // tc
    blk = lambda b, j: (b * nt + j, 0)
    const = lambda b, j: (0, 0)
    wa_pad = jnp.zeros((LANES, GLA_K), F32).at[SMALL_LR:SMALL_LR + GLA_GATE_RANK].set(wa2)
    return pl.pallas_call(
        _gla_kernel,
        grid=(batch, nt),
        in_specs=[pl.BlockSpec((tc, 2 * GLA_K), blk),
                  pl.BlockSpec((tc, GLA_V), blk),
                  pl.BlockSpec((tc, GLA_V), blk),
                  pl.BlockSpec((tc, LANES), blk),
                  pl.BlockSpec((LANES, GLA_K), const),
                  pl.BlockSpec((1, GLA_K), const),
                  pl.BlockSpec((1, GLA_DV), const)],
        out_specs=pl.BlockSpec((tc, GLA_V), blk),
        out_shape=jax.ShapeDtypeStruct((n, GLA_V), F32),
        scratch_shapes=[pltpu.VMEM((GLA_HEADS, GLA_DK, GLA_DV), F32)],
        compiler_params=_params(2),
        name="gla",
    )(bqk, bv, br, small, wa_pad, ba[None], gn[None])


def _l2(x):
    return x * lax.rsqrt(jnp.sum(x * x, axis=-1, keepdims=True) + NORM_EPS)


def _gdn_kernel(x_ref, xp_ref, g_ref, sm_ref, cw_ref, al_ref, dt_ref, gn_ref, o_ref, state_ref):
    first = pl.program_id(1) == 0

    @pl.when(first)
    def _():
        state_ref[...] = jnp.zeros_like(state_ref)

    tc = x_ref.shape[0]
    halo = xp_ref.shape[0]
    xprev = jnp.where(first, 0.0, xp_ref[...])
    xcat = jnp.concatenate([xprev, x_ref[...]], axis=0)
    conv = cw_ref[GDN_CONV - 1:GDN_CONV] * xcat[halo:halo + tc]
    for t in range(1, GDN_CONV):
        conv = conv + cw_ref[GDN_CONV - 1 - t:GDN_CONV - t] * xcat[halo - t:halo - t + tc]
    qkv = _silu(conv)

    sm = sm_ref[...]
    beta_all = _sigmoid(sm)
    g_all = -jnp.exp(al_ref[...]) * _softplus(sm + dt_ref[...])
    tri = jnp.where(_chunk_tril(tc), 1.0, 0.0).astype(BF16)
    big_g = _dot_exact_lhs(tri, g_all)
    g_rows = jnp.transpose(big_g)

    r = _iota((CHUNK, CHUNK), 0)
    cidx = _iota((CHUNK, CHUNK), 1)
    causal = cidx <= r
    strict = cidx < r
    eye = jnp.where(cidx == r, 1.0, 0.0)

    units = [(c, h) for c in range(tc // CHUNK) for h in range(GDN_HEADS)]
    qs, ks, egs, g_lasts, gcs, decays, lows, rhss = [], [], [], [], [], [], [], []
    for c, h in units:
        rows = slice(c * CHUNK, (c + 1) * CHUNK)
        q = _l2(qkv[rows, h * GDN_DK:(h + 1) * GDN_DK]) * (GDN_DK ** -0.5)
        k = _l2(qkv[rows, GDN_K + h * GDN_DK:GDN_K + (h + 1) * GDN_DK])
        v = qkv[rows, 2 * GDN_K + h * GDN_DV:2 * GDN_K + (h + 1) * GDN_DV]
        beta = beta_all[rows, SMALL_BETA + h:SMALL_BETA + h + 1]
        gc = big_g[rows, SMALL_A + h:SMALL_A + h + 1]
        g_row = g_rows[SMALL_A + h:SMALL_A + h + 1, c * CHUNK:(c + 1) * CHUNK]
        decay = jnp.where(causal, jnp.exp(jnp.where(causal, gc - g_row, 0.0)), 0.0)
        eg = jnp.exp(gc)
        qs.append(q)
        ks.append(k)
        egs.append(eg)
        gcs.append(gc)
        g_lasts.append(gc[CHUNK - 1:CHUNK])
        decays.append(decay)
        lows.append(jnp.where(strict, beta * _dot_nt(k, k) * decay, 0.0))
        rhss.append(jnp.concatenate([v * beta, k * (beta * eg)], axis=-1))
    invs = [eye - low for low in lows]
    pws = lows
    for _ in range(int(math.log2(CHUNK)) - 1):
        pws = [_dot(pw, pw) for pw in pws]
        invs = [inv + _dot(inv, pw) for inv, pw in zip(invs, pws)]
    sols = [_dot(inv, rhs) for inv, rhs in zip(invs, rhss)]
    atts = [_dot_nt(q, k) * decay for q, k, decay in zip(qs, ks, decays)]

    for i, (c, h) in enumerate(units):
        rows = slice(c * CHUNK, (c + 1) * CHUNK)
        u = sols[i][:, :GDN_DV]
        wm = sols[i][:, GDN_DV:]
        q_dec = qs[i] * egs[i]
        k_dec = ks[i] * jnp.exp(g_lasts[i] - gcs[i])
        st = state_ref[h]
        v_new = u - _dot(wm, st)
        o = _dot(q_dec, st) + _dot(atts[i], v_new)
        state_ref[h] = jnp.exp(g_lasts[i]) * st + _dot_tn(k_dec, v_new)
        vs = slice(h * GDN_DV, (h + 1) * GDN_DV)
        o_ref[rows, vs] = _rms(o, gn_ref[...]) * _silu(g_ref[rows, vs])


def _gdn(cqkv, cg, small, conv_w, a_log, dt_bias, gn, batch, seq, tc=256, halo=8):
    n = cqkv.shape[0]
    nt = seq // tc
    blk = lambda b, j: (b * nt + j, 0)
    const = lambda b, j: (0, 0)
    prev = lambda b, j: (jnp.maximum((b * nt + j) * (tc // halo) - 1, 0), 0)
    al_row = jnp.zeros((1, LANES), F32).at[0, SMALL_A:SMALL_A + GDN_HEADS].set(a_log)
    dt_row = jnp.zeros((1, LANES), F32).at[0, SMALL_A:SMALL_A + GDN_HEADS].set(dt_bias)
    return pl.pallas_call(
        _gdn_kernel,
        grid=(batch, nt),
        in_specs=[pl.BlockSpec((tc, GDN_QKV), blk),
                  pl.BlockSpec((halo, GDN_QKV), prev),
                  pl.BlockSpec((tc, GDN_V), blk),
                  pl.BlockSpec((tc, LANES), blk),
                  pl.BlockSpec((GDN_CONV, GDN_QKV), const),
                  pl.BlockSpec((1, LANES), const),
                  pl.BlockSpec((1, LANES), const),
                  pl.BlockSpec((1, GDN_DV), const)],
        out_specs=pl.BlockSpec((tc, GDN_V), blk),
        out_shape=jax.ShapeDtypeStruct((n, GDN_V), F32),
        scratch_shapes=[pltpu.VMEM((GDN_HEADS, GDN_DK, GDN_DV), F32)],
        compiler_params=_params(2),
        name="gdn",
    )(cqkv, cqkv, cg, small, conv_w, al_row, dt_row, gn[None])


def _first_argmax(x, lane):
    m = jnp.max(x, axis=-1, keepdims=True)
    idx = jnp.min(jnp.where(x == m, lane.astype(F32), float(LANES)), axis=-1, keepdims=True)
    return m, idx.astype(I32)


def _merge_kernel(ya_ref, yb_ref, yc_ref, gt_ref, x_ref, wa_ref, wb_ref, wc_ref, wo_ref, fn_ref,
                  wr_ref, br_ref, x1_ref, h2_ref, ri_ref, rw_ref, cnt_ref, carry_ref):
    @pl.when(pl.program_id(0) == 0)
    def _():
        carry_ref[...] = jnp.zeros_like(carry_ref)

    d = D_MODEL
    ya = ya_ref[...].astype(BF16)
    yb = yb_ref[...].astype(BF16)
    yc = yc_ref[...].astype(BF16)
    cw = 2 * LANES
    merged = []
    for c in range(d // cw):
        lo = c * cw
        merged.append((
            _sigmoid(gt_ref[:, lo:lo + cw]) * jnp.dot(ya, wa_ref[:, lo:lo + cw], preferred_element_type=F32)
            + _sigmoid(gt_ref[:, d + lo:d + lo + cw]) * jnp.dot(yb, wb_ref[:, lo:lo + cw], preferred_element_type=F32)
            + _sigmoid(gt_ref[:, 2 * d + lo:2 * d + lo + cw]) * jnp.dot(yc, wc_ref[:, lo:lo + cw], preferred_element_type=F32)
        ).astype(BF16))
    merged = jnp.concatenate(merged, axis=-1)
    x1 = x_ref[...] + jnp.dot(merged, wo_ref[...], preferred_element_type=F32)
    x1_ref[...] = x1
    h2 = _rms(x1, fn_ref[...])
    _store_token_tiles(h2_ref, h2)

    logits = _dot_hi(h2, wr_ref[...]) + br_ref[...]
    tm = logits.shape[0]
    lane = _iota((tm, LANES), 1)
    neg = -jnp.inf
    is_c = lane < N_GROUPS
    cm, g_idx = _first_argmax(jnp.where(is_c, logits, neg), lane)
    g_prob = 1.0 / jnp.sum(jnp.where(is_c, jnp.exp(logits - cm), 0.0), axis=-1, keepdims=True)
    sel = (lane >= N_GROUPS) & (((lane - N_GROUPS) // EXPERTS_PER_GROUP) == g_idx)
    fm = jnp.max(jnp.where(sel, logits, neg), axis=-1, keepdims=True)
    ef = jnp.where(sel, jnp.exp(logits - fm), 0.0)
    p1, i1 = _first_argmax(jnp.where(sel, ef, neg), lane)
    p2, i2 = _first_argmax(jnp.where(sel & (lane != i1), ef, neg), lane)
    w1 = g_prob * p1 / (p1 + p2)
    w2 = g_prob * p2 / (p1 + p2)
    e1 = i1 - N_GROUPS
    e2 = i2 - N_GROUPS

    oh = jnp.where((lane == e1) | (lane == e2 + N_EXPERTS), 1.0, 0.0)
    stril = jnp.where(_iota((tm, tm), 1) < _iota((tm, tm), 0), 1.0, 0.0).astype(BF16)
    before = jnp.dot(stril, oh.astype(BF16), preferred_element_type=F32)
    tot = jnp.sum(oh, axis=0, keepdims=True)
    tot_first = jnp.where(lane[:1] < N_EXPERTS, tot, 0.0)
    carry = carry_ref[...]
    base = carry + pltpu.roll(carry + tot_first, N_EXPERTS, 1)
    ranks = oh * (before + base)
    rank1 = jnp.sum(jnp.where(lane < N_EXPERTS, ranks, 0.0), axis=-1, keepdims=True)
    rank2 = jnp.sum(jnp.where(lane < N_EXPERTS, 0.0, ranks), axis=-1, keepdims=True)
    new_carry = carry + tot_first + pltpu.roll(tot - tot_first, LANES - N_EXPERTS, 1)
    carry_ref[...] = new_carry
    cnt_ref[...] = jnp.broadcast_to(new_carry, cnt_ref.shape).astype(I32)

    ri_ref[...] = jnp.where(lane == 0, e1, jnp.where(lane == 1, e2, jnp.where(
        lane == 2, rank1.astype(I32), jnp.where(lane == 3, rank2.astype(I32), 0))))
    rw_ref[...] = jnp.where(lane == 0, w1, jnp.where(lane == 1, w2, 0.0))


def _merge(ya, yb, yc, gates, x, wa, wb, wc, wo, fn, wr, br, tm=256):
    n = x.shape[0]
    d = D_MODEL
    blk = lambda i: (i, 0)
    const = lambda i: (0, 0)
    return pl.pallas_call(
        _merge_kernel,
        grid=(n // tm,),
        in_specs=[pl.BlockSpec((tm, SWA_Q), blk), pl.BlockSpec((tm, GLA_V), blk),
                  pl.BlockSpec((tm, GDN_V), blk), pl.BlockSpec((tm, 3 * d), blk),
                  pl.BlockSpec((tm, d), blk),
                  pl.BlockSpec((SWA_Q, d), const), pl.BlockSpec((GLA_V, d), const),
                  pl.BlockSpec((GDN_V, d), const), pl.BlockSpec((d, d), const),
                  pl.BlockSpec((1, d), const), pl.BlockSpec((d, LANES), const),
                  pl.BlockSpec((1, LANES), const)],
        out_specs=[pl.BlockSpec((tm, d), blk), pl.BlockSpec((tm * SUBLANES, LANES), blk),
                   pl.BlockSpec((tm, LANES), blk), pl.BlockSpec((tm, LANES), blk),
                   pl.BlockSpec((8, LANES), const)],
        out_shape=[jax.ShapeDtypeStruct((n, d), F32), jax.ShapeDtypeStruct((n * SUBLANES, LANES), F32),
                   jax.ShapeDtypeStruct((n, LANES), I32), jax.ShapeDtypeStruct((n, LANES), F32),
                   jax.ShapeDtypeStruct((8, LANES), I32)],
        scratch_shapes=[pltpu.VMEM((1, LANES), F32)],
        compiler_params=_params(1),
        name="merge_route",
    )(ya, yb, yc, gates, x, wa, wb, wc, wo, fn, wr, br)


def _dispatch_kernel(slot_ref, h_ref, zero_ref, xs_ref, sem):
    del zero_ref
    td = h_ref.shape[0] // SUBLANES

    def rows(start):
        def body(t, carry):
            src = h_ref.at[pl.ds(pl.multiple_of(t * SUBLANES, SUBLANES), SUBLANES)]
            for k in range(TOP_K):
                s = slot_ref[0, 0, TOP_K * t + k]
                cp = pltpu.make_async_copy(
                    src, xs_ref.at[pl.ds(pl.multiple_of(s * SUBLANES, SUBLANES), SUBLANES)], sem)
                if start:
                    cp.start()
                else:
                    cp.wait()
            return carry

        lax.fori_loop(0, td, body, 0, unroll=4 if start else 8)

    rows(True)
    rows(False)


def _dispatch(h2, slots, n_slots, td=1024):
    n = h2.shape[0] // SUBLANES
    td = min(td, n)
    zeros = jnp.zeros((n_slots * SUBLANES, LANES), F32)
    return pl.pallas_call(
        _dispatch_kernel,
        grid=(n // td,),
        in_specs=[pl.BlockSpec((1, 1, TOP_K * td), lambda i: (i, 0, 0), memory_space=pltpu.SMEM),
                  pl.BlockSpec((td * SUBLANES, LANES), lambda i: (i, 0)),
                  pl.BlockSpec(memory_space=pl.ANY)],
        out_specs=pl.BlockSpec(memory_space=pl.ANY),
        out_shape=jax.ShapeDtypeStruct((n_slots * SUBLANES, LANES), F32),
        scratch_shapes=[pltpu.SemaphoreType.DMA(())],
        input_output_aliases={2: 0},
        compiler_params=_params(1),
        name="dispatch",
    )(slots.reshape(n // td, 1, TOP_K * td), h2, zeros)


def _expert_kernel(be_ref, nxt_ref, par_ref, xs_ref, wg_hbm, wu_hbm, wd_hbm, ys_ref,
                   wg_f, wu_f, wd_f, wg_s, wu_s, wd_s, sems, *, layer):
    i = pl.program_id(0)
    e = be_ref[i]
    slot = par_ref[i]
    first_of_expert = jnp.logical_or(i == 0, e != be_ref[jnp.maximum(i - 1, 0)])

    def weight_copies(expert, buf):
        return (pltpu.make_async_copy(wg_hbm.at[layer, expert], wg_f.at[buf], sems.at[buf, 0]),
                pltpu.make_async_copy(wu_hbm.at[layer, expert], wu_f.at[buf], sems.at[buf, 1]),
                pltpu.make_async_copy(wd_hbm.at[layer, expert], wd_f.at[buf], sems.at[buf, 2]))

    @pl.when(i == 0)
    def _():
        for cp in weight_copies(e, slot):
            cp.start()

    @pl.when(first_of_expert)
    def _():
        @pl.when(nxt_ref[i] != e)
        def _():
            for cp in weight_copies(nxt_ref[i], 1 - slot):
                cp.start()

        for cp in weight_copies(e, slot):
            cp.wait()
        wg_s[...] = wg_f[slot].astype(BF16)
        wu_s[...] = wu_f[slot].astype(BF16)
        wd_s[...] = wd_f[slot].astype(BF16)

    half = xs_ref.shape[0] // 2
    th = half // SUBLANES
    xs = [_load_token_tiles(xs_ref.at[pl.ds(p * half, half)], th).astype(BF16) for p in range(2)]
    gs = [jnp.dot(x, wg_s[...], preferred_element_type=F32) for x in xs]
    us = [jnp.dot(x, wu_s[...], preferred_element_type=F32) for x in xs]
    acts = [(_silu(g) * u).astype(BF16) for g, u in zip(gs, us)]
    for p in range(2):
        _store_token_tiles(ys_ref.at[pl.ds(p * half, half)],
                           jnp.dot(acts[p], wd_s[...], preferred_element_type=F32))


def _experts(xs, block_expert, next_expert, block_parity, wg, wu, wd, layer):
    n_slots = xs.shape[0] // SUBLANES
    tb = MOE_ROWS
    blk = lambda i, be, nx, par: (i, 0)
    grid_spec = pltpu.PrefetchScalarGridSpec(
        num_scalar_prefetch=3,
        grid=(n_slots // tb,),
        in_specs=[pl.BlockSpec((tb * SUBLANES, LANES), blk),
                  pl.BlockSpec(memory_space=pl.ANY), pl.BlockSpec(memory_space=pl.ANY),
                  pl.BlockSpec(memory_space=pl.ANY)],
        out_specs=pl.BlockSpec((tb * SUBLANES, LANES), blk),
        scratch_shapes=[pltpu.VMEM((2, D_MODEL, EXPERT_FF), F32), pltpu.VMEM((2, D_MODEL, EXPERT_FF), F32),
                        pltpu.VMEM((2, EXPERT_FF, D_MODEL), F32),
                        pltpu.VMEM((D_MODEL, EXPERT_FF), BF16), pltpu.VMEM((D_MODEL, EXPERT_FF), BF16),
                        pltpu.VMEM((EXPERT_FF, D_MODEL), BF16),
                        pltpu.SemaphoreType.DMA((2, 3))],
    )
    return pl.pallas_call(
        functools.partial(_expert_kernel, layer=layer),
        grid_spec=grid_spec,
        out_shape=jax.ShapeDtypeStruct((n_slots * SUBLANES, LANES), F32),
        compiler_params=_params(1),
        name="experts",
    )(block_expert, next_expert, block_parity, xs, wg, wu, wd)


def _combine_kernel(slot_ref, next_slot_ref, x1_ref, rw_ref, p_ref, ys_ref, pn_ref, wpg_ref, wpl_ref, o_ref,
                    ybuf, sems):
    i = pl.program_id(0)
    tp = x1_ref.shape[0]
    cur = i % 2

    def rows(sref, buf, start):
        def body(t, carry):
            for k in range(TOP_K):
                s = sref[0, 0, TOP_K * t + k]
                cp = pltpu.make_async_copy(
                    ys_ref.at[pl.ds(pl.multiple_of(s * SUBLANES, SUBLANES), SUBLANES)],
                    ybuf.at[buf, k, pl.ds(pl.multiple_of(t * SUBLANES, SUBLANES), SUBLANES)], sems.at[buf])
                if start:
                    cp.start()
                else:
                    cp.wait()
            return carry

        lax.fori_loop(0, tp, body, 0, unroll=4 if start else 8)

    @pl.when(i == 0)
    def _():
        rows(slot_ref, 0, True)

    @pl.when(i + 1 < pl.num_programs(0))
    def _():
        rows(next_slot_ref, 1 - cur, True)

    rows(slot_ref, cur, False)

    rw = rw_ref[...]
    x2 = (x1_ref[...] + rw[:, 0:1] * _load_token_tiles(ybuf.at[cur, 0], tp)
          + rw[:, 1:2] * _load_token_tiles(ybuf.at[cur, 1], tp))
    h3 = _rms(x2, pn_ref[...]).astype(BF16)
    pe = p_ref[0].astype(BF16)
    cw = 2 * LANES
    for c in range(D_MODEL // cw):
        cols = slice(c * cw, (c + 1) * cw)
        gate = _sigmoid(jnp.dot(h3, wpg_ref[:, cols], preferred_element_type=F32))
        o_ref[:, cols] = x2[:, cols] + gate * jnp.dot(pe, wpl_ref[:, cols], preferred_element_type=F32)


def _combine_ple(x1, rw, slots, ys, p, layer, pn, wpg, wpl, tp=256):
    n = x1.shape[0]
    d = D_MODEL
    blk = lambda i: (i, 0)
    const = lambda i: (0, 0)
    nsteps = n // tp
    slot_blocks = slots.reshape(nsteps, 1, TOP_K * tp)
    return pl.pallas_call(
        _combine_kernel,
        grid=(nsteps,),
        in_specs=[pl.BlockSpec((1, 1, TOP_K * tp), lambda i: (i, 0, 0), memory_space=pltpu.SMEM),
                  pl.BlockSpec((1, 1, TOP_K * tp), lambda i: (jnp.minimum(i + 1, nsteps - 1), 0, 0),
                               memory_space=pltpu.SMEM),
                  pl.BlockSpec((tp, d), blk), pl.BlockSpec((tp, LANES), blk),
                  pl.BlockSpec((1, tp, PLE_DIM), lambda i: (layer, i, 0)),
                  pl.BlockSpec(memory_space=pl.ANY),
                  pl.BlockSpec((1, d), const), pl.BlockSpec((d, d), const),
                  pl.BlockSpec((PLE_DIM, d), const)],
        out_specs=pl.BlockSpec((tp, d), blk),
        out_shape=jax.ShapeDtypeStruct((n, d), F32),
        scratch_shapes=[pltpu.VMEM((2, TOP_K, tp * SUBLANES, LANES), F32), pltpu.SemaphoreType.DMA((2,))],
        compiler_params=_params(1),
        name="combine_ple",
    )(slot_blocks, slot_blocks, x1, rw, p, ys, pn, wpg, wpl)


def _pack_w_in(w_in):
    o = IN_OFFSETS
    w = w_in.astype(BF16)
    pad = jnp.zeros((D_MODEL, LANES - GLA_GATE_RANK - 2 * GDN_HEADS), BF16)
    cols = [w[:, :o[6]], w[:, o[7]:o[9]], w[:, o[11]:], w[:, o[6]:o[7]], w[:, o[9]:o[11]], pad]
    return jnp.concatenate(cols, axis=1)


def _layer(x, p, layer, cos_t, sin_t, batch, seq, attn_norm, w_in, q_norm, k_norm, sinks, gla_wa2, gla_ba,
           gla_norm, gdn_conv, gdn_a_log, gdn_dt_bias, gdn_norm, w_br_a, w_br_b, w_br_c, w_o,
           ffn_norm, w_coarse, b_coarse, w_fine, b_fine, w_gate_e, w_up_e, w_down_e,
           ple_norm, w_ple_gate, w_ple):
    n = x.shape[0]
    aq, akv, bqk, bv, br, cqkv, cg, gates, small = _inproj(x, attn_norm[None], _pack_w_in(w_in))
    ya = _swa(aq, akv, cos_t, sin_t, q_norm, k_norm, sinks, batch, seq)
    yb = _gla(bqk, bv, br, small, gla_wa2, gla_ba, gla_norm, batch, seq)
    yc = _gdn(cqkv, cg, small, gdn_conv, gdn_a_log, gdn_dt_bias, gdn_norm, batch, seq)

    wr = jnp.zeros((D_MODEL, LANES), F32).at[:, :N_GROUPS].set(w_coarse)
    wr = wr.at[:, N_GROUPS:N_GROUPS + N_EXPERTS].set(w_fine)
    brow = jnp.zeros((1, LANES), F32).at[0, :N_GROUPS].set(b_coarse)
    brow = brow.at[0, N_GROUPS:N_GROUPS + N_EXPERTS].set(b_fine)
    x1, h2, ri, rw, cnt = _merge(ya, yb, yc, gates, x, w_br_a.astype(BF16), w_br_b.astype(BF16),
                                 w_br_c.astype(BF16), w_o.astype(BF16), ffn_norm[None], wr, brow)

    counts = cnt[0, :N_EXPERTS]
    padded = (counts + MOE_ROWS - 1) // MOE_ROWS * MOE_ROWS
    pad_end = jnp.cumsum(padded)
    pad_start = pad_end - padded
    slots = (pad_start[ri[:, :TOP_K]] + ri[:, TOP_K:2 * TOP_K]).astype(I32)
    n_slots = n * TOP_K + N_EXPERTS * MOE_ROWS
    block_start = jnp.arange(n_slots // MOE_ROWS, dtype=I32) * MOE_ROWS
    last_owner = jnp.max(jnp.where(padded > 0, jnp.arange(N_EXPERTS, dtype=I32), 0))
    block_expert = jnp.minimum(jnp.sum((pad_end[None, :] <= block_start[:, None]).astype(I32), axis=1),
                               last_owner).astype(I32)

    experts = jnp.arange(N_EXPERTS, dtype=I32)
    owns = padded > 0
    later = lax.cummin(jnp.where(owns, experts, N_EXPERTS)[::-1])[::-1]
    next_owner = jnp.concatenate([later[1:], jnp.full((1,), N_EXPERTS, I32)])
    next_owner = jnp.where(next_owner < N_EXPERTS, next_owner, experts)
    parity = (jnp.cumsum(owns.astype(I32)) - 1) % 2
    next_expert = next_owner[block_expert].astype(I32)
    block_parity = jnp.maximum(parity[block_expert], 0).astype(I32)

    xs = _dispatch(h2, slots, n_slots)
    ys = _experts(xs, block_expert, next_expert, block_parity, w_gate_e, w_up_e, w_down_e, layer)
    return _combine_ple(x1, rw, slots, ys, p, layer, ple_norm[None], w_ple_gate.astype(BF16),
                        w_ple.astype(BF16))


def kernel(x, p, positions, attn_norm, w_in, q_norm, k_norm, sinks, gla_wa2, gla_ba, gla_norm, gdn_conv, gdn_a_log, gdn_dt_bias, gdn_norm, w_br_a, w_br_b, w_br_c, w_o, ffn_norm, w_coarse, b_coarse, w_fine, b_fine, w_gate_e, w_up_e, w_down_e, ple_norm, w_ple_gate, w_ple):
    batch, seq, d = x.shape
    n = batch * seq
    depth = p.shape[0]
    cos_t, sin_t = _rope_tables(positions)
    xf = x.reshape(n, d)
    pf = p.reshape(depth, n, p.shape[-1])
    per_layer = (attn_norm, w_in, q_norm, k_norm, sinks, gla_wa2, gla_ba, gla_norm, gdn_conv, gdn_a_log,
                 gdn_dt_bias, gdn_norm, w_br_a, w_br_b, w_br_c, w_o, ffn_norm, w_coarse, b_coarse,
                 w_fine, b_fine, w_gate_e, w_up_e, w_down_e, ple_norm, w_ple_gate, w_ple)
    stacked = (w_gate_e, w_up_e, w_down_e)
    for i in range(depth):
        xf = _layer(xf, pf, i, cos_t, sin_t, batch, seq,
                    *[a if any(a is s for s in stacked) else a[i] for a in per_layer])
    return xf.reshape(batch, seq, d)
```

```python
import functools
import math

import numpy as np
import jax
import jax.numpy as jnp
from jax import lax
from jax.experimental import pallas as pl
from jax.experimental.pallas import tpu as pltpu

F32 = jnp.float32
BF16 = jnp.bfloat16
I32 = jnp.int32

D_MODEL = 1024
PLE_DIM = 256
NORM_EPS = 1e-6
MASK_VALUE = -1e30

SWA_Q_HEADS = 8
SWA_KV_HEADS = 2
SWA_HEAD_DIM = 64
SWA_WINDOW = 128
ROT_DIM = SWA_HEAD_DIM // 4
ROPE_THETA = 500000.0

GLA_HEADS = 4
GLA_DK = 64
GLA_DV = 128
GLA_GATE_RANK = 16
GLA_GATE_NORM = 16.0
CHUNK = 64

GDN_HEADS = 4
GDN_DK = 128
GDN_DV = 128
GDN_CONV = 4

N_GROUPS = 4
EXPERTS_PER_GROUP = 8
N_EXPERTS = N_GROUPS * EXPERTS_PER_GROUP
EXPERT_FF = 512
TOP_K = 2

SWA_Q = SWA_Q_HEADS * SWA_HEAD_DIM
SWA_KV = SWA_KV_HEADS * SWA_HEAD_DIM
GLA_K = GLA_HEADS * GLA_DK
GLA_V = GLA_HEADS * GLA_DV
GDN_K = GDN_HEADS * GDN_DK
GDN_V = GDN_HEADS * GDN_DV
GDN_QKV = 2 * GDN_K + GDN_V
IN_SPLITS = (SWA_Q, SWA_KV, SWA_KV, GLA_K, GLA_K, GLA_V, GLA_GATE_RANK, GLA_V,
             GDN_QKV, GDN_HEADS, GDN_HEADS, GDN_V, 3 * D_MODEL)
IN_OFFSETS = tuple(int(o) for o in np.cumsum((0,) + IN_SPLITS))

LANES = 128
SMALL_LR = 0
SMALL_BETA = GLA_GATE_RANK
SMALL_A = GLA_GATE_RANK + GDN_HEADS

OUT_WIDTHS = (SWA_Q, 2 * SWA_KV, 2 * GLA_K, GLA_V, GLA_V, GDN_QKV, GDN_V, 3 * D_MODEL, LANES)

MOE_ROWS = 256
VMEM_LIMIT = 56 * 1024 * 1024


def _params(n_axes):
    return pltpu.CompilerParams(dimension_semantics=("arbitrary",) * n_axes,
                                vmem_limit_bytes=VMEM_LIMIT)


def _dot(a, b):
    return jnp.dot(a.astype(BF16), b.astype(BF16), preferred_element_type=F32)


def _dot_nt(a, b):
    return lax.dot_general(a.astype(BF16), b.astype(BF16), (((1,), (1,)), ((), ())),
                           preferred_element_type=F32)


def _dot_tn(a, b):
    return lax.dot_general(a.astype(BF16), b.astype(BF16), (((0,), (0,)), ((), ())),
                           preferred_element_type=F32)


def _split2(x):
    hi = x.astype(BF16)
    lo = (x - hi.astype(F32)).astype(BF16)
    return hi, lo


def _dot_exact_lhs(a, x):
    hi = x.astype(BF16)
    r = x - hi.astype(F32)
    mid = r.astype(BF16)
    lo = (r - mid.astype(F32)).astype(BF16)
    a = a.astype(BF16)
    return (jnp.dot(a, hi, preferred_element_type=F32) + jnp.dot(a, mid, preferred_element_type=F32)
            + jnp.dot(a, lo, preferred_element_type=F32))


def _dot_hi(a, b):
    ah, al = _split2(a)
    bh, bl = _split2(b)
    return (jnp.dot(ah, bh, preferred_element_type=F32) + jnp.dot(ah, bl, preferred_element_type=F32)
            + jnp.dot(al, bh, preferred_element_type=F32))


def _rms(x, g):
    return x * lax.rsqrt(jnp.mean(x * x, axis=-1, keepdims=True) + NORM_EPS) * g


def _sigmoid(x):
    return 0.5 * jnp.tanh(0.5 * x) + 0.5


def _silu(x):
    return x * _sigmoid(x)


def _softplus(x):
    return jnp.maximum(x, 0.0) + jnp.log(1.0 + jnp.exp(-jnp.abs(x)))


def _log_sigmoid(x):
    return -_softplus(-x)


def _iota(shape, axis):
    return lax.broadcasted_iota(I32, shape, axis)


SUBLANES = 8
ROW_TILES = D_MODEL // LANES


def _store_token_tiles(ref, x):
    t = x.shape[0]
    for s in range(ROW_TILES):
        ref[pl.ds(s, t, stride=ROW_TILES), :] = x[:, s * LANES:(s + 1) * LANES]


def _load_token_tiles(ref, t):
    return jnp.concatenate([ref[pl.ds(s, t, stride=ROW_TILES), :] for s in range(ROW_TILES)], axis=-1)


def _chunk_tril(n, strict=False):
    r = _iota((n, n), 0)
    c = _iota((n, n), 1)
    same = (r // CHUNK) == (c // CHUNK)
    return same & ((c < r) if strict else (c <= r))


def _inproj_kernel(x_ref, g_ref, w_ref, *out_refs):
    h = _rms(x_ref[...], g_ref[...]).astype(BF16)
    off = 0
    for o_ref in out_refs:
        wd = o_ref.shape[-1]
        o_ref[...] = jnp.dot(h, w_ref[:, off:off + wd], preferred_element_type=F32)
        off += wd


def _inproj(x, g, w, tm=256):
    n = x.shape[0]
    nc = w.shape[1]
    return pl.pallas_call(
        _inproj_kernel,
        grid=(n // tm,),
        in_specs=[pl.BlockSpec((tm, D_MODEL), lambda i: (i, 0)),
                  pl.BlockSpec((1, D_MODEL), lambda i: (0, 0)),
                  pl.BlockSpec((D_MODEL, nc), lambda i: (0, 0), pipeline_mode=pl.Buffered(1))],
        out_specs=[pl.BlockSpec((tm, wd), lambda i: (i, 0)) for wd in OUT_WIDTHS],
        out_shape=[jax.ShapeDtypeStruct((n, wd), F32) for wd in OUT_WIDTHS],
        compiler_params=_params(1),
        name="inproj",
    )(x, g, w)


def _rope_kernel(pos_ref, invf_ref, cos_ref, sin_ref):
    ang = pos_ref[...].astype(F32) * invf_ref[...]
    d = _iota(ang.shape, 1) % SWA_HEAD_DIM
    half = ROT_DIM // 2
    cos_ref[...] = jnp.where(d < ROT_DIM, jnp.cos(ang), 1.0)
    s = jnp.sin(ang)
    sin_ref[...] = jnp.where(d < half, -s, jnp.where(d < ROT_DIM, s, 0.0))


def _rope_tables(positions, tm=1024):
    n = positions.size
    tm = min(tm, n)
    inv_freq = 1.0 / (ROPE_THETA ** (jnp.arange(0, ROT_DIM, 2, dtype=F32) / ROT_DIM))
    lane_freq = jnp.tile(jnp.concatenate([inv_freq, inv_freq, jnp.zeros(SWA_HEAD_DIM - ROT_DIM, F32)]),
                         LANES // SWA_HEAD_DIM)[None]
    return pl.pallas_call(
        _rope_kernel,
        grid=(n // tm,),
        in_specs=[pl.BlockSpec((tm, 1), lambda i: (i, 0)),
                  pl.BlockSpec((1, LANES), lambda i: (0, 0))],
        out_specs=[pl.BlockSpec((tm, LANES), lambda i: (i, 0))] * 2,
        out_shape=[jax.ShapeDtypeStruct((n, LANES), F32)] * 2,
        compiler_params=_params(1),
        name="rope_tables",
    )(positions.reshape(n, 1), lane_freq)


def _dot_exact_rhs(x, b):
    hi, lo = _split2(x)
    return jnp.dot(hi, b, preferred_element_type=F32) + jnp.dot(lo, b, preferred_element_type=F32)


def _rope_matrices():
    m = _iota((LANES, LANES), 0)
    l = _iota((LANES, LANES), 1)
    d = l % SWA_HEAD_DIM
    half = ROT_DIM // 2
    same_head = (m // SWA_HEAD_DIM) == (l // SWA_HEAD_DIM)
    partner = ((d < half) & (m == l + half)) | ((d >= half) & (d < ROT_DIM) & (m == l - half))
    return jnp.where(same_head, 1.0, 0.0).astype(BF16), jnp.where(partner, 1.0, 0.0).astype(BF16)


def _norm_rope(x, gain, cos, sin, same_head, partner):
    ms = _dot_exact_rhs(x * x, same_head) * (1.0 / SWA_HEAD_DIM)
    xn = x * lax.rsqrt(ms + NORM_EPS) * gain
    return xn * cos + _dot_exact_rhs(xn, partner) * sin


def _swa_kernel(sink_ref, q_ref, kv_ref, cos_ref, sin_ref, qn_ref, kn_ref, o_ref, kprev_ref, vprev_ref):
    j = pl.program_id(1)
    w = SWA_WINDOW
    hd = SWA_HEAD_DIM
    group = SWA_Q_HEADS // SWA_KV_HEADS

    @pl.when(j == 0)
    def _():
        kprev_ref[...] = jnp.zeros_like(kprev_ref)
        vprev_ref[...] = jnp.zeros_like(vprev_ref)

    cos = cos_ref[...]
    sin = sin_ref[...]
    same_head, partner = _rope_matrices()
    kc = _norm_rope(kv_ref[:, :SWA_KV], kn_ref[...], cos, sin, same_head, partner).astype(BF16)
    vc = kv_ref[:, SWA_KV:].astype(BF16)
    k_all = jnp.concatenate([kprev_ref[...], kc], axis=0)
    v_all = jnp.concatenate([vprev_ref[...], vc], axis=0)
    kprev_ref[...] = kc
    vprev_ref[...] = vc

    rows = group * w
    qi = _iota((rows, 2 * w), 0) % w
    kj = _iota((rows, 2 * w), 1)
    first_key = jnp.where(j > 0, 0, w)
    mask = (kj > qi) & (kj <= qi + w) & (kj >= first_key)
    head_of_row = _iota((rows, 1), 0) // w
    qrs = [_norm_rope(q_ref[:, c * LANES:(c + 1) * LANES], qn_ref[...], cos, sin, same_head, partner)
           for c in range(SWA_Q // LANES)]
    per_block = LANES // hd
    qgs, sinks = [], []
    for g in range(SWA_KV_HEADS):
        heads = range(g * group, (g + 1) * group)
        qgs.append(jnp.concatenate(
            [qrs[h // per_block][:, (h % per_block) * hd:(h % per_block + 1) * hd] for h in heads], axis=0))
        sk = jnp.full((rows, 1), sink_ref[g * group], F32)
        for i in range(1, group):
            sk = jnp.where(head_of_row == i, sink_ref[g * group + i], sk)
        sinks.append(sk)
    ss = [jnp.where(mask, _dot_nt(qgs[g], k_all[:, g * hd:(g + 1) * hd]) * (hd ** -0.5), MASK_VALUE)
          for g in range(SWA_KV_HEADS)]
    ms = [jnp.maximum(jnp.max(s, axis=-1, keepdims=True), sk) for s, sk in zip(ss, sinks)]
    es = [jnp.exp(s - m).astype(BF16) for s, m in zip(ss, ms)]
    ones = jnp.ones((2 * w, hd), BF16)
    dens = [jnp.dot(e, ones, preferred_element_type=F32) + jnp.exp(sk - m) for e, sk, m in zip(es, sinks, ms)]
    for g in range(SWA_KV_HEADS):
        og = jnp.dot(es[g], v_all[:, g * hd:(g + 1) * hd], preferred_element_type=F32) / dens[g]
        for i in range(group):
            h = g * group + i
            o_ref[:, h * hd:(h + 1) * hd] = og[i * w:(i + 1) * w]


def _swa(aq, akv, cos_t, sin_t, q_norm, k_norm, sinks, batch, seq):
    n = aq.shape[0]
    w = SWA_WINDOW
    nq = seq // w
    cur = lambda b, j: (b * nq + j, 0)
    rep = LANES // SWA_HEAD_DIM
    return pl.pallas_call(
        _swa_kernel,
        grid=(batch, nq),
        in_specs=[pl.BlockSpec(memory_space=pltpu.SMEM),
                  pl.BlockSpec((w, SWA_Q), cur),
                  pl.BlockSpec((w, 2 * SWA_KV), cur),
                  pl.BlockSpec((w, LANES), cur),
                  pl.BlockSpec((w, LANES), cur),
                  pl.BlockSpec((1, LANES), lambda b, j: (0, 0)),
                  pl.BlockSpec((1, LANES), lambda b, j: (0, 0))],
        out_specs=pl.BlockSpec((w, SWA_Q), cur),
        out_shape=jax.ShapeDtypeStruct((n, SWA_Q), F32),
        scratch_shapes=[pltpu.VMEM((w, SWA_KV), BF16), pltpu.VMEM((w, SWA_KV), BF16)],
        compiler_params=_params(2),
        name="swa",
    )(sinks, aq, akv, cos_t, sin_t, jnp.tile(q_norm, rep)[None], jnp.tile(k_norm, rep)[None])


def _gla_kernel(qk_ref, v_ref, r_ref, sm_ref, wa_ref, ba_ref, gn_ref, o_ref, state_ref):
    @pl.when(pl.program_id(1) == 0)
    def _():
        state_ref[...] = jnp.zeros_like(state_ref)

    tc = qk_ref.shape[0]
    la = _log_sigmoid(_dot(sm_ref[...], wa_ref[...]) + ba_ref[...]) * (1.0 / GLA_GATE_NORM)
    tri = jnp.where(_chunk_tril(tc), 1.0, 0.0).astype(BF16)
    b = _dot_exact_lhs(tri, la)
    causal = _iota((CHUNK, CHUNK), 1) <= _iota((CHUNK, CHUNK), 0)
    n_chunks = tc // CHUNK
    qes, dcols, intra, upd = [], [], {}, {}
    for c in range(n_chunks):
        rows = slice(c * CHUNK, (c + 1) * CHUNK)
        bc = b[rows]
        b_last = bc[CHUNK - 1:CHUNK]
        k = qk_ref[rows, GLA_K:]
        qe = qk_ref[rows, :GLA_K] * (GLA_DK ** -0.5) * jnp.exp(bc)
        ke = k * jnp.exp(-bc)
        kd = k * jnp.exp(b_last - bc)
        qes.append(qe)
        dcols.append(jnp.transpose(jnp.broadcast_to(jnp.exp(b_last), (GLA_DV, GLA_K))))
        for h in range(GLA_HEADS):
            ks = slice(h * GLA_DK, (h + 1) * GLA_DK)
            vh = v_ref[rows, h * GLA_DV:(h + 1) * GLA_DV]
            att = jnp.where(causal, _dot_nt(qe[:, ks], ke[:, ks]), 0.0)
            intra[c, h] = _dot(att, vh)
            upd[c, h] = _dot_tn(kd[:, ks], vh)
    for h in range(GLA_HEADS):
        ks = slice(h * GLA_DK, (h + 1) * GLA_DK)
        vs = slice(h * GLA_DV, (h + 1) * GLA_DV)
        st = state_ref[h]
        for c in range(n_chunks):
            rows = slice(c * CHUNK, (c + 1) * CHUNK)
            o = intra[c, h] + _dot(qes[c][:, ks], st)
            st = dcols[c][ks] * st + upd[c, h]
            o_ref[rows, vs] = _rms(o, gn_ref[...]) * _silu(r_ref[rows, vs])
        state_ref[h] = st


def _gla(bqk, bv, br, small, wa2, ba, gn, batch, seq, tc=256):
    n = bqk.shape[0]
    nt = seq // tc
    blk = lambda b, j: (b * nt + j, 0)
    const = lambda b, j: (0, 0)
    wa_pad = jnp.zeros((LANES, GLA_K), F32).at[SMALL_LR:SMALL_LR + GLA_GATE_RANK].set(wa2)
    return pl.pallas_call(
        _gla_kernel,
        grid=(batch, nt),
        in_specs=[pl.BlockSpec((tc, 2 * GLA_K), blk),
                  pl.BlockSpec((tc, GLA_V), blk),
                  pl.BlockSpec((tc, GLA_V), blk),
                  pl.BlockSpec((tc, LANES), blk),
                  pl.BlockSpec((LANES, GLA_K), const),
                  pl.BlockSpec((1, GLA_K), const),
                  pl.BlockSpec((1, GLA_DV), const)],
        out_specs=pl.BlockSpec((tc, GLA_V), blk),
        out_shape=jax.ShapeDtypeStruct((n, GLA_V), F32),
        scratch_shapes=[pltpu.VMEM((GLA_HEADS, GLA_DK, GLA_DV), F32)],
        compiler_params=_params(2),
        name="gla",
    )(bqk, bv, br, small, wa_pad, ba[None], gn[None])


def _l2(x):
    return x * lax.rsqrt(jnp.sum(x * x, axis=-1, keepdims=True) + NORM_EPS)


def _gdn_kernel(x_ref, xp_ref, g_ref, sm_ref, cw_ref, al_ref, dt_ref, gn_ref, o_ref, state_ref):
    first = pl.program_id(1) == 0

    @pl.when(first)
    def _():
        state_ref[...] = jnp.zeros_like(state_ref)

    tc = x_ref.shape[0]
    halo = xp_ref.shape[0]
    xprev = jnp.where(first, 0.0, xp_ref[...])
    xcat = jnp.concatenate([xprev, x_ref[...]], axis=0)
    conv = cw_ref[GDN_CONV - 1:GDN_CONV] * xcat[halo:halo + tc]
    for t in range(1, GDN_CONV):
        conv = conv + cw_ref[GDN_CONV - 1 - t:GDN_CONV - t] * xcat[halo - t:halo - t + tc]
    qkv = _silu(conv)

    sm = sm_ref[...]
    beta_all = _sigmoid(sm)
    g_all = -jnp.exp(al_ref[...]) * _softplus(sm + dt_ref[...])
    tri = jnp.where(_chunk_tril(tc), 1.0, 0.0).astype(BF16)
    big_g = _dot_exact_lhs(tri, g_all)
    g_rows = jnp.transpose(big_g)

    r = _iota((CHUNK, CHUNK), 0)
    cidx = _iota((CHUNK, CHUNK), 1)
    causal = cidx <= r
    strict = cidx < r
    eye = jnp.where(cidx == r, 1.0, 0.0)

    units = [(c, h) for c in range(tc // CHUNK) for h in range(GDN_HEADS)]
    qs, ks, egs, g_lasts, gcs, decays, lows, rhss = [], [], [], [], [], [], [], []
    for c, h in units:
        rows = slice(c * CHUNK, (c + 1) * CHUNK)
        q = _l2(qkv[rows, h * GDN_DK:(h + 1) * GDN_DK]) * (GDN_DK ** -0.5)
        k = _l2(qkv[rows, GDN_K + h * GDN_DK:GDN_K + (h + 1) * GDN_DK])
        v = qkv[rows, 2 * GDN_K + h * GDN_DV:2 * GDN_K + (h + 1) * GDN_DV]
        beta = beta_all[rows, SMALL_BETA + h:SMALL_BETA + h + 1]
        gc = big_g[rows, SMALL_A + h:SMALL_A + h + 1]
        g_row = g_rows[SMALL_A + h:SMALL_A + h + 1, c * CHUNK:(c + 1) * CHUNK]
        decay = jnp.where(causal, jnp.exp(jnp.where(causal, gc - g_row, 0.0)), 0.0)
        eg = jnp.exp(gc)
        qs.append(q)
        ks.append(k)
        egs.append(eg)
        gcs.append(gc)
        g_lasts.append(gc[CHUNK - 1:CHUNK])
        decays.append(decay)
        lows.append(jnp.where(strict, beta * _dot_nt(k, k) * decay, 0.0))
        rhss.append(jnp.concatenate([v * beta, k * (beta * eg)], axis=-1))
    invs = [eye - low for low in lows]
    pws = lows
    for _ in range(int(math.log2(CHUNK)) - 1):
        pws = [_dot(pw, pw) for pw in pws]
        invs = [inv + _dot(inv, pw) for inv, pw in zip(invs, pws)]
    sols = [_dot(inv, rhs) for inv, rhs in zip(invs, rhss)]
    atts = [_dot_nt(q, k) * decay for q, k, decay in zip(qs, ks, decays)]

    for i, (c, h) in enumerate(units):
        rows = slice(c * CHUNK, (c + 1) * CHUNK)
        u = sols[i][:, :GDN_DV]
        wm = sols[i][:, GDN_DV:]
        q_dec = qs[i] * egs[i]
        k_dec = ks[i] * jnp.exp(g_lasts[i] - gcs[i])
        st = state_ref[h]
        v_new = u - _dot(wm, st)
        o = _dot(q_dec, st) + _dot(atts[i], v_new)
        state_ref[h] = jnp.exp(g_lasts[i]) * st + _dot_tn(k_dec, v_new)
        vs = slice(h * GDN_DV, (h + 1) * GDN_DV)
        o_ref[rows, vs] = _rms(o, gn_ref[...]) * _silu(g_ref[rows, vs])


def _gdn(cqkv, cg, small, conv_w, a_log, dt_bias, gn, batch, seq, tc=256, halo=8):
    n = cqkv.shape[0]
    nt = seq // tc
    blk = lambda b, j: (b * nt + j, 0)
    const = lambda b, j: (0, 0)
    prev = lambda b, j: (jnp.maximum((b * nt + j) * (tc // halo) - 1, 0), 0)
    al_row = jnp.zeros((1, LANES), F32).at[0, SMALL_A:SMALL_A + GDN_HEADS].set(a_log)
    dt_row = jnp.zeros((1, LANES), F32).at[0, SMALL_A:SMALL_A + GDN_HEADS].set(dt_bias)
    return pl.pallas_call(
        _gdn_kernel,
        grid=(batch, nt),
        in_specs=[pl.BlockSpec((tc, GDN_QKV), blk),
                  pl.BlockSpec((halo, GDN_QKV), prev),
                  pl.BlockSpec((tc, GDN_V), blk),
                  pl.BlockSpec((tc, LANES), blk),
                  pl.BlockSpec((GDN_CONV, GDN_QKV), const),
                  pl.BlockSpec((1, LANES), const),
                  pl.BlockSpec((1, LANES), const),
                  pl.BlockSpec((1, GDN_DV), const)],
        out_specs=pl.BlockSpec((tc, GDN_V), blk),
        out_shape=jax.ShapeDtypeStruct((n, GDN_V), F32),
        scratch_shapes=[pltpu.VMEM((GDN_HEADS, GDN_DK, GDN_DV), F32)],
        compiler_params=_params(2),
        name="gdn",
    )(cqkv, cqkv, cg, small, conv_w, al_row, dt_row, gn[None])


def _first_argmax(x, lane):
    m = jnp.max(x, axis=-1, keepdims=True)
    idx = jnp.min(jnp.where(x == m, lane.astype(F32), float(LANES)), axis=-1, keepdims=True)
    return m, idx.astype(I32)


def _merge_kernel(ya_ref, yb_ref, yc_ref, gt_ref, x_ref, wa_ref, wb_ref, wc_ref, wo_ref, fn_ref,
                  wr_ref, br_ref, x1_ref, h2_ref, ri_ref, rw_ref, cnt_ref, carry_ref):
    @pl.when(pl.program_id(0) == 0)
    def _():
        carry_ref[...] = jnp.zeros_like(carry_ref)

    d = D_MODEL
    ya = ya_ref[...].astype(BF16)
    yb = yb_ref[...].astype(BF16)
    yc = yc_ref[...].astype(BF16)
    cw = 2 * LANES
    merged = []
    for c in range(d // cw):
        lo = c * cw
        merged.append((
            _sigmoid(gt_ref[:, lo:lo + cw]) * jnp.dot(ya, wa_ref[:, lo:lo + cw], preferred_element_type=F32)
            + _sigmoid(gt_ref[:, d + lo:d + lo + cw]) * jnp.dot(yb, wb_ref[:, lo:lo + cw], preferred_element_type=F32)
            + _sigmoid(gt_ref[:, 2 * d + lo:2 * d + lo + cw]) * jnp.dot(yc, wc_ref[:, lo:lo + cw], preferred_element_type=F32)
        ).astype(BF16))
    merged = jnp.concatenate(merged, axis=-1)
    x1 = x_ref[...] + jnp.dot(merged, wo_ref[...], preferred_element_type=F32)
    x1_ref[...] = x1
    h2 = _rms(x1, fn_ref[...])
    _store_token_tiles(h2_ref, h2)

    logits = _dot_hi(h2, wr_ref[...]) + br_ref[...]
    tm = logits.shape[0]
    lane = _iota((tm, LANES), 1)
    neg = -jnp.inf
    is_c = lane < N_GROUPS
    cm, g_idx = _first_argmax(jnp.where(is_c, logits, neg), lane)
    g_prob = 1.0 / jnp.sum(jnp.where(is_c, jnp.exp(logits - cm), 0.0), axis=-1, keepdims=True)
    sel = (lane >= N_GROUPS) & (((lane - N_GROUPS) // EXPERTS_PER_GROUP) == g_idx)
    fm = jnp.max(jnp.where(sel, logits, neg), axis=-1, keepdims=True)
    ef = jnp.where(sel, jnp.exp(logits - fm), 0.0)
    p1, i1 = _first_argmax(jnp.where(sel, ef, neg), lane)
    p2, i2 = _first_argmax(jnp.where(sel & (lane != i1), ef, neg), lane)
    w1 = g_prob * p1 / (p1 + p2)
    w2 = g_prob * p2 / (p1 + p2)
    e1 = i1 - N_GROUPS
    e2 = i2 - N_GROUPS

    oh = jnp.where((lane == e1) | (lane == e2 + N_EXPERTS), 1.0, 0.0)
    stril = jnp.where(_iota((tm, tm), 1) < _iota((tm, tm), 0), 1.0, 0.0).astype(BF16)
    before = jnp.dot(stril, oh.astype(BF16), preferred_element_type=F32)
    tot = jnp.sum(oh, axis=0, keepdims=True)
    tot_first = jnp.where(lane[:1] < N_EXPERTS, tot, 0.0)
    carry = carry_ref[...]
    base = carry + pltpu.roll(carry + tot_first, N_EXPERTS, 1)
    ranks = oh * (before + base)
    rank1 = jnp.sum(jnp.where(lane < N_EXPERTS, ranks, 0.0), axis=-1, keepdims=True)
    rank2 = jnp.sum(jnp.where(lane < N_EXPERTS, 0.0, ranks), axis=-1, keepdims=True)
    new_carry = carry + tot_first + pltpu.roll(tot - tot_first, LANES - N_EXPERTS, 1)
    carry_ref[...] = new_carry
    cnt_ref[...] = jnp.broadcast_to(new_carry, cnt_ref.shape).astype(I32)

    ri_ref[...] = jnp.where(lane == 0, e1, jnp.where(lane == 1, e2, jnp.where(
        lane == 2, rank1.astype(I32), jnp.where(lane == 3, rank2.astype(I32), 0))))
    rw_ref[...] = jnp.where(lane == 0, w1, jnp.where(lane == 1, w2, 0.0))


def _merge(ya, yb, yc, gates, x, wa, wb, wc, wo, fn, wr, br, tm=256):
    n = x.shape[0]
    d = D_MODEL
    blk = lambda i: (i, 0)
    const = lambda i: (0, 0)
    return pl.pallas_call(
        _merge_kernel,
        grid=(n // tm,),
        in_specs=[pl.BlockSpec((tm, SWA_Q), blk), pl.BlockSpec((tm, GLA_V), blk),
                  pl.BlockSpec((tm, GDN_V), blk), pl.BlockSpec((tm, 3 * d), blk),
                  pl.BlockSpec((tm, d), blk),
                  pl.BlockSpec((SWA_Q, d), const), pl.BlockSpec((GLA_V, d), const),
                  pl.BlockSpec((GDN_V, d), const), pl.BlockSpec((d, d), const),
                  pl.BlockSpec((1, d), const), pl.BlockSpec((d, LANES), const),
                  pl.BlockSpec((1, LANES), const)],
        out_specs=[pl.BlockSpec((tm, d), blk), pl.BlockSpec((tm * SUBLANES, LANES), blk),
                   pl.BlockSpec((tm, LANES), blk), pl.BlockSpec((tm, LANES), blk),
                   pl.BlockSpec((8, LANES), const)],
        out_shape=[jax.ShapeDtypeStruct((n, d), F32), jax.ShapeDtypeStruct((n * SUBLANES, LANES), F32),
                   jax.ShapeDtypeStruct((n, LANES), I32), jax.ShapeDtypeStruct((n, LANES), F32),
                   jax.ShapeDtypeStruct((8, LANES), I32)],
        scratch_shapes=[pltpu.VMEM((1, LANES), F32)],
        compiler_params=_params(1),
        name="merge_route",
    )(ya, yb, yc, gates, x, wa, wb, wc, wo, fn, wr, br)


def _dispatch_kernel(fill_ref, slot_ref, h_ref, xs_ref, zbuf, sem, zsem):
    td = h_ref.shape[0] // SUBLANES
    n_blocks = xs_ref.shape[0] // (MOE_ROWS * SUBLANES)

    def zero_fill(start):
        def run(cp):
            if start:
                cp.start()
            else:
                cp.wait()

        def pad_rows(e, carry):
            lo = fill_ref[e]
            n_pad = fill_ref[N_EXPERTS + e] - lo
            bit = MOE_ROWS // 2
            while bit >= 1:
                off = lo + (n_pad // (2 * bit)) * (2 * bit)

                @pl.when((n_pad & bit) != 0)
                def _(off=off, bit=bit):
                    run(pltpu.make_async_copy(
                        zbuf.at[pl.ds(0, bit * SUBLANES)],
                        xs_ref.at[pl.ds(pl.multiple_of(off * SUBLANES, SUBLANES), bit * SUBLANES)], zsem))

                bit //= 2
            return carry

        lax.fori_loop(0, N_EXPERTS, pad_rows, 0)

        def unused_block(b, carry):
            run(pltpu.make_async_copy(
                zbuf, xs_ref.at[pl.ds(pl.multiple_of(b * (MOE_ROWS * SUBLANES), SUBLANES), MOE_ROWS * SUBLANES)],
                zsem))
            return carry

        lax.fori_loop(fill_ref[2 * N_EXPERTS], n_blocks, unused_block, 0)

    @pl.when(pl.program_id(0) == 0)
    def _():
        zbuf[...] = jnp.zeros_like(zbuf)
        zero_fill(True)
        zero_fill(False)

    def rows(start):
        def body(t, carry):
            src = h_ref.at[pl.ds(pl.multiple_of(t * SUBLANES, SUBLANES), SUBLANES)]
            for k in range(TOP_K):
                s = slot_ref[0, 0, TOP_K * t + k]
                cp = pltpu.make_async_copy(
                    src, xs_ref.at[pl.ds(pl.multiple_of(s * SUBLANES, SUBLANES), SUBLANES)], sem)
                if start:
                    cp.start()
                else:
                    cp.wait()
            return carry

        lax.fori_loop(0, td, body, 0, unroll=4 if start else 8)

    rows(True)
    rows(False)


def _dispatch(h2, slots, fill, n_slots, td=1024):
    n = h2.shape[0] // SUBLANES
    td = min(td, n)
    grid_spec = pltpu.PrefetchScalarGridSpec(
        num_scalar_prefetch=1,
        grid=(n // td,),
        in_specs=[pl.BlockSpec((1, 1, TOP_K * td), lambda i, fill: (i, 0, 0), memory_space=pltpu.SMEM),
                  pl.BlockSpec((td * SUBLANES, LANES), lambda i, fill: (i, 0))],
        out_specs=pl.BlockSpec(memory_space=pl.ANY),
        scratch_shapes=[pltpu.VMEM((MOE_ROWS * SUBLANES, LANES), F32), pltpu.SemaphoreType.DMA(()),
                        pltpu.SemaphoreType.DMA(())],
    )
    return pl.pallas_call(
        _dispatch_kernel,
        grid_spec=grid_spec,
        out_shape=jax.ShapeDtypeStruct((n_slots * SUBLANES, LANES), F32),
        compiler_params=_params(1),
        name="dispatch",
    )(fill, slots.reshape(n // td, 1, TOP_K * td), h2)


def _expert_kernel(be_ref, nxt_ref, par_ref, xs_ref, wg_hbm, wu_hbm, wd_hbm, ys_ref,
                   wg_f, wu_f, wd_f, wg_s, wu_s, wd_s, sems, *, layer):
    i = pl.program_id(0)
    e = be_ref[i]
    slot = par_ref[i]
    first_of_expert = jnp.logical_or(i == 0, e != be_ref[jnp.maximum(i - 1, 0)])

    def weight_copies(expert, buf):
        return (pltpu.make_async_copy(wg_hbm.at[layer, expert], wg_f.at[buf], sems.at[buf, 0]),
                pltpu.make_async_copy(wu_hbm.at[layer, expert], wu_f.at[buf], sems.at[buf, 1]),
                pltpu.make_async_copy(wd_hbm.at[layer, expert], wd_f.at[buf], sems.at[buf, 2]))

    @pl.when(i == 0)
    def _():
        for cp in weight_copies(e, slot):
            cp.start()

    @pl.when(first_of_expert)
    def _():
        @pl.when(nxt_ref[i] != e)
        def _():
            for cp in weight_copies(nxt_ref[i], 1 - slot):
                cp.start()

        for cp in weight_copies(e, slot):
            cp.wait()
        wg_s[...] = wg_f[slot].astype(BF16)
        wu_s[...] = wu_f[slot].astype(BF16)
        wd_s[...] = wd_f[slot].astype(BF16)

    half = xs_ref.shape[0] // 2
    th = half // SUBLANES
    xs = [_load_token_tiles(xs_ref.at[pl.ds(p * half, half)], th).astype(BF16) for p in range(2)]
    gs = [jnp.dot(x, wg_s[...], preferred_element_type=F32) for x in xs]
    us = [jnp.dot(x, wu_s[...], preferred_element_type=F32) for x in xs]
    acts = [(_silu(g) * u).astype(BF16) for g, u in zip(gs, us)]
    for p in range(2):
        _store_token_tiles(ys_ref.at[pl.ds(p * half, half)],
                           jnp.dot(acts[p], wd_s[...], preferred_element_type=F32))


def _experts(xs, block_expert, next_expert, block_parity, wg, wu, wd, layer):
    n_slots = xs.shape[0] // SUBLANES
    tb = MOE_ROWS
    blk = lambda i, be, nx, par: (i, 0)
    grid_spec = pltpu.PrefetchScalarGridSpec(
        num_scalar_prefetch=3,
        grid=(n_slots // tb,),
        in_specs=[pl.BlockSpec((tb * SUBLANES, LANES), blk),
                  pl.BlockSpec(memory_space=pl.ANY), pl.BlockSpec(memory_space=pl.ANY),
                  pl.BlockSpec(memory_space=pl.ANY)],
        out_specs=pl.BlockSpec((tb * SUBLANES, LANES), blk),
        scratch_shapes=[pltpu.VMEM((2, D_MODEL, EXPERT_FF), F32), pltpu.VMEM((2, D_MODEL, EXPERT_FF), F32),
                        pltpu.VMEM((2, EXPERT_FF, D_MODEL), F32),
                        pltpu.VMEM((D_MODEL, EXPERT_FF), BF16), pltpu.VMEM((D_MODEL, EXPERT_FF), BF16),
                        pltpu.VMEM((EXPERT_FF, D_MODEL), BF16),
                        pltpu.SemaphoreType.DMA((2, 3))],
    )
    return pl.pallas_call(
        functools.partial(_expert_kernel, layer=layer),
        grid_spec=grid_spec,
        out_shape=jax.ShapeDtypeStruct((n_slots * SUBLANES, LANES), F32),
        compiler_params=_params(1),
        name="experts",
    )(block_expert, next_expert, block_parity, xs, wg, wu, wd)


def _combine_kernel(slot_ref, next_slot_ref, x1_ref, rw_ref, p_ref, ys_ref, pn_ref, wpg_ref, wpl_ref, o_ref,
                    ybuf, sems):
    i = pl.program_id(0)
    tp = x1_ref.shape[0]
    cur = i % 2

    def rows(sref, buf, start):
        def body(t, carry):
            for k in range(TOP_K):
                s = sref[0, 0, TOP_K * t + k]
                cp = pltpu.make_async_copy(
                    ys_ref.at[pl.ds(pl.multiple_of(s * SUBLANES, SUBLANES), SUBLANES)],
                    ybuf.at[buf, k, pl.ds(pl.multiple_of(t * SUBLANES, SUBLANES), SUBLANES)], sems.at[buf])
                if start:
                    cp.start()
                else:
                    cp.wait()
            return carry

        lax.fori_loop(0, tp, body, 0, unroll=4 if start else 8)

    @pl.when(i == 0)
    def _():
        rows(slot_ref, 0, True)

    @pl.when(i + 1 < pl.num_programs(0))
    def _():
        rows(next_slot_ref, 1 - cur, True)

    rows(slot_ref, cur, False)

    rw = rw_ref[...]
    x2 = (x1_ref[...] + rw[:, 0:1] * _load_token_tiles(ybuf.at[cur, 0], tp)
          + rw[:, 1:2] * _load_token_tiles(ybuf.at[cur, 1], tp))
    h3 = _rms(x2, pn_ref[...]).astype(BF16)
    pe = p_ref[0].astype(BF16)
    cw = 2 * LANES
    for c in range(D_MODEL // cw):
        cols = slice(c * cw, (c + 1) * cw)
        gate = _sigmoid(jnp.dot(h3, wpg_ref[:, cols], preferred_element_type=F32))
        o_ref[:, cols] = x2[:, cols] + gate * jnp.dot(pe, wpl_ref[:, cols], preferred_element_type=F32)


def _combine_ple(x1, rw, slots, ys, p, layer, pn, wpg, wpl, tp=256):
    n = x1.shape[0]
    d = D_MODEL
    blk = lambda i: (i, 0)
    const = lambda i: (0, 0)
    nsteps = n // tp
    slot_blocks = slots.reshape(nsteps, 1, TOP_K * tp)
    return pl.pallas_call(
        _combine_kernel,
        grid=(nsteps,),
        in_specs=[pl.BlockSpec((1, 1, TOP_K * tp), lambda i: (i, 0, 0), memory_space=pltpu.SMEM),
                  pl.BlockSpec((1, 1, TOP_K * tp), lambda i: (jnp.minimum(i + 1, nsteps - 1), 0, 0),
                               memory_space=pltpu.SMEM),
                  pl.BlockSpec((tp, d), blk), pl.BlockSpec((tp, LANES), blk),
                  pl.BlockSpec((1, tp, PLE_DIM), lambda i: (layer, i, 0)),
                  pl.BlockSpec(memory_space=pl.ANY),
                  pl.BlockSpec((1, d), const), pl.BlockSpec((d, d), const),
                  pl.BlockSpec((PLE_DIM, d), const)],
        out_specs=pl.BlockSpec((tp, d), blk),
        out_shape=jax.ShapeDtypeStruct((n, d), F32),
        scratch_shapes=[pltpu.VMEM((2, TOP_K, tp * SUBLANES, LANES), F32), pltpu.SemaphoreType.DMA((2,))],
        compiler_params=_params(1),
        name="combine_ple",
    )(slot_blocks, slot_blocks, x1, rw, p, ys, pn, wpg, wpl)


def _pack_w_in(w_in):
    o = IN_OFFSETS
    w = w_in.astype(BF16)
    pad = jnp.zeros((D_MODEL, LANES - GLA_GATE_RANK - 2 * GDN_HEADS), BF16)
    cols = [w[:, :o[6]], w[:, o[7]:o[9]], w[:, o[11]:], w[:, o[6]:o[7]], w[:, o[9]:o[11]], pad]
    return jnp.concatenate(cols, axis=1)


def _layer(x, p, layer, cos_t, sin_t, batch, seq, attn_norm, w_in, q_norm, k_norm, sinks, gla_wa2, gla_ba,
           gla_norm, gdn_conv, gdn_a_log, gdn_dt_bias, gdn_norm, w_br_a, w_br_b, w_br_c, w_o,
           ffn_norm, w_coarse, b_coarse, w_fine, b_fine, w_gate_e, w_up_e, w_down_e,
           ple_norm, w_ple_gate, w_ple):
    n = x.shape[0]
    aq, akv, bqk, bv, br, cqkv, cg, gates, small = _inproj(x, attn_norm[None], _pack_w_in(w_in))
    ya = _swa(aq, akv, cos_t, sin_t, q_norm, k_norm, sinks, batch, seq)
    yb = _gla(bqk, bv, br, small, gla_wa2, gla_ba, gla_norm, batch, seq)
    yc = _gdn(cqkv, cg, small, gdn_conv, gdn_a_log, gdn_dt_bias, gdn_norm, batch, seq)

    wr = jnp.zeros((D_MODEL, LANES), F32).at[:, :N_GROUPS].set(w_coarse)
    wr = wr.at[:, N_GROUPS:N_GROUPS + N_EXPERTS].set(w_fine)
    brow = jnp.zeros((1, LANES), F32).at[0, :N_GROUPS].set(b_coarse)
    brow = brow.at[0, N_GROUPS:N_GROUPS + N_EXPERTS].set(b_fine)
    x1, h2, ri, rw, cnt = _merge(ya, yb, yc, gates, x, w_br_a.astype(BF16), w_br_b.astype(BF16),
                                 w_br_c.astype(BF16), w_o.astype(BF16), ffn_norm[None], wr, brow)

    counts = cnt[0, :N_EXPERTS]
    padded = (counts + MOE_ROWS - 1) // MOE_ROWS * MOE_ROWS
    pad_end = jnp.cumsum(padded)
    pad_start = pad_end - padded
    slots = (pad_start[ri[:, :TOP_K]] + ri[:, TOP_K:2 * TOP_K]).astype(I32)
    n_slots = n * TOP_K + N_EXPERTS * MOE_ROWS
    block_start = jnp.arange(n_slots // MOE_ROWS, dtype=I32) * MOE_ROWS
    last_owner = jnp.max(jnp.where(padded > 0, jnp.arange(N_EXPERTS, dtype=I32), 0))
    block_expert = jnp.minimum(jnp.sum((pad_end[None, :] <= block_start[:, None]).astype(I32), axis=1),
                               last_owner).astype(I32)

    experts = jnp.arange(N_EXPERTS, dtype=I32)
    owns = padded > 0
    later = lax.cummin(jnp.where(owns, experts, N_EXPERTS)[::-1])[::-1]
    next_owner = jnp.concatenate([later[1:], jnp.full((1,), N_EXPERTS, I32)])
    next_owner = jnp.where(next_owner < N_EXPERTS, next_owner, experts)
    parity = (jnp.cumsum(owns.astype(I32)) - 1) % 2
    next_expert = next_owner[block_expert].astype(I32)
    block_parity = jnp.maximum(parity[block_expert], 0).astype(I32)

    fill = jnp.concatenate([pad_start + counts, pad_end, pad_end[-1:] // MOE_ROWS]).astype(I32)
    xs = _dispatch(h2, slots, fill, n_slots)
    ys = _experts(xs, block_expert, next_expert, block_parity, w_gate_e, w_up_e, w_down_e, layer)
    return _combine_ple(x1, rw, slots, ys, p, layer, ple_norm[None], w_ple_gate.astype(BF16),
                        w_ple.astype(BF16))


def kernel(x, p, positions, attn_norm, w_in, q_norm, k_norm, sinks, gla_wa2, gla_ba, gla_norm, gdn_conv, gdn_a_log, gdn_dt_bias, gdn_norm, w_br_a, w_br_b, w_br_c, w_o, ffn_norm, w_coarse, b_coarse, w_fine, b_fine, w_gate_e, w_up_e, w_down_e, ple_norm, w_ple_gate, w_ple):
    batch, seq, d = x.shape
    n = batch * seq
    depth = p.shape[0]
    cos_t, sin_t = _rope_tables(positions)
    xf = x.reshape(n, d)
    pf = p.reshape(depth, n, p.shape[-1])
    per_layer = (attn_norm, w_in, q_norm, k_norm, sinks, gla_wa2, gla_ba, gla_norm, gdn_conv, gdn_a_log,
                 gdn_dt_bias, gdn_norm, w_br_a, w_br_b, w_br_c, w_o, ffn_norm, w_coarse, b_coarse,
                 w_fine, b_fine, w_gate_e, w_up_e, w_down_e, ple_norm, w_ple_gate, w_ple)
    stacked = (w_gate_e, w_up_e, w_down_e)
    for i in range(depth):
        xf = _layer(xf, pf, i, cos_t, sin_t, batch, seq,
                    *[a if any(a is s for s in stacked) else a[i] for a in per_layer])
    return xf.reshape(batch, seq, d)
```

```python
import functools
import math

import numpy as np
import jax
import jax.numpy as jnp
from jax import lax
from jax.experimental import pallas as pl
from jax.experimental.pallas import tpu as pltpu

F32 = jnp.float32
BF16 = jnp.bfloat16
I32 = jnp.int32

D_MODEL = 1024
PLE_DIM = 256
NORM_EPS = 1e-6
MASK_VALUE = -1e30

SWA_Q_HEADS = 8
SWA_KV_HEADS = 2
SWA_HEAD_DIM = 64
SWA_WINDOW = 128
ROT_DIM = SWA_HEAD_DIM // 4
ROPE_THETA = 500000.0

GLA_HEADS = 4
GLA_DK = 64
GLA_DV = 128
GLA_GATE_RANK = 16
GLA_GATE_NORM = 16.0
CHUNK = 64

GDN_HEADS = 4
GDN_DK = 128
GDN_DV = 128
GDN_CONV = 4

N_GROUPS = 4
EXPERTS_PER_GROUP = 8
N_EXPERTS = N_GROUPS * EXPERTS_PER_GROUP
EXPERT_FF = 512
TOP_K = 2

SWA_Q = SWA_Q_HEADS * SWA_HEAD_DIM
SWA_KV = SWA_KV_HEADS * SWA_HEAD_DIM
GLA_K = GLA_HEADS * GLA_DK
GLA_V = GLA_HEADS * GLA_DV
GDN_K = GDN_HEADS * GDN_DK
GDN_V = GDN_HEADS * GDN_DV
GDN_QKV = 2 * GDN_K + GDN_V
IN_SPLITS = (SWA_Q, SWA_KV, SWA_KV, GLA_K, GLA_K, GLA_V, GLA_GATE_RANK, GLA_V,
             GDN_QKV, GDN_HEADS, GDN_HEADS, GDN_V, 3 * D_MODEL)
IN_OFFSETS = tuple(int(o) for o in np.cumsum((0,) + IN_SPLITS))

LANES = 128
SMALL_LR = 0
SMALL_BETA = GLA_GATE_RANK
SMALL_A = GLA_GATE_RANK + GDN_HEADS

OUT_WIDTHS = (SWA_Q, 2 * SWA_KV, 2 * GLA_K, GLA_V, GLA_V, GDN_QKV, GDN_V, 3 * D_MODEL, LANES)

MOE_ROWS = 256
VMEM_LIMIT = 56 * 1024 * 1024


def _params(n_axes):
    return pltpu.CompilerParams(dimension_semantics=("arbitrary",) * n_axes,
                                vmem_limit_bytes=VMEM_LIMIT)


def _dot(a, b):
    return jnp.dot(a.astype(BF16), b.astype(BF16), preferred_element_type=F32)


def _dot_nt(a, b):
    return lax.dot_general(a.astype(BF16), b.astype(BF16), (((1,), (1,)), ((), ())),
                           preferred_element_type=F32)


def _dot_tn(a, b):
    return lax.dot_general(a.astype(BF16), b.astype(BF16), (((0,), (0,)), ((), ())),
                           preferred_element_type=F32)


def _split2(x):
    hi = x.astype(BF16)
    lo = (x - hi.astype(F32)).astype(BF16)
    return hi, lo


def _dot_exact_lhs(a, x):
    hi = x.astype(BF16)
    r = x - hi.astype(F32)
    mid = r.astype(BF16)
    lo = (r - mid.astype(F32)).astype(BF16)
    a = a.astype(BF16)
    return (jnp.dot(a, hi, preferred_element_type=F32) + jnp.dot(a, mid, preferred_element_type=F32)
            + jnp.dot(a, lo, preferred_element_type=F32))


def _dot_hi(a, b):
    ah, al = _split2(a)
    bh, bl = _split2(b)
    return (jnp.dot(ah, bh, preferred_element_type=F32) + jnp.dot(ah, bl, preferred_element_type=F32)
            + jnp.dot(al, bh, preferred_element_type=F32))


def _rms(x, g):
    return x * lax.rsqrt(jnp.mean(x * x, axis=-1, keepdims=True) + NORM_EPS) * g


def _sigmoid(x):
    return 0.5 * jnp.tanh(0.5 * x) + 0.5


def _silu(x):
    return x * _sigmoid(x)


def _softplus(x):
    return jnp.maximum(x, 0.0) + jnp.log(1.0 + jnp.exp(-jnp.abs(x)))


def _log_sigmoid(x):
    return -_softplus(-x)


def _iota(shape, axis):
    return lax.broadcasted_iota(I32, shape, axis)


SUBLANES = 8
ROW_TILES = D_MODEL // LANES


def _store_token_tiles(ref, x):
    t = x.shape[0]
    for s in range(ROW_TILES):
        ref[pl.ds(s, t, stride=ROW_TILES), :] = x[:, s * LANES:(s + 1) * LANES]


def _load_token_tiles(ref, t):
    return jnp.concatenate([ref[pl.ds(s, t, stride=ROW_TILES), :] for s in range(ROW_TILES)], axis=-1)


def _chunk_tril(n, strict=False):
    r = _iota((n, n), 0)
    c = _iota((n, n), 1)
    same = (r // CHUNK) == (c // CHUNK)
    return same & ((c < r) if strict else (c <= r))


def _inproj_kernel(x_ref, g_ref, w_ref, *out_refs):
    h = _rms(x_ref[...], g_ref[...]).astype(BF16)
    off = 0
    for o_ref in out_refs:
        wd = o_ref.shape[-1]
        o_ref[...] = jnp.dot(h, w_ref[:, off:off + wd], preferred_element_type=F32)
        off += wd


def _inproj(x, g, w, tm=256):
    n = x.shape[0]
    nc = w.shape[1]
    return pl.pallas_call(
        _inproj_kernel,
        grid=(n // tm,),
        in_specs=[pl.BlockSpec((tm, D_MODEL), lambda i: (i, 0)),
                  pl.BlockSpec((1, D_MODEL), lambda i: (0, 0)),
                  pl.BlockSpec((D_MODEL, nc), lambda i: (0, 0), pipeline_mode=pl.Buffered(1))],
        out_specs=[pl.BlockSpec((tm, wd), lambda i: (i, 0)) for wd in OUT_WIDTHS],
        out_shape=[jax.ShapeDtypeStruct((n, wd), F32) for wd in OUT_WIDTHS],
        compiler_params=_params(1),
        name="inproj",
    )(x, g, w)


def _rope_kernel(pos_ref, invf_ref, cos_ref, sin_ref):
    ang = pos_ref[...].astype(F32) * invf_ref[...]
    d = _iota(ang.shape, 1) % SWA_HEAD_DIM
    half = ROT_DIM // 2
    cos_ref[...] = jnp.where(d < ROT_DIM, jnp.cos(ang), 1.0)
    s = jnp.sin(ang)
    sin_ref[...] = jnp.where(d < half, -s, jnp.where(d < ROT_DIM, s, 0.0))


def _rope_tables(positions, tm=1024):
    n = positions.size
    tm = min(tm, n)
    inv_freq = 1.0 / (ROPE_THETA ** (jnp.arange(0, ROT_DIM, 2, dtype=F32) / ROT_DIM))
    lane_freq = jnp.tile(jnp.concatenate([inv_freq, inv_freq, jnp.zeros(SWA_HEAD_DIM - ROT_DIM, F32)]),
                         LANES // SWA_HEAD_DIM)[None]
    return pl.pallas_call(
        _rope_kernel,
        grid=(n // tm,),
        in_specs=[pl.BlockSpec((tm, 1), lambda i: (i, 0)),
                  pl.BlockSpec((1, LANES), lambda i: (0, 0))],
        out_specs=[pl.BlockSpec((tm, LANES), lambda i: (i, 0))] * 2,
        out_shape=[jax.ShapeDtypeStruct((n, LANES), F32)] * 2,
        compiler_params=_params(1),
        name="rope_tables",
    )(positions.reshape(n, 1), lane_freq)


def _dot_exact_rhs(x, b):
    hi, lo = _split2(x)
    return jnp.dot(hi, b, preferred_element_type=F32) + jnp.dot(lo, b, preferred_element_type=F32)


def _rope_matrices():
    m = _iota((LANES, LANES), 0)
    l = _iota((LANES, LANES), 1)
    d = l % SWA_HEAD_DIM
    half = ROT_DIM // 2
    same_head = (m // SWA_HEAD_DIM) == (l // SWA_HEAD_DIM)
    partner = ((d < half) & (m == l + half)) | ((d >= half) & (d < ROT_DIM) & (m == l - half))
    return jnp.where(same_head, 1.0, 0.0).astype(BF16), jnp.where(partner, 1.0, 0.0).astype(BF16)


def _norm_rope(x, gain, cos, sin, same_head, partner):
    ms = _dot_exact_rhs(x * x, same_head) * (1.0 / SWA_HEAD_DIM)
    xn = x * lax.rsqrt(ms + NORM_EPS) * gain
    return xn * cos + _dot_exact_rhs(xn, partner) * sin


def _swa_kernel(sink_ref, q_ref, kv_ref, cos_ref, sin_ref, qn_ref, kn_ref, o_ref, kprev_ref, vprev_ref):
    j = pl.program_id(1)
    w = SWA_WINDOW
    hd = SWA_HEAD_DIM
    group = SWA_Q_HEADS // SWA_KV_HEADS

    @pl.when(j == 0)
    def _():
        kprev_ref[...] = jnp.zeros_like(kprev_ref)
        vprev_ref[...] = jnp.zeros_like(vprev_ref)

    cos = cos_ref[...]
    sin = sin_ref[...]
    same_head, partner = _rope_matrices()
    kc = _norm_rope(kv_ref[:, :SWA_KV], kn_ref[...], cos, sin, same_head, partner).astype(BF16)
    vc = kv_ref[:, SWA_KV:].astype(BF16)
    k_all = jnp.concatenate([kprev_ref[...], kc], axis=0)
    v_all = jnp.concatenate([vprev_ref[...], vc], axis=0)
    kprev_ref[...] = kc
    vprev_ref[...] = vc

    rows = group * w
    qi = _iota((rows, 2 * w), 0) % w
    kj = _iota((rows, 2 * w), 1)
    first_key = jnp.where(j > 0, 0, w)
    mask = (kj > qi) & (kj <= qi + w) & (kj >= first_key)
    head_of_row = _iota((rows, 1), 0) // w
    qrs = [_norm_rope(q_ref[:, c * LANES:(c + 1) * LANES], qn_ref[...], cos, sin, same_head, partner)
           for c in range(SWA_Q // LANES)]
    per_block = LANES // hd
    qgs, sinks = [], []
    for g in range(SWA_KV_HEADS):
        heads = range(g * group, (g + 1) * group)
        qgs.append(jnp.concatenate(
            [qrs[h // per_block][:, (h % per_block) * hd:(h % per_block + 1) * hd] for h in heads], axis=0))
        sk = jnp.full((rows, 1), sink_ref[g * group], F32)
        for i in range(1, group):
            sk = jnp.where(head_of_row == i, sink_ref[g * group + i], sk)
        sinks.append(sk)
    ss = [jnp.where(mask, _dot_nt(qgs[g], k_all[:, g * hd:(g + 1) * hd]) * (hd ** -0.5), MASK_VALUE)
          for g in range(SWA_KV_HEADS)]
    ms = [jnp.maximum(jnp.max(s, axis=-1, keepdims=True), sk) for s, sk in zip(ss, sinks)]
    es = [jnp.exp(s - m).astype(BF16) for s, m in zip(ss, ms)]
    ones = jnp.ones((2 * w, hd), BF16)
    dens = [jnp.dot(e, ones, preferred_element_type=F32) + jnp.exp(sk - m) for e, sk, m in zip(es, sinks, ms)]
    for g in range(SWA_KV_HEADS):
        og = jnp.dot(es[g], v_all[:, g * hd:(g + 1) * hd], preferred_element_type=F32) / dens[g]
        for i in range(group):
            h = g * group + i
            o_ref[:, h * hd:(h + 1) * hd] = og[i * w:(i + 1) * w]


def _swa(aq, akv, cos_t, sin_t, q_norm, k_norm, sinks, batch, seq):
    n = aq.shape[0]
    w = SWA_WINDOW
    nq = seq // w
    cur = lambda b, j: (b * nq + j, 0)
    rep = LANES // SWA_HEAD_DIM
    return pl.pallas_call(
        _swa_kernel,
        grid=(batch, nq),
        in_specs=[pl.BlockSpec(memory_space=pltpu.SMEM),
                  pl.BlockSpec((w, SWA_Q), cur),
                  pl.BlockSpec((w, 2 * SWA_KV), cur),
                  pl.BlockSpec((w, LANES), cur),
                  pl.BlockSpec((w, LANES), cur),
                  pl.BlockSpec((1, LANES), lambda b, j: (0, 0)),
                  pl.BlockSpec((1, LANES), lambda b, j: (0, 0))],
        out_specs=pl.BlockSpec((w, SWA_Q), cur),
        out_shape=jax.ShapeDtypeStruct((n, SWA_Q), F32),
        scratch_shapes=[pltpu.VMEM((w, SWA_KV), BF16), pltpu.VMEM((w, SWA_KV), BF16)],
        compiler_params=_params(2),
        name="swa",
    )(sinks, aq, akv, cos_t, sin_t, jnp.tile(q_norm, rep)[None], jnp.tile(k_norm, rep)[None])


def _gla_kernel(qk_ref, v_ref, r_ref, sm_ref, wa_ref, ba_ref, gn_ref, o_ref, state_ref):
    @pl.when(pl.program_id(1) == 0)
    def _():
        state_ref[...] = jnp.zeros_like(state_ref)

    tc = qk_ref.shape[0]
    la = _log_sigmoid(_dot(sm_ref[...], wa_ref[...]) + ba_ref[...]) * (1.0 / GLA_GATE_NORM)
    tri = jnp.where(_chunk_tril(tc), 1.0, 0.0).astype(BF16)
    b = _dot_exact_lhs(tri, la)
    causal = _iota((CHUNK, CHUNK), 1) <= _iota((CHUNK, CHUNK), 0)
    n_chunks = tc // CHUNK
    qes, dcols, intra, upd = [], [], {}, {}
    for c in range(n_chunks):
        rows = slice(c * CHUNK, (c + 1) * CHUNK)
        bc = b[rows]
        b_last = bc[CHUNK - 1:CHUNK]
        k = qk_ref[rows, GLA_K:]
        qe = qk_ref[rows, :GLA_K] * (GLA_DK ** -0.5) * jnp.exp(bc)
        ke = k * jnp.exp(-bc)
        kd = k * jnp.exp(b_last - bc)
        qes.append(qe)
        dcols.append(jnp.transpose(jnp.broadcast_to(jnp.exp(b_last), (GLA_DV, GLA_K))))
        for h in range(GLA_HEADS):
            ks = slice(h * GLA_DK, (h + 1) * GLA_DK)
            vh = v_ref[rows, h * GLA_DV:(h + 1) * GLA_DV]
            att = jnp.where(causal, _dot_nt(qe[:, ks], ke[:, ks]), 0.0)
            intra[c, h] = _dot(att, vh)
            upd[c, h] = _dot_tn(kd[:, ks], vh)
    for h in range(GLA_HEADS):
        ks = slice(h * GLA_DK, (h + 1) * GLA_DK)
        vs = slice(h * GLA_DV, (h + 1) * GLA_DV)
        st = state_ref[h]
        for c in range(n_chunks):
            rows = slice(c * CHUNK, (c + 1) * CHUNK)
            o = intra[c, h] + _dot(qes[c][:, ks], st)
            st = dcols[c][ks] * st + upd[c, h]
            o_ref[rows, vs] = _rms(o, gn_ref[...]) * _silu(r_ref[rows, vs])
        state_ref[h] = st


def _gla(bqk, bv, br, small, wa2, ba, gn, batch, seq, tc=256):
    n = bqk.shape[0]
    nt = seq // tc
    blk = lambda b, j: (b * nt + j, 0)
    const = lambda b, j: (0, 0)
    wa_pad = jnp.zeros((LANES, GLA_K), F32).at[SMALL_LR:SMALL_LR + GLA_GATE_RANK].set(wa2)
    return pl.pallas_call(
        _gla_kernel,
        grid=(batch, nt),
        in_specs=[pl.BlockSpec((tc, 2 * GLA_K), blk),
                  pl.BlockSpec((tc, GLA_V), blk),
                  pl.BlockSpec((tc, GLA_V), blk),
                  pl.BlockSpec((tc, LANES), blk),
                  pl.BlockSpec((LANES, GLA_K), const),
                  pl.BlockSpec((1, GLA_K), const),
                  pl.BlockSpec((1, GLA_DV), const)],
        out_specs=pl.BlockSpec((tc, GLA_V), blk),
        out_shape=jax.ShapeDtypeStruct((n, GLA_V), F32),
        scratch_shapes=[pltpu.VMEM((GLA_HEADS, GLA_DK, GLA_DV), F32)],
        compiler_params=_params(2),
        name="gla",
    )(bqk, bv, br, small, wa_pad, ba[None], gn[None])


def _l2(x):
    return x * lax.rsqrt(jnp.sum(x * x, axis=-1, keepdims=True) + NORM_EPS)


def _gdn_kernel(x_ref, xp_ref, g_ref, sm_ref, cw_ref, al_ref, dt_ref, gn_ref, o_ref, state_ref):
    first = pl.program_id(1) == 0

    @pl.when(first)
    def _():
        state_ref[...] = jnp.zeros_like(state_ref)

    tc = x_ref.shape[0]
    halo = xp_ref.shape[0]
    x = x_ref[...]
    xprev = jnp.where(first, 0.0, xp_ref[...])
    sm = sm_ref[...]
    row = _iota((halo, 1), 0)
    conv = cw_ref[GDN_CONV - 1:GDN_CONV] * x
    for t in range(1, GDN_CONV):
        rolled = pltpu.roll(x, t, 0)
        head = jnp.where(row < t, pltpu.roll(xprev, t, 0), rolled[:halo])
        shifted = jnp.concatenate([head, rolled[halo:]], axis=0)
        conv = conv + cw_ref[GDN_CONV - 1 - t:GDN_CONV - t] * shifted
    qkv = _silu(conv)

    beta_all = _sigmoid(sm)
    g_all = -jnp.exp(al_ref[...]) * _softplus(sm + dt_ref[...])
    tri = jnp.where(_chunk_tril(tc), 1.0, 0.0).astype(BF16)
    big_g = _dot_exact_lhs(tri, g_all)
    g_rows = jnp.transpose(big_g)

    r = _iota((CHUNK, CHUNK), 0)
    cidx = _iota((CHUNK, CHUNK), 1)
    causal = cidx <= r
    strict = cidx < r
    eye = jnp.where(cidx == r, 1.0, 0.0)

    units = [(c, h) for c in range(tc // CHUNK) for h in range(GDN_HEADS)]
    qs, ks, egs, g_lasts, gcs, decays, lows, rhss = [], [], [], [], [], [], [], []
    for c, h in units:
        rows = slice(c * CHUNK, (c + 1) * CHUNK)
        q = _l2(qkv[rows, h * GDN_DK:(h + 1) * GDN_DK]) * (GDN_DK ** -0.5)
        k = _l2(qkv[rows, GDN_K + h * GDN_DK:GDN_K + (h + 1) * GDN_DK])
        v = qkv[rows, 2 * GDN_K + h * GDN_DV:2 * GDN_K + (h + 1) * GDN_DV]
        beta = beta_all[rows, SMALL_BETA + h:SMALL_BETA + h + 1]
        gc = big_g[rows, SMALL_A + h:SMALL_A + h + 1]
        g_row = g_rows[SMALL_A + h:SMALL_A + h + 1, c * CHUNK:(c + 1) * CHUNK]
        decay = jnp.where(causal, jnp.exp(jnp.where(causal, gc - g_row, 0.0)), 0.0)
        eg = jnp.exp(gc)
        qs.append(q)
        ks.append(k)
        egs.append(eg)
        gcs.append(gc)
        g_lasts.append(gc[CHUNK - 1:CHUNK])
        decays.append(decay)
        lows.append(jnp.where(strict, beta * _dot_nt(k, k) * decay, 0.0))
        rhss.append(jnp.concatenate([v * beta, k * (beta * eg)], axis=-1))
    invs = [eye - low for low in lows]
    pws = lows
    for _ in range(int(math.log2(CHUNK)) - 1):
        pws = [_dot(pw, pw) for pw in pws]
        invs = [inv + _dot(inv, pw) for inv, pw in zip(invs, pws)]
    sols = [_dot(inv, rhs) for inv, rhs in zip(invs, rhss)]
    atts = [_dot_nt(q, k) * decay for q, k, decay in zip(qs, ks, decays)]
    k_decs = [k * jnp.exp(gl - gc) for k, gl, gc in zip(ks, g_lasts, gcs)]
    kws = [_dot_tn(k_dec, sol) for k_dec, sol in zip(k_decs, sols)]
    aws = [_dot(att, sol) for att, sol in zip(atts, sols)]

    for i, (c, h) in enumerate(units):
        rows = slice(c * CHUNK, (c + 1) * CHUNK)
        st = state_ref[h]
        lhs = jnp.concatenate([kws[i][:, GDN_DV:], qs[i] * egs[i] - aws[i][:, GDN_DV:]], axis=0)
        prod = _dot(lhs, st)
        o = prod[GDN_DK:] + aws[i][:, :GDN_DV]
        state_ref[h] = jnp.exp(g_lasts[i]) * st - prod[:GDN_DK] + kws[i][:, :GDN_DV]
        vs = slice(h * GDN_DV, (h + 1) * GDN_DV)
        o_ref[rows, vs] = _rms(o, gn_ref[...]) * _silu(g_ref[rows, vs])


def _gdn(cqkv, cg, small, conv_w, a_log, dt_bias, gn, batch, seq, tc=256, halo=8):
    n = cqkv.shape[0]
    nt = seq // tc
    blk = lambda b, j: (b * nt + j, 0)
    const = lambda b, j: (0, 0)
    prev = lambda b, j: (jnp.maximum((b * nt + j) * (tc // halo) - 1, 0), 0)
    al_row = jnp.zeros((1, LANES), F32).at[0, SMALL_A:SMALL_A + GDN_HEADS].set(a_log)
    dt_row = jnp.zeros((1, LANES), F32).at[0, SMALL_A:SMALL_A + GDN_HEADS].set(dt_bias)
    return pl.pallas_call(
        _gdn_kernel,
        grid=(batch, nt),
        in_specs=[pl.BlockSpec((tc, GDN_QKV), blk),
                  pl.BlockSpec((halo, GDN_QKV), prev),
                  pl.BlockSpec((tc, GDN_V), blk),
                  pl.BlockSpec((tc, LANES), blk),
                  pl.BlockSpec((GDN_CONV, GDN_QKV), const),
                  pl.BlockSpec((1, LANES), const),
                  pl.BlockSpec((1, LANES), const),
                  pl.BlockSpec((1, GDN_DV), const)],
        out_specs=pl.BlockSpec((tc, GDN_V), blk),
        out_shape=jax.ShapeDtypeStruct((n, GDN_V), F32),
        scratch_shapes=[pltpu.VMEM((GDN_HEADS, GDN_DK, GDN_DV), F32)],
        compiler_params=_params(2),
        name="gdn",
    )(cqkv, cqkv, cg, small, conv_w, al_row, dt_row, gn[None])


def _first_argmax(x, lane):
    m = jnp.max(x, axis=-1, keepdims=True)
    idx = jnp.min(jnp.where(x == m, lane.astype(F32), float(LANES)), axis=-1, keepdims=True)
    return m, idx.astype(I32)


def _merge_kernel(ya_ref, yb_ref, yc_ref, gt_ref, x_ref, wa_ref, wb_ref, wc_ref, wo_ref, fn_ref,
                  wr_ref, br_ref, x1_ref, h2_ref, ri_ref, rw_ref, cnt_ref, carry_ref):
    @pl.when(pl.program_id(0) == 0)
    def _():
        carry_ref[...] = jnp.zeros_like(carry_ref)

    d = D_MODEL
    ya = ya_ref[...].astype(BF16)
    yb = yb_ref[...].astype(BF16)
    yc = yc_ref[...].astype(BF16)
    cw = 2 * LANES
    merged = []
    for c in range(d // cw):
        lo = c * cw
        merged.append((
            _sigmoid(gt_ref[:, lo:lo + cw]) * jnp.dot(ya, wa_ref[:, lo:lo + cw], preferred_element_type=F32)
            + _sigmoid(gt_ref[:, d + lo:d + lo + cw]) * jnp.dot(yb, wb_ref[:, lo:lo + cw], preferred_element_type=F32)
            + _sigmoid(gt_ref[:, 2 * d + lo:2 * d + lo + cw]) * jnp.dot(yc, wc_ref[:, lo:lo + cw], preferred_element_type=F32)
        ).astype(BF16))
    merged = jnp.concatenate(merged, axis=-1)
    x1 = x_ref[...] + jnp.dot(merged, wo_ref[...], preferred_element_type=F32)
    x1_ref[...] = x1
    h2 = _rms(x1, fn_ref[...])
    _store_token_tiles(h2_ref, h2)

    logits = _dot_hi(h2, wr_ref[...]) + br_ref[...]
    tm = logits.shape[0]
    lane = _iota((tm, LANES), 1)
    neg = -jnp.inf
    is_c = lane < N_GROUPS
    cm, g_idx = _first_argmax(jnp.where(is_c, logits, neg), lane)
    g_prob = 1.0 / jnp.sum(jnp.where(is_c, jnp.exp(logits - cm), 0.0), axis=-1, keepdims=True)
    sel = (lane >= N_GROUPS) & (((lane - N_GROUPS) // EXPERTS_PER_GROUP) == g_idx)
    fm = jnp.max(jnp.where(sel, logits, neg), axis=-1, keepdims=True)
    ef = jnp.where(sel, jnp.exp(logits - fm), 0.0)
    p1, i1 = _first_argmax(jnp.where(sel, ef, neg), lane)
    p2, i2 = _first_argmax(jnp.where(sel & (lane != i1), ef, neg), lane)
    w1 = g_prob * p1 / (p1 + p2)
    w2 = g_prob * p2 / (p1 + p2)
    e1 = i1 - N_GROUPS
    e2 = i2 - N_GROUPS

    oh = jnp.where((lane == e1) | (lane == e2 + N_EXPERTS), 1.0, 0.0)
    stril = jnp.where(_iota((tm, tm), 1) < _iota((tm, tm), 0), 1.0, 0.0).astype(BF16)
    before = jnp.dot(stril, oh.astype(BF16), preferred_element_type=F32)
    tot = jnp.sum(oh, axis=0, keepdims=True)
    tot_first = jnp.where(lane[:1] < N_EXPERTS, tot, 0.0)
    carry = carry_ref[...]
    base = carry + pltpu.roll(carry + tot_first, N_EXPERTS, 1)
    ranks = oh * (before + base)
    rank1 = jnp.sum(jnp.where(lane < N_EXPERTS, ranks, 0.0), axis=-1, keepdims=True)
    rank2 = jnp.sum(jnp.where(lane < N_EXPERTS, 0.0, ranks), axis=-1, keepdims=True)
    new_carry = carry + tot_first + pltpu.roll(tot - tot_first, LANES - N_EXPERTS, 1)
    carry_ref[...] = new_carry
    cnt_ref[...] = jnp.broadcast_to(new_carry, cnt_ref.shape).astype(I32)

    ri_ref[...] = jnp.where(lane == 0, e1, jnp.where(lane == 1, e2, jnp.where(
        lane == 2, rank1.astype(I32), jnp.where(lane == 3, rank2.astype(I32), 0))))
    rw_ref[...] = jnp.where(lane == 0, w1, jnp.where(lane == 1, w2, 0.0))


def _merge(ya, yb, yc, gates, x, wa, wb, wc, wo, fn, wr, br, tm=256):
    n = x.shape[0]
    d = D_MODEL
    blk = lambda i: (i, 0)
    const = lambda i: (0, 0)
    return pl.pallas_call(
        _merge_kernel,
        grid=(n // tm,),
        in_specs=[pl.BlockSpec((tm, SWA_Q), blk), pl.BlockSpec((tm, GLA_V), blk),
                  pl.BlockSpec((tm, GDN_V), blk), pl.BlockSpec((tm, 3 * d), blk),
                  pl.BlockSpec((tm, d), blk),
                  pl.BlockSpec((SWA_Q, d), const), pl.BlockSpec((GLA_V, d), const),
                  pl.BlockSpec((GDN_V, d), const), pl.BlockSpec((d, d), const),
                  pl.BlockSpec((1, d), const), pl.BlockSpec((d, LANES), const),
                  pl.BlockSpec((1, LANES), const)],
        out_specs=[pl.BlockSpec((tm, d), blk), pl.BlockSpec((tm * SUBLANES, LANES), blk),
                   pl.BlockSpec((tm, LANES), blk), pl.BlockSpec((tm, LANES), blk),
                   pl.BlockSpec((8, LANES), const)],
        out_shape=[jax.ShapeDtypeStruct((n, d), F32), jax.ShapeDtypeStruct((n * SUBLANES, LANES), F32),
                   jax.ShapeDtypeStruct((n, LANES), I32), jax.ShapeDtypeStruct((n, LANES), F32),
                   jax.ShapeDtypeStruct((8, LANES), I32)],
        scratch_shapes=[pltpu.VMEM((1, LANES), F32)],
        compiler_params=_params(1),
        name="merge_route",
    )(ya, yb, yc, gates, x, wa, wb, wc, wo, fn, wr, br)


def _dispatch_kernel(fill_ref, slot_ref, h_ref, xs_ref, zbuf, sem, zsem):
    td = h_ref.shape[0] // SUBLANES
    n_blocks = xs_ref.shape[0] // (MOE_ROWS * SUBLANES)

    def zero_fill(start):
        def run(cp):
            if start:
                cp.start()
            else:
                cp.wait()

        def pad_rows(e, carry):
            lo = fill_ref[e]
            n_pad = fill_ref[N_EXPERTS + e] - lo
            bit = MOE_ROWS // 2
            while bit >= 1:
                off = lo + (n_pad // (2 * bit)) * (2 * bit)

                @pl.when((n_pad & bit) != 0)
                def _(off=off, bit=bit):
                    run(pltpu.make_async_copy(
                        zbuf.at[pl.ds(0, bit * SUBLANES)],
                        xs_ref.at[pl.ds(pl.multiple_of(off * SUBLANES, SUBLANES), bit * SUBLANES)], zsem))

                bit //= 2
            return carry

        lax.fori_loop(0, N_EXPERTS, pad_rows, 0)

        def unused_block(b, carry):
            run(pltpu.make_async_copy(
                zbuf, xs_ref.at[pl.ds(pl.multiple_of(b * (MOE_ROWS * SUBLANES), SUBLANES), MOE_ROWS * SUBLANES)],
                zsem))
            return carry

        lax.fori_loop(fill_ref[2 * N_EXPERTS], n_blocks, unused_block, 0)

    @pl.when(pl.program_id(0) == 0)
    def _():
        zbuf[...] = jnp.zeros_like(zbuf)
        zero_fill(True)
        zero_fill(False)

    def rows(start):
        def body(t, carry):
            src = h_ref.at[pl.ds(pl.multiple_of(t * SUBLANES, SUBLANES), SUBLANES)]
            for k in range(TOP_K):
                s = slot_ref[0, 0, TOP_K * t + k]
                cp = pltpu.make_async_copy(
                    src, xs_ref.at[pl.ds(pl.multiple_of(s * SUBLANES, SUBLANES), SUBLANES)], sem)
                if start:
                    cp.start()
                else:
                    cp.wait()
            return carry

        lax.fori_loop(0, td, body, 0, unroll=4 if start else 8)

    rows(True)
    rows(False)


def _dispatch(h2, slots, fill, n_slots, td=1024):
    n = h2.shape[0] // SUBLANES
    td = min(td, n)
    grid_spec = pltpu.PrefetchScalarGridSpec(
        num_scalar_prefetch=1,
        grid=(n // td,),
        in_specs=[pl.BlockSpec((1, 1, TOP_K * td), lambda i, fill: (i, 0, 0), memory_space=pltpu.SMEM),
                  pl.BlockSpec((td * SUBLANES, LANES), lambda i, fill: (i, 0))],
        out_specs=pl.BlockSpec(memory_space=pl.ANY),
        scratch_shapes=[pltpu.VMEM((MOE_ROWS * SUBLANES, LANES), F32), pltpu.SemaphoreType.DMA(()),
                        pltpu.SemaphoreType.DMA(())],
    )
    return pl.pallas_call(
        _dispatch_kernel,
        grid_spec=grid_spec,
        out_shape=jax.ShapeDtypeStruct((n_slots * SUBLANES, LANES), F32),
        compiler_params=_params(1),
        name="dispatch",
    )(fill, slots.reshape(n // td, 1, TOP_K * td), h2)


def _expert_kernel(be_ref, nxt_ref, par_ref, xs_ref, wg_hbm, wu_hbm, wd_hbm, ys_ref,
                   wg_f, wu_f, wd_f, wg_s, wu_s, wd_s, sems, *, layer):
    i = pl.program_id(0)
    e = be_ref[i]
    slot = par_ref[i]
    first_of_expert = jnp.logical_or(i == 0, e != be_ref[jnp.maximum(i - 1, 0)])

    def weight_copies(expert, buf):
        return (pltpu.make_async_copy(wg_hbm.at[layer, expert], wg_f.at[buf], sems.at[buf, 0]),
                pltpu.make_async_copy(wu_hbm.at[layer, expert], wu_f.at[buf], sems.at[buf, 1]),
                pltpu.make_async_copy(wd_hbm.at[layer, expert], wd_f.at[buf], sems.at[buf, 2]))

    @pl.when(i == 0)
    def _():
        for cp in weight_copies(e, slot):
            cp.start()

    @pl.when(first_of_expert)
    def _():
        @pl.when(nxt_ref[i] != e)
        def _():
            for cp in weight_copies(nxt_ref[i], 1 - slot):
                cp.start()

        for cp in weight_copies(e, slot):
            cp.wait()
        wg_s[...] = wg_f[slot].astype(BF16)
        wu_s[...] = wu_f[slot].astype(BF16)
        wd_s[...] = wd_f[slot].astype(BF16)

    half = xs_ref.shape[0] // 2
    th = half // SUBLANES
    xs = [_load_token_tiles(xs_ref.at[pl.ds(p * half, half)], th).astype(BF16) for p in range(2)]
    gs = [jnp.dot(x, wg_s[...], preferred_element_type=F32) for x in xs]
    us = [jnp.dot(x, wu_s[...], preferred_element_type=F32) for x in xs]
    acts = [(_silu(g) * u).astype(BF16) for g, u in zip(gs, us)]
    for p in range(2):
        _store_token_tiles(ys_ref.at[pl.ds(p * half, half)],
                           jnp.dot(acts[p], wd_s[...], preferred_element_type=F32))


def _experts(xs, block_expert, next_expert, block_parity, wg, wu, wd, layer):
    n_slots = xs.shape[0] // SUBLANES
    tb = MOE_ROWS
    blk = lambda i, be, nx, par: (i, 0)
    grid_spec = pltpu.PrefetchScalarGridSpec(
        num_scalar_prefetch=3,
        grid=(n_slots // tb,),
        in_specs=[pl.BlockSpec((tb * SUBLANES, LANES), blk),
                  pl.BlockSpec(memory_space=pl.ANY), pl.BlockSpec(memory_space=pl.ANY),
                  pl.BlockSpec(memory_space=pl.ANY)],
        out_specs=pl.BlockSpec((tb * SUBLANES, LANES), blk),
        scratch_shapes=[pltpu.VMEM((2, D_MODEL, EXPERT_FF), F32), pltpu.VMEM((2, D_MODEL, EXPERT_FF), F32),
                        pltpu.VMEM((2, EXPERT_FF, D_MODEL), F32),
                        pltpu.VMEM((D_MODEL, EXPERT_FF), BF16), pltpu.VMEM((D_MODEL, EXPERT_FF), BF16),
                        pltpu.VMEM((EXPERT_FF, D_MODEL), BF16),
                        pltpu.SemaphoreType.DMA((2, 3))],
    )
    return pl.pallas_call(
        functools.partial(_expert_kernel, layer=layer),
        grid_spec=grid_spec,
        out_shape=jax.ShapeDtypeStruct((n_slots * SUBLANES, LANES), F32),
        compiler_params=_params(1),
        name="experts",
    )(block_expert, next_expert, block_parity, xs, wg, wu, wd)


def _combine_kernel(slot_ref, next_slot_ref, x1_ref, rw_ref, p_ref, ys_ref, pn_ref, wpg_ref, wpl_ref, o_ref,
                    ybuf, sems):
    i = pl.program_id(0)
    tp = x1_ref.shape[0]
    cur = i % 2

    def rows(sref, buf, start):
        def body(t, carry):
            for k in range(TOP_K):
                s = sref[0, 0, TOP_K * t + k]
                cp = pltpu.make_async_copy(
                    ys_ref.at[pl.ds(pl.multiple_of(s * SUBLANES, SUBLANES), SUBLANES)],
                    ybuf.at[buf, k, pl.ds(pl.multiple_of(t * SUBLANES, SUBLANES), SUBLANES)], sems.at[buf])
                if start:
                    cp.start()
                else:
                    cp.wait()
            return carry

        lax.fori_loop(0, tp, body, 0, unroll=4 if start else 8)

    @pl.when(i == 0)
    def _():
        rows(slot_ref, 0, True)

    @pl.when(i + 1 < pl.num_programs(0))
    def _():
        rows(next_slot_ref, 1 - cur, True)

    rows(slot_ref, cur, False)

    rw = rw_ref[...]
    x2 = (x1_ref[...] + rw[:, 0:1] * _load_token_tiles(ybuf.at[cur, 0], tp)
          + rw[:, 1:2] * _load_token_tiles(ybuf.at[cur, 1], tp))
    h3 = _rms(x2, pn_ref[...]).astype(BF16)
    pe = p_ref[0].astype(BF16)
    cw = 2 * LANES
    for c in range(D_MODEL // cw):
        cols = slice(c * cw, (c + 1) * cw)
        gate = _sigmoid(jnp.dot(h3, wpg_ref[:, cols], preferred_element_type=F32))
        o_ref[:, cols] = x2[:, cols] + gate * jnp.dot(pe, wpl_ref[:, cols], preferred_element_type=F32)


def _combine_ple(x1, rw, slots, ys, p, layer, pn, wpg, wpl, tp=256):
    n = x1.shape[0]
    d = D_MODEL
    blk = lambda i: (i, 0)
    const = lambda i: (0, 0)
    nsteps = n // tp
    slot_blocks = slots.reshape(nsteps, 1, TOP_K * tp)
    return pl.pallas_call(
        _combine_kernel,
        grid=(nsteps,),
        in_specs=[pl.BlockSpec((1, 1, TOP_K * tp), lambda i: (i, 0, 0), memory_space=pltpu.SMEM),
                  pl.BlockSpec((1, 1, TOP_K * tp), lambda i: (jnp.minimum(i + 1, nsteps - 1), 0, 0),
                               memory_space=pltpu.SMEM),
                  pl.BlockSpec((tp, d), blk), pl.BlockSpec((tp, LANES), blk),
                  pl.BlockSpec((1, tp, PLE_DIM), lambda i: (layer, i, 0)),
                  pl.BlockSpec(memory_space=pl.ANY),
                  pl.BlockSpec((1, d), const), pl.BlockSpec((d, d), const),
                  pl.BlockSpec((PLE_DIM, d), const)],
        out_specs=pl.BlockSpec((tp, d), blk),
        out_shape=jax.ShapeDtypeStruct((n, d), F32),
        scratch_shapes=[pltpu.VMEM((2, TOP_K, tp * SUBLANES, LANES), F32), pltpu.SemaphoreType.DMA((2,))],
        compiler_params=_params(1),
        name="combine_ple",
    )(slot_blocks, slot_blocks, x1, rw, p, ys, pn, wpg, wpl)


def _pack_w_in(w_in):
    o = IN_OFFSETS
    w = w_in.astype(BF16)
    pad = jnp.zeros((D_MODEL, LANES - GLA_GATE_RANK - 2 * GDN_HEADS), BF16)
    cols = [w[:, :o[6]], w[:, o[7]:o[9]], w[:, o[11]:], w[:, o[6]:o[7]], w[:, o[9]:o[11]], pad]
    return jnp.concatenate(cols, axis=1)


def _layer(x, p, layer, cos_t, sin_t, batch, seq, attn_norm, w_in, q_norm, k_norm, sinks, gla_wa2, gla_ba,
           gla_norm, gdn_conv, gdn_a_log, gdn_dt_bias, gdn_norm, w_br_a, w_br_b, w_br_c, w_o,
           ffn_norm, w_coarse, b_coarse, w_fine, b_fine, w_gate_e, w_up_e, w_down_e,
           ple_norm, w_ple_gate, w_ple):
    n = x.shape[0]
    aq, akv, bqk, bv, br, cqkv, cg, gates, small = _inproj(x, attn_norm[None], _pack_w_in(w_in))
    ya = _swa(aq, akv, cos_t, sin_t, q_norm, k_norm, sinks, batch, seq)
    yb = _gla(bqk, bv, br, small, gla_wa2, gla_ba, gla_norm, batch, seq)
    yc = _gdn(cqkv, cg, small, gdn_conv, gdn_a_log, gdn_dt_bias, gdn_norm, batch, seq)

    wr = jnp.zeros((D_MODEL, LANES), F32).at[:, :N_GROUPS].set(w_coarse)
    wr = wr.at[:, N_GROUPS:N_GROUPS + N_EXPERTS].set(w_fine)
    brow = jnp.zeros((1, LANES), F32).at[0, :N_GROUPS].set(b_coarse)
    brow = brow.at[0, N_GROUPS:N_GROUPS + N_EXPERTS].set(b_fine)
    x1, h2, ri, rw, cnt = _merge(ya, yb, yc, gates, x, w_br_a.astype(BF16), w_br_b.astype(BF16),
                                 w_br_c.astype(BF16), w_o.astype(BF16), ffn_norm[None], wr, brow)

    counts = cnt[0, :N_EXPERTS]
    padded = (counts + MOE_ROWS - 1) // MOE_ROWS * MOE_ROWS
    pad_end = jnp.cumsum(padded)
    pad_start = pad_end - padded
    slots = (pad_start[ri[:, :TOP_K]] + ri[:, TOP_K:2 * TOP_K]).astype(I32)
    n_slots = n * TOP_K + N_EXPERTS * MOE_ROWS
    block_start = jnp.arange(n_slots // MOE_ROWS, dtype=I32) * MOE_ROWS
    last_owner = jnp.max(jnp.where(padded > 0, jnp.arange(N_EXPERTS, dtype=I32), 0))
    block_expert = jnp.minimum(jnp.sum((pad_end[None, :] <= block_start[:, None]).astype(I32), axis=1),
                               last_owner).astype(I32)

    experts = jnp.arange(N_EXPERTS, dtype=I32)
    owns = padded > 0
    later = lax.cummin(jnp.where(owns, experts, N_EXPERTS)[::-1])[::-1]
    next_owner = jnp.concatenate([later[1:], jnp.full((1,), N_EXPERTS, I32)])
    next_owner = jnp.where(next_owner < N_EXPERTS, next_owner, experts)
    parity = (jnp.cumsum(owns.astype(I32)) - 1) % 2
    next_expert = next_owner[block_expert].astype(I32)
    block_parity = jnp.maximum(parity[block_expert], 0).astype(I32)

    fill = jnp.concatenate([pad_start + counts, pad_end, pad_end[-1:] // MOE_ROWS]).astype(I32)
    xs = _dispatch(h2, slots, fill, n_slots)
    ys = _experts(xs, block_expert, next_expert, block_parity, w_gate_e, w_up_e, w_down_e, layer)
    return _combine_ple(x1, rw, slots, ys, p, layer, ple_norm[None], w_ple_gate.astype(BF16),
                        w_ple.astype(BF16))


def kernel(x, p, positions, attn_norm, w_in, q_norm, k_norm, sinks, gla_wa2, gla_ba, gla_norm, gdn_conv, gdn_a_log, gdn_dt_bias, gdn_norm, w_br_a, w_br_b, w_br_c, w_o, ffn_norm, w_coarse, b_coarse, w_fine, b_fine, w_gate_e, w_up_e, w_down_e, ple_norm, w_ple_gate, w_ple):
    batch, seq, d = x.shape
    n = batch * seq
    depth = p.shape[0]
    cos_t, sin_t = _rope_tables(positions)
    xf = x.reshape(n, d)
    pf = p.reshape(depth, n, p.shape[-1])
    per_layer = (attn_norm, w_in, q_norm, k_norm, sinks, gla_wa2, gla_ba, gla_norm, gdn_conv, gdn_a_log,
                 gdn_dt_bias, gdn_norm, w_br_a, w_br_b, w_br_c, w_o, ffn_norm, w_coarse, b_coarse,
                 w_fine, b_fine, w_gate_e, w_up_e, w_down_e, ple_norm, w_ple_gate, w_ple)
    stacked = (w_gate_e, w_up_e, w_down_e)
    for i in range(depth):
        xf = _layer(xf, pf, i, cos_t, sin_t, batch, seq,
                    *[a if any(a is s for s in stacked) else a[i] for a in per_layer])
    return xf.reshape(batch, seq, d)
```

```python
import functools
import math

import numpy as np
import jax
import jax.numpy as jnp
from jax import lax
from jax.experimental import pallas as pl
from jax.experimental.pallas import tpu as pltpu

F32 = jnp.float32
BF16 = jnp.bfloat16
I32 = jnp.int32

D_MODEL = 1024
PLE_DIM = 256
NORM_EPS = 1e-6
MASK_VALUE = -1e30

SWA_Q_HEADS = 8
SWA_KV_HEADS = 2
SWA_HEAD_DIM = 64
SWA_WINDOW = 128
ROT_DIM = SWA_HEAD_DIM // 4
ROPE_THETA = 500000.0

GLA_HEADS = 4
GLA_DK = 64
GLA_DV = 128
GLA_GATE_RANK = 16
GLA_GATE_NORM = 16.0
CHUNK = 64

GDN_HEADS = 4
GDN_DK = 128
GDN_DV = 128
GDN_CONV = 4

N_GROUPS = 4
EXPERTS_PER_GROUP = 8
N_EXPERTS = N_GROUPS * EXPERTS_PER_GROUP
EXPERT_FF = 512
TOP_K = 2

SWA_Q = SWA_Q_HEADS * SWA_HEAD_DIM
SWA_KV = SWA_KV_HEADS * SWA_HEAD_DIM
GLA_K = GLA_HEADS * GLA_DK
GLA_V = GLA_HEADS * GLA_DV
GDN_K = GDN_HEADS * GDN_DK
GDN_V = GDN_HEADS * GDN_DV
GDN_QKV = 2 * GDN_K + GDN_V
IN_SPLITS = (SWA_Q, SWA_KV, SWA_KV, GLA_K, GLA_K, GLA_V, GLA_GATE_RANK, GLA_V,
             GDN_QKV, GDN_HEADS, GDN_HEADS, GDN_V, 3 * D_MODEL)
IN_OFFSETS = tuple(int(o) for o in np.cumsum((0,) + IN_SPLITS))

LANES = 128
SMALL_LR = 0
SMALL_BETA = GLA_GATE_RANK
SMALL_A = GLA_GATE_RANK + GDN_HEADS

OUT_WIDTHS = (SWA_Q, 2 * SWA_KV, 2 * GLA_K, GLA_V, GLA_V, GDN_QKV, GDN_V, 3 * D_MODEL, LANES)

MOE_ROWS = 256
VMEM_LIMIT = 56 * 1024 * 1024


def _params(n_axes):
    return pltpu.CompilerParams(dimension_semantics=("arbitrary",) * n_axes,
                                vmem_limit_bytes=VMEM_LIMIT)


def _dot(a, b):
    return jnp.dot(a.astype(BF16), b.astype(BF16), preferred_element_type=F32)


def _dot_nt(a, b):
    return lax.dot_general(a.astype(BF16), b.astype(BF16), (((1,), (1,)), ((), ())),
                           preferred_element_type=F32)


def _dot_tn(a, b):
    return lax.dot_general(a.astype(BF16), b.astype(BF16), (((0,), (0,)), ((), ())),
                           preferred_element_type=F32)


def _split2(x):
    hi = x.astype(BF16)
    lo = (x - hi.astype(F32)).astype(BF16)
    return hi, lo


def _dot_exact_lhs(a, x):
    hi = x.astype(BF16)
    r = x - hi.astype(F32)
    mid = r.astype(BF16)
    lo = (r - mid.astype(F32)).astype(BF16)
    a = a.astype(BF16)
    return (jnp.dot(a, hi, preferred_element_type=F32) + jnp.dot(a, mid, preferred_element_type=F32)
            + jnp.dot(a, lo, preferred_element_type=F32))


def _dot_hi(a, b):
    ah, al = _split2(a)
    bh, bl = _split2(b)
    return (jnp.dot(ah, bh, preferred_element_type=F32) + jnp.dot(ah, bl, preferred_element_type=F32)
            + jnp.dot(al, bh, preferred_element_type=F32))


def _rms(x, g):
    return x * lax.rsqrt(jnp.mean(x * x, axis=-1, keepdims=True) + NORM_EPS) * g


def _sigmoid(x):
    return 0.5 * jnp.tanh(0.5 * x) + 0.5


def _silu(x):
    return x * _sigmoid(x)


def _softplus(x):
    return jnp.maximum(x, 0.0) + jnp.log(1.0 + jnp.exp(-jnp.abs(x)))


def _log_sigmoid(x):
    return -_softplus(-x)


def _iota(shape, axis):
    return lax.broadcasted_iota(I32, shape, axis)


SUBLANES = 8
ROW_TILES = D_MODEL // LANES


def _store_token_tiles(ref, x):
    t = x.shape[0]
    for s in range(ROW_TILES):
        ref[pl.ds(s, t, stride=ROW_TILES), :] = x[:, s * LANES:(s + 1) * LANES]


def _load_token_tiles(ref, t):
    return jnp.concatenate([ref[pl.ds(s, t, stride=ROW_TILES), :] for s in range(ROW_TILES)], axis=-1)


def _chunk_tril(n, strict=False):
    r = _iota((n, n), 0)
    c = _iota((n, n), 1)
    same = (r // CHUNK) == (c // CHUNK)
    return same & ((c < r) if strict else (c <= r))


def _inproj_kernel(x_ref, g_ref, w_ref, *out_refs):
    h = _rms(x_ref[...], g_ref[...]).astype(BF16)
    off = 0
    for o_ref in out_refs:
        wd = o_ref.shape[-1]
        o_ref[...] = jnp.dot(h, w_ref[:, off:off + wd], preferred_element_type=F32)
        off += wd


def _inproj(x, g, w, tm=256):
    n = x.shape[0]
    nc = w.shape[1]
    return pl.pallas_call(
        _inproj_kernel,
        grid=(n // tm,),
        in_specs=[pl.BlockSpec((tm, D_MODEL), lambda i: (i, 0)),
                  pl.BlockSpec((1, D_MODEL), lambda i: (0, 0)),
                  pl.BlockSpec((D_MODEL, nc), lambda i: (0, 0), pipeline_mode=pl.Buffered(1))],
        out_specs=[pl.BlockSpec((tm, wd), lambda i: (i, 0)) for wd in OUT_WIDTHS],
        out_shape=[jax.ShapeDtypeStruct((n, wd), F32) for wd in OUT_WIDTHS],
        compiler_params=_params(1),
        name="inproj",
    )(x, g, w)


def _rope_kernel(pos_ref, invf_ref, cos_ref, sin_ref):
    ang = pos_ref[...].astype(F32) * invf_ref[...]
    d = _iota(ang.shape, 1) % SWA_HEAD_DIM
    half = ROT_DIM // 2
    cos_ref[...] = jnp.where(d < ROT_DIM, jnp.cos(ang), 1.0)
    s = jnp.sin(ang)
    sin_ref[...] = jnp.where(d < half, -s, jnp.where(d < ROT_DIM, s, 0.0))


def _rope_tables(positions, tm=1024):
    n = positions.size
    tm = min(tm, n)
    inv_freq = 1.0 / (ROPE_THETA ** (jnp.arange(0, ROT_DIM, 2, dtype=F32) / ROT_DIM))
    lane_freq = jnp.tile(jnp.concatenate([inv_freq, inv_freq, jnp.zeros(SWA_HEAD_DIM - ROT_DIM, F32)]),
                         LANES // SWA_HEAD_DIM)[None]
    return pl.pallas_call(
        _rope_kernel,
        grid=(n // tm,),
        in_specs=[pl.BlockSpec((tm, 1), lambda i: (i, 0)),
                  pl.BlockSpec((1, LANES), lambda i: (0, 0))],
        out_specs=[pl.BlockSpec((tm, LANES), lambda i: (i, 0))] * 2,
        out_shape=[jax.ShapeDtypeStruct((n, LANES), F32)] * 2,
        compiler_params=_params(1),
        name="rope_tables",
    )(positions.reshape(n, 1), lane_freq)


def _dot_exact_rhs(x, b):
    hi, lo = _split2(x)
    return jnp.dot(hi, b, preferred_element_type=F32) + jnp.dot(lo, b, preferred_element_type=F32)


def _rope_matrices():
    m = _iota((LANES, LANES), 0)
    l = _iota((LANES, LANES), 1)
    d = l % SWA_HEAD_DIM
    half = ROT_DIM // 2
    same_head = (m // SWA_HEAD_DIM) == (l // SWA_HEAD_DIM)
    partner = ((d < half) & (m == l + half)) | ((d >= half) & (d < ROT_DIM) & (m == l - half))
    return jnp.where(same_head, 1.0, 0.0).astype(BF16), jnp.where(partner, 1.0, 0.0).astype(BF16)


def _norm_rope(x, gain, cos, sin, same_head, partner):
    ms = _dot_exact_rhs(x * x, same_head) * (1.0 / SWA_HEAD_DIM)
    xn = x * lax.rsqrt(ms + NORM_EPS) * gain
    return xn * cos + _dot_exact_rhs(xn, partner) * sin


def _swa_kernel(sink_ref, q_ref, kv_ref, cos_ref, sin_ref, qn_ref, kn_ref, o_ref, kprev_ref, vprev_ref):
    j = pl.program_id(1)
    w = SWA_WINDOW
    hd = SWA_HEAD_DIM
    group = SWA_Q_HEADS // SWA_KV_HEADS

    @pl.when(j == 0)
    def _():
        kprev_ref[...] = jnp.zeros_like(kprev_ref)
        vprev_ref[...] = jnp.zeros_like(vprev_ref)

    cos = cos_ref[...]
    sin = sin_ref[...]
    same_head, partner = _rope_matrices()
    kc = _norm_rope(kv_ref[:, :SWA_KV], kn_ref[...], cos, sin, same_head, partner).astype(BF16)
    vc = kv_ref[:, SWA_KV:].astype(BF16)
    k_all = jnp.concatenate([kprev_ref[...], kc], axis=0)
    v_all = jnp.concatenate([vprev_ref[...], vc], axis=0)
    kprev_ref[...] = kc
    vprev_ref[...] = vc

    rows = group * w
    qi = _iota((rows, 2 * w), 0) % w
    kj = _iota((rows, 2 * w), 1)
    first_key = jnp.where(j > 0, 0, w)
    mask = (kj > qi) & (kj <= qi + w) & (kj >= first_key)
    head_of_row = _iota((rows, 1), 0) // w
    qrs = [_norm_rope(q_ref[:, c * LANES:(c + 1) * LANES], qn_ref[...], cos, sin, same_head, partner)
           for c in range(SWA_Q // LANES)]
    per_block = LANES // hd
    qgs, sinks = [], []
    for g in range(SWA_KV_HEADS):
        heads = range(g * group, (g + 1) * group)
        qgs.append(jnp.concatenate(
            [qrs[h // per_block][:, (h % per_block) * hd:(h % per_block + 1) * hd] for h in heads], axis=0))
        sk = jnp.full((rows, 1), sink_ref[g * group], F32)
        for i in range(1, group):
            sk = jnp.where(head_of_row == i, sink_ref[g * group + i], sk)
        sinks.append(sk)
    ss = [jnp.where(mask, _dot_nt(qgs[g], k_all[:, g * hd:(g + 1) * hd]) * (hd ** -0.5), MASK_VALUE)
          for g in range(SWA_KV_HEADS)]
    ms = [jnp.maximum(jnp.max(s, axis=-1, keepdims=True), sk) for s, sk in zip(ss, sinks)]
    es = [jnp.exp(s - m).astype(BF16) for s, m in zip(ss, ms)]
    ones = jnp.ones((2 * w, hd), BF16)
    dens = [jnp.dot(e, ones, preferred_element_type=F32) + jnp.exp(sk - m) for e, sk, m in zip(es, sinks, ms)]
    for g in range(SWA_KV_HEADS):
        og = jnp.dot(es[g], v_all[:, g * hd:(g + 1) * hd], preferred_element_type=F32) / dens[g]
        for i in range(group):
            h = g * group + i
            o_ref[:, h * hd:(h + 1) * hd] = og[i * w:(i + 1) * w]


def _swa(aq, akv, cos_t, sin_t, q_norm, k_norm, sinks, batch, seq):
    n = aq.shape[0]
    w = SWA_WINDOW
    nq = seq // w
    cur = lambda b, j: (b * nq + j, 0)
    rep = LANES // SWA_HEAD_DIM
    return pl.pallas_call(
        _swa_kernel,
        grid=(batch, nq),
        in_specs=[pl.BlockSpec(memory_space=pltpu.SMEM),
                  pl.BlockSpec((w, SWA_Q), cur),
                  pl.BlockSpec((w, 2 * SWA_KV), cur),
                  pl.BlockSpec((w, LANES), cur),
                  pl.BlockSpec((w, LANES), cur),
                  pl.BlockSpec((1, LANES), lambda b, j: (0, 0)),
                  pl.BlockSpec((1, LANES), lambda b, j: (0, 0))],
        out_specs=pl.BlockSpec((w, SWA_Q), cur),
        out_shape=jax.ShapeDtypeStruct((n, SWA_Q), F32),
        scratch_shapes=[pltpu.VMEM((w, SWA_KV), BF16), pltpu.VMEM((w, SWA_KV), BF16)],
        compiler_params=_params(2),
        name="swa",
    )(sinks, aq, akv, cos_t, sin_t, jnp.tile(q_norm, rep)[None], jnp.tile(k_norm, rep)[None])


def _gla_kernel(qk_ref, v_ref, r_ref, sm_ref, wa_ref, ba_ref, gn_ref, o_ref, state_ref):
    @pl.when(pl.program_id(1) == 0)
    def _():
        state_ref[...] = jnp.zeros_like(state_ref)

    tc = qk_ref.shape[0]
    la = _log_sigmoid(_dot(sm_ref[...], wa_ref[...]) + ba_ref[...]) * (1.0 / GLA_GATE_NORM)
    tri = jnp.where(_chunk_tril(tc), 1.0, 0.0).astype(BF16)
    b = _dot_exact_lhs(tri, la)
    causal = _iota((CHUNK, CHUNK), 1) <= _iota((CHUNK, CHUNK), 0)
    n_chunks = tc // CHUNK
    qes, dcols, intra, upd = [], [], {}, {}
    for c in range(n_chunks):
        rows = slice(c * CHUNK, (c + 1) * CHUNK)
        bc = b[rows]
        b_last = bc[CHUNK - 1:CHUNK]
        k = qk_ref[rows, GLA_K:]
        qe = qk_ref[rows, :GLA_K] * (GLA_DK ** -0.5) * jnp.exp(bc)
        ke = k * jnp.exp(-bc)
        kd = k * jnp.exp(b_last - bc)
        qes.append(qe)
        dcols.append(jnp.transpose(jnp.broadcast_to(jnp.exp(b_last), (GLA_DV, GLA_K))))
        for h in range(GLA_HEADS):
            ks = slice(h * GLA_DK, (h + 1) * GLA_DK)
            vh = v_ref[rows, h * GLA_DV:(h + 1) * GLA_DV]
            att = jnp.where(causal, _dot_nt(qe[:, ks], ke[:, ks]), 0.0)
            intra[c, h] = _dot(att, vh)
            upd[c, h] = _dot_tn(kd[:, ks], vh)
    for h in range(GLA_HEADS):
        ks = slice(h * GLA_DK, (h + 1) * GLA_DK)
        vs = slice(h * GLA_DV, (h + 1) * GLA_DV)
        st = state_ref[h]
        for c in range(n_chunks):
            rows = slice(c * CHUNK, (c + 1) * CHUNK)
            o = intra[c, h] + _dot(qes[c][:, ks], st)
            st = dcols[c][ks] * st + upd[c, h]
            o_ref[rows, vs] = _rms(o, gn_ref[...]) * _silu(r_ref[rows, vs])
        state_ref[h] = st


def _gla(bqk, bv, br, small, wa2, ba, gn, batch, seq, tc=256):
    n = bqk.shape[0]
    nt = seq // tc
    blk = lambda b, j: (b * nt + j, 0)
    const = lambda b, j: (0, 0)
    wa_pad = jnp.zeros((LANES, GLA_K), F32).at[SMALL_LR:SMALL_LR + GLA_GATE_RANK].set(wa2)
    return pl.pallas_call(
        _gla_kernel,
        grid=(batch, nt),
        in_specs=[pl.BlockSpec((tc, 2 * GLA_K), blk),
                  pl.BlockSpec((tc, GLA_V), blk),
                  pl.BlockSpec((tc, GLA_V), blk),
                  pl.BlockSpec((tc, LANES), blk),
                  pl.BlockSpec((LANES, GLA_K), const),
                  pl.BlockSpec((1, GLA_K), const),
                  pl.BlockSpec((1, GLA_DV), const)],
        out_specs=pl.BlockSpec((tc, GLA_V), blk),
        out_shape=jax.ShapeDtypeStruct((n, GLA_V), F32),
        scratch_shapes=[pltpu.VMEM((GLA_HEADS, GLA_DK, GLA_DV), F32)],
        compiler_params=_params(2),
        name="gla",
    )(bqk, bv, br, small, wa_pad, ba[None], gn[None])


def _l2(x):
    return x * lax.rsqrt(jnp.sum(x * x, axis=-1, keepdims=True) + NORM_EPS)


def _gdn_kernel(x_ref, xp_ref, g_ref, sm_ref, cw_ref, al_ref, dt_ref, gn_ref, o_ref, state_ref):
    first = pl.program_id(1) == 0

    @pl.when(first)
    def _():
        state_ref[...] = jnp.zeros_like(state_ref)

    tc = x_ref.shape[0]
    halo = xp_ref.shape[0]
    x = x_ref[...]
    xprev = jnp.where(first, 0.0, xp_ref[...])
    sm = sm_ref[...]
    row = _iota((halo, 1), 0)
    conv = cw_ref[GDN_CONV - 1:GDN_CONV] * x
    for t in range(1, GDN_CONV):
        rolled = pltpu.roll(x, t, 0)
        head = jnp.where(row < t, pltpu.roll(xprev, t, 0), rolled[:halo])
        shifted = jnp.concatenate([head, rolled[halo:]], axis=0)
        conv = conv + cw_ref[GDN_CONV - 1 - t:GDN_CONV - t] * shifted
    qkv = _silu(conv)

    beta_all = _sigmoid(sm)
    g_all = -jnp.exp(al_ref[...]) * _softplus(sm + dt_ref[...])
    tri = jnp.where(_chunk_tril(tc), 1.0, 0.0).astype(BF16)
    big_g = _dot_exact_lhs(tri, g_all)
    g_rows = jnp.transpose(big_g)

    r = _iota((CHUNK, CHUNK), 0)
    cidx = _iota((CHUNK, CHUNK), 1)
    causal = cidx <= r
    strict = cidx < r
    eye = jnp.where(cidx == r, 1.0, 0.0)

    units = [(c, h) for c in range(tc // CHUNK) for h in range(GDN_HEADS)]
    qs, ks, egs, g_lasts, gcs, decays, lows, rhss = [], [], [], [], [], [], [], []
    for c, h in units:
        rows = slice(c * CHUNK, (c + 1) * CHUNK)
        q = _l2(qkv[rows, h * GDN_DK:(h + 1) * GDN_DK]) * (GDN_DK ** -0.5)
        k = _l2(qkv[rows, GDN_K + h * GDN_DK:GDN_K + (h + 1) * GDN_DK])
        v = qkv[rows, 2 * GDN_K + h * GDN_DV:2 * GDN_K + (h + 1) * GDN_DV]
        beta = beta_all[rows, SMALL_BETA + h:SMALL_BETA + h + 1]
        gc = big_g[rows, SMALL_A + h:SMALL_A + h + 1]
        g_row = g_rows[SMALL_A + h:SMALL_A + h + 1, c * CHUNK:(c + 1) * CHUNK]
        decay = jnp.where(causal, jnp.exp(jnp.where(causal, gc - g_row, 0.0)), 0.0)
        eg = jnp.exp(gc)
        qs.append(q)
        ks.append(k)
        egs.append(eg)
        gcs.append(gc)
        g_lasts.append(gc[CHUNK - 1:CHUNK])
        decays.append(decay)
        lows.append(jnp.where(strict, beta * _dot_nt(k, k) * decay, 0.0))
        rhss.append(jnp.concatenate([v * beta, k * (beta * eg)], axis=-1))
    invs = [eye - low for low in lows]
    pws = lows
    for _ in range(int(math.log2(CHUNK)) - 1):
        pws = [_dot(pw, pw) for pw in pws]
        invs = [inv + _dot(inv, pw) for inv, pw in zip(invs, pws)]
    sols = [_dot(inv, rhs) for inv, rhs in zip(invs, rhss)]
    atts = [_dot_nt(q, k) * decay for q, k, decay in zip(qs, ks, decays)]
    k_decs = [k * jnp.exp(gl - gc) for k, gl, gc in zip(ks, g_lasts, gcs)]
    kws = [_dot_tn(k_dec, sol) for k_dec, sol in zip(k_decs, sols)]
    aws = [_dot(att, sol) for att, sol in zip(atts, sols)]

    for i, (c, h) in enumerate(units):
        rows = slice(c * CHUNK, (c + 1) * CHUNK)
        st = state_ref[h]
        lhs = jnp.concatenate([kws[i][:, GDN_DV:], qs[i] * egs[i] - aws[i][:, GDN_DV:]], axis=0)
        prod = _dot(lhs, st)
        o = prod[GDN_DK:] + aws[i][:, :GDN_DV]
        state_ref[h] = jnp.exp(g_lasts[i]) * st - prod[:GDN_DK] + kws[i][:, :GDN_DV]
        vs = slice(h * GDN_DV, (h + 1) * GDN_DV)
        o_ref[rows, vs] = _rms(o, gn_ref[...]) * _silu(g_ref[rows, vs])


def _gdn(cqkv, cg, small, conv_w, a_log, dt_bias, gn, batch, seq, tc=256, halo=8):
    n = cqkv.shape[0]
    nt = seq // tc
    blk = lambda b, j: (b * nt + j, 0)
    const = lambda b, j: (0, 0)
    prev = lambda b, j: (jnp.maximum((b * nt + j) * (tc // halo) - 1, 0), 0)
    al_row = jnp.zeros((1, LANES), F32).at[0, SMALL_A:SMALL_A + GDN_HEADS].set(a_log)
    dt_row = jnp.zeros((1, LANES), F32).at[0, SMALL_A:SMALL_A + GDN_HEADS].set(dt_bias)
    return pl.pallas_call(
        _gdn_kernel,
        grid=(batch, nt),
        in_specs=[pl.BlockSpec((tc, GDN_QKV), blk),
                  pl.BlockSpec((halo, GDN_QKV), prev),
                  pl.BlockSpec((tc, GDN_V), blk),
                  pl.BlockSpec((tc, LANES), blk),
                  pl.BlockSpec((GDN_CONV, GDN_QKV), const),
                  pl.BlockSpec((1, LANES), const),
                  pl.BlockSpec((1, LANES), const),
                  pl.BlockSpec((1, GDN_DV), const)],
        out_specs=pl.BlockSpec((tc, GDN_V), blk),
        out_shape=jax.ShapeDtypeStruct((n, GDN_V), F32),
        scratch_shapes=[pltpu.VMEM((GDN_HEADS, GDN_DK, GDN_DV), F32)],
        compiler_params=_params(2),
        name="gdn",
    )(cqkv, cqkv, cg, small, conv_w, al_row, dt_row, gn[None])


def _first_argmax(x, lane):
    m = jnp.max(x, axis=-1, keepdims=True)
    idx = jnp.min(jnp.where(x == m, lane.astype(F32), float(LANES)), axis=-1, keepdims=True)
    return m, idx.astype(I32)


def _merge_kernel(ya_ref, yb_ref, yc_ref, gt_ref, x_ref, wa_ref, wb_ref, wc_ref, wo_ref, fn_ref,
                  wr_ref, br_ref, x1_ref, h2_ref, ri_ref, rw_ref, cnt_ref, carry_ref):
    @pl.when(pl.program_id(0) == 0)
    def _():
        carry_ref[...] = jnp.zeros_like(carry_ref)

    d = D_MODEL
    ya = ya_ref[...].astype(BF16)
    yb = yb_ref[...].astype(BF16)
    yc = yc_ref[...].astype(BF16)
    cw = 2 * LANES
    merged = []
    for c in range(d // cw):
        lo = c * cw
        merged.append((
            _sigmoid(gt_ref[:, lo:lo + cw]) * jnp.dot(ya, wa_ref[:, lo:lo + cw], preferred_element_type=F32)
            + _sigmoid(gt_ref[:, d + lo:d + lo + cw]) * jnp.dot(yb, wb_ref[:, lo:lo + cw], preferred_element_type=F32)
            + _sigmoid(gt_ref[:, 2 * d + lo:2 * d + lo + cw]) * jnp.dot(yc, wc_ref[:, lo:lo + cw], preferred_element_type=F32)
        ).astype(BF16))
    merged = jnp.concatenate(merged, axis=-1)
    x1 = x_ref[...] + jnp.dot(merged, wo_ref[...], preferred_element_type=F32)
    x1_ref[...] = x1
    h2 = _rms(x1, fn_ref[...])
    _store_token_tiles(h2_ref, h2)

    logits = _dot_hi(h2, wr_ref[...]) + br_ref[...]
    tm = logits.shape[0]
    lane = _iota((tm, LANES), 1)
    neg = -jnp.inf
    is_c = lane < N_GROUPS
    cm, g_idx = _first_argmax(jnp.where(is_c, logits, neg), lane)
    g_prob = 1.0 / jnp.sum(jnp.where(is_c, jnp.exp(logits - cm), 0.0), axis=-1, keepdims=True)
    sel = (lane >= N_GROUPS) & (((lane - N_GROUPS) // EXPERTS_PER_GROUP) == g_idx)
    fm = jnp.max(jnp.where(sel, logits, neg), axis=-1, keepdims=True)
    ef = jnp.where(sel, jnp.exp(logits - fm), 0.0)
    p1, i1 = _first_argmax(jnp.where(sel, ef, neg), lane)
    p2, i2 = _first_argmax(jnp.where(sel & (lane != i1), ef, neg), lane)
    w1 = g_prob * p1 / (p1 + p2)
    w2 = g_prob * p2 / (p1 + p2)
    e1 = i1 - N_GROUPS
    e2 = i2 - N_GROUPS

    oh = jnp.where((lane == e1) | (lane == e2 + N_EXPERTS), 1.0, 0.0)
    stril = jnp.where(_iota((tm, tm), 1) < _iota((tm, tm), 0), 1.0, 0.0).astype(BF16)
    before = jnp.dot(stril, oh.astype(BF16), preferred_element_type=F32)
    tot = jnp.sum(oh, axis=0, keepdims=True)
    tot_first = jnp.where(lane[:1] < N_EXPERTS, tot, 0.0)
    carry = carry_ref[...]
    base = carry + pltpu.roll(carry + tot_first, N_EXPERTS, 1)
    ranks = oh * (before + base)
    rank1 = jnp.sum(jnp.where(lane < N_EXPERTS, ranks, 0.0), axis=-1, keepdims=True)
    rank2 = jnp.sum(jnp.where(lane < N_EXPERTS, 0.0, ranks), axis=-1, keepdims=True)
    new_carry = carry + tot_first + pltpu.roll(tot - tot_first, LANES - N_EXPERTS, 1)
    carry_ref[...] = new_carry
    cnt_ref[...] = jnp.broadcast_to(new_carry, cnt_ref.shape).astype(I32)

    ri_ref[...] = jnp.where(lane == 0, e1, jnp.where(lane == 1, e2, jnp.where(
        lane == 2, rank1.astype(I32), jnp.where(lane == 3, rank2.astype(I32), 0))))
    rw_ref[...] = jnp.where(lane == 0, w1, jnp.where(lane == 1, w2, 0.0))


def _merge(ya, yb, yc, gates, x, wa, wb, wc, wo, fn, wr, br, tm=256):
    n = x.shape[0]
    d = D_MODEL
    blk = lambda i: (i, 0)
    const = lambda i: (0, 0)
    return pl.pallas_call(
        _merge_kernel,
        grid=(n // tm,),
        in_specs=[pl.BlockSpec((tm, SWA_Q), blk), pl.BlockSpec((tm, GLA_V), blk),
                  pl.BlockSpec((tm, GDN_V), blk), pl.BlockSpec((tm, 3 * d), blk),
                  pl.BlockSpec((tm, d), blk),
                  pl.BlockSpec((SWA_Q, d), const), pl.BlockSpec((GLA_V, d), const),
                  pl.BlockSpec((GDN_V, d), const), pl.BlockSpec((d, d), const),
                  pl.BlockSpec((1, d), const), pl.BlockSpec((d, LANES), const),
                  pl.BlockSpec((1, LANES), const)],
        out_specs=[pl.BlockSpec((tm, d), blk), pl.BlockSpec((tm * SUBLANES, LANES), blk),
                   pl.BlockSpec((tm, LANES), blk), pl.BlockSpec((tm, LANES), blk),
                   pl.BlockSpec((8, LANES), const)],
        out_shape=[jax.ShapeDtypeStruct((n, d), F32), jax.ShapeDtypeStruct((n * SUBLANES, LANES), F32),
                   jax.ShapeDtypeStruct((n, LANES), I32), jax.ShapeDtypeStruct((n, LANES), F32),
                   jax.ShapeDtypeStruct((8, LANES), I32)],
        scratch_shapes=[pltpu.VMEM((1, LANES), F32)],
        compiler_params=_params(1),
        name="merge_route",
    )(ya, yb, yc, gates, x, wa, wb, wc, wo, fn, wr, br)


def _expert_kernel(be_ref, nxt_ref, par_ref, tok_ref, next_tok_ref, h_hbm, wg_hbm, wu_hbm, wd_hbm, ys_ref,
                   xbuf, wg_f, wu_f, wd_f, wg_s, wu_s, wd_s, xsems, sems, *, layer):
    i = pl.program_id(0)
    e = be_ref[i]
    slot = par_ref[i]
    first_of_expert = jnp.logical_or(i == 0, e != be_ref[jnp.maximum(i - 1, 0)])
    tb = ys_ref.shape[0] // SUBLANES
    cur = i % 2

    def rows(tref, buf, start):
        def body(r, carry):
            t = tref[0, 0, r]
            cp = pltpu.make_async_copy(
                h_hbm.at[pl.ds(pl.multiple_of(t * SUBLANES, SUBLANES), SUBLANES)],
                xbuf.at[buf, pl.ds(pl.multiple_of(r * SUBLANES, SUBLANES), SUBLANES)], xsems.at[buf])
            if start:
                cp.start()
            else:
                cp.wait()
            return carry

        lax.fori_loop(0, tb, body, 0, unroll=8)

    @pl.when(i == 0)
    def _():
        rows(tok_ref, 0, True)

    @pl.when(i + 1 < pl.num_programs(0))
    def _():
        rows(next_tok_ref, 1 - cur, True)

    def weight_copies(expert, buf):
        return (pltpu.make_async_copy(wg_hbm.at[layer, expert], wg_f.at[buf], sems.at[buf, 0]),
                pltpu.make_async_copy(wu_hbm.at[layer, expert], wu_f.at[buf], sems.at[buf, 1]),
                pltpu.make_async_copy(wd_hbm.at[layer, expert], wd_f.at[buf], sems.at[buf, 2]))

    @pl.when(i == 0)
    def _():
        for cp in weight_copies(e, slot):
            cp.start()

    @pl.when(first_of_expert)
    def _():
        @pl.when(nxt_ref[i] != e)
        def _():
            for cp in weight_copies(nxt_ref[i], 1 - slot):
                cp.start()

        for cp in weight_copies(e, slot):
            cp.wait()
        wg_s[...] = wg_f[slot].astype(BF16)
        wu_s[...] = wu_f[slot].astype(BF16)
        wd_s[...] = wd_f[slot].astype(BF16)

    rows(tok_ref, cur, False)

    half = ys_ref.shape[0] // 2
    th = half // SUBLANES
    xs = [_load_token_tiles(xbuf.at[cur, pl.ds(p * half, half)], th).astype(BF16) for p in range(2)]
    gs = [jnp.dot(x, wg_s[...], preferred_element_type=F32) for x in xs]
    us = [jnp.dot(x, wu_s[...], preferred_element_type=F32) for x in xs]
    acts = [(_silu(g) * u).astype(BF16) for g, u in zip(gs, us)]
    for p in range(2):
        _store_token_tiles(ys_ref.at[pl.ds(p * half, half)],
                           jnp.dot(acts[p], wd_s[...], preferred_element_type=F32))


def _experts(h2, slot_token, block_expert, next_expert, block_parity, wg, wu, wd, layer):
    n_slots = slot_token.shape[0]
    tb = MOE_ROWS
    nb = n_slots // tb
    tok_blocks = slot_token.reshape(nb, 1, tb)
    grid_spec = pltpu.PrefetchScalarGridSpec(
        num_scalar_prefetch=3,
        grid=(nb,),
        in_specs=[pl.BlockSpec((1, 1, tb), lambda i, be, nx, par: (i, 0, 0), memory_space=pltpu.SMEM),
                  pl.BlockSpec((1, 1, tb), lambda i, be, nx, par: (jnp.minimum(i + 1, nb - 1), 0, 0),
                               memory_space=pltpu.SMEM),
                  pl.BlockSpec(memory_space=pl.ANY),
                  pl.BlockSpec(memory_space=pl.ANY), pl.BlockSpec(memory_space=pl.ANY),
                  pl.BlockSpec(memory_space=pl.ANY)],
        out_specs=pl.BlockSpec((tb * SUBLANES, LANES), lambda i, be, nx, par: (i, 0)),
        scratch_shapes=[pltpu.VMEM((2, tb * SUBLANES, LANES), F32),
                        pltpu.VMEM((2, D_MODEL, EXPERT_FF), F32), pltpu.VMEM((2, D_MODEL, EXPERT_FF), F32),
                        pltpu.VMEM((2, EXPERT_FF, D_MODEL), F32),
                        pltpu.VMEM((D_MODEL, EXPERT_FF), BF16), pltpu.VMEM((D_MODEL, EXPERT_FF), BF16),
                        pltpu.VMEM((EXPERT_FF, D_MODEL), BF16),
                        pltpu.SemaphoreType.DMA((2,)), pltpu.SemaphoreType.DMA((2, 3))],
    )
    return pl.pallas_call(
        functools.partial(_expert_kernel, layer=layer),
        grid_spec=grid_spec,
        out_shape=jax.ShapeDtypeStruct((n_slots * SUBLANES, LANES), F32),
        compiler_params=_params(1),
        name="experts",
    )(block_expert, next_expert, block_parity, tok_blocks, tok_blocks, h2, wg, wu, wd)


def _combine_kernel(slot_ref, next_slot_ref, x1_ref, rw_ref, p_ref, ys_ref, pn_ref, wpg_ref, wpl_ref, o_ref,
                    ybuf, sems):
    i = pl.program_id(0)
    tp = x1_ref.shape[0]
    cur = i % 2

    def rows(sref, buf, start):
        def body(t, carry):
            for k in range(TOP_K):
                s = sref[0, 0, TOP_K * t + k]
                cp = pltpu.make_async_copy(
                    ys_ref.at[pl.ds(pl.multiple_of(s * SUBLANES, SUBLANES), SUBLANES)],
                    ybuf.at[buf, k, pl.ds(pl.multiple_of(t * SUBLANES, SUBLANES), SUBLANES)], sems.at[buf])
                if start:
                    cp.start()
                else:
                    cp.wait()
            return carry

        lax.fori_loop(0, tp, body, 0, unroll=4 if start else 8)

    @pl.when(i == 0)
    def _():
        rows(slot_ref, 0, True)

    @pl.when(i + 1 < pl.num_programs(0))
    def _():
        rows(next_slot_ref, 1 - cur, True)

    rows(slot_ref, cur, False)

    rw = rw_ref[...]
    x2 = (x1_ref[...] + rw[:, 0:1] * _load_token_tiles(ybuf.at[cur, 0], tp)
          + rw[:, 1:2] * _load_token_tiles(ybuf.at[cur, 1], tp))
    h3 = _rms(x2, pn_ref[...]).astype(BF16)
    pe = p_ref[0].astype(BF16)
    cw = 2 * LANES
    for c in range(D_MODEL // cw):
        cols = slice(c * cw, (c + 1) * cw)
        gate = _sigmoid(jnp.dot(h3, wpg_ref[:, cols], preferred_element_type=F32))
        o_ref[:, cols] = x2[:, cols] + gate * jnp.dot(pe, wpl_ref[:, cols], preferred_element_type=F32)


def _combine_ple(x1, rw, slots, ys, p, layer, pn, wpg, wpl, tp=256):
    n = x1.shape[0]
    d = D_MODEL
    blk = lambda i: (i, 0)
    const = lambda i: (0, 0)
    nsteps = n // tp
    slot_blocks = slots.reshape(nsteps, 1, TOP_K * tp)
    return pl.pallas_call(
        _combine_kernel,
        grid=(nsteps,),
        in_specs=[pl.BlockSpec((1, 1, TOP_K * tp), lambda i: (i, 0, 0), memory_space=pltpu.SMEM),
                  pl.BlockSpec((1, 1, TOP_K * tp), lambda i: (jnp.minimum(i + 1, nsteps - 1), 0, 0),
                               memory_space=pltpu.SMEM),
                  pl.BlockSpec((tp, d), blk), pl.BlockSpec((tp, LANES), blk),
                  pl.BlockSpec((1, tp, PLE_DIM), lambda i: (layer, i, 0)),
                  pl.BlockSpec(memory_space=pl.ANY),
                  pl.BlockSpec((1, d), const), pl.BlockSpec((d, d), const),
                  pl.BlockSpec((PLE_DIM, d), const)],
        out_specs=pl.BlockSpec((tp, d), blk),
        out_shape=jax.ShapeDtypeStruct((n, d), F32),
        scratch_shapes=[pltpu.VMEM((2, TOP_K, tp * SUBLANES, LANES), F32), pltpu.SemaphoreType.DMA((2,))],
        compiler_params=_params(1),
        name="combine_ple",
    )(slot_blocks, slot_blocks, x1, rw, p, ys, pn, wpg, wpl)


def _pack_w_in(w_in):
    o = IN_OFFSETS
    w = w_in.astype(BF16)
    pad = jnp.zeros((D_MODEL, LANES - GLA_GATE_RANK - 2 * GDN_HEADS), BF16)
    cols = [w[:, :o[6]], w[:, o[7]:o[9]], w[:, o[11]:], w[:, o[6]:o[7]], w[:, o[9]:o[11]], pad]
    return jnp.concatenate(cols, axis=1)


def _layer(x, p, layer, cos_t, sin_t, batch, seq, attn_norm, w_in, q_norm, k_norm, sinks, gla_wa2, gla_ba,
           gla_norm, gdn_conv, gdn_a_log, gdn_dt_bias, gdn_norm, w_br_a, w_br_b, w_br_c, w_o,
           ffn_norm, w_coarse, b_coarse, w_fine, b_fine, w_gate_e, w_up_e, w_down_e,
           ple_norm, w_ple_gate, w_ple):
    n = x.shape[0]
    aq, akv, bqk, bv, br, cqkv, cg, gates, small = _inproj(x, attn_norm[None], _pack_w_in(w_in))
    ya = _swa(aq, akv, cos_t, sin_t, q_norm, k_norm, sinks, batch, seq)
    yb = _gla(bqk, bv, br, small, gla_wa2, gla_ba, gla_norm, batch, seq)
    yc = _gdn(cqkv, cg, small, gdn_conv, gdn_a_log, gdn_dt_bias, gdn_norm, batch, seq)

    wr = jnp.zeros((D_MODEL, LANES), F32).at[:, :N_GROUPS].set(w_coarse)
    wr = wr.at[:, N_GROUPS:N_GROUPS + N_EXPERTS].set(w_fine)
    brow = jnp.zeros((1, LANES), F32).at[0, :N_GROUPS].set(b_coarse)
    brow = brow.at[0, N_GROUPS:N_GROUPS + N_EXPERTS].set(b_fine)
    x1, h2, ri, rw, cnt = _merge(ya, yb, yc, gates, x, w_br_a.astype(BF16), w_br_b.astype(BF16),
                                 w_br_c.astype(BF16), w_o.astype(BF16), ffn_norm[None], wr, brow)

    counts = cnt[0, :N_EXPERTS]
    padded = (counts + MOE_ROWS - 1) // MOE_ROWS * MOE_ROWS
    pad_end = jnp.cumsum(padded)
    pad_start = pad_end - padded
    slots = (pad_start[ri[:, :TOP_K]] + ri[:, TOP_K:2 * TOP_K]).astype(I32)
    n_slots = n * TOP_K + N_EXPERTS * MOE_ROWS
    block_start = jnp.arange(n_slots // MOE_ROWS, dtype=I32) * MOE_ROWS
    last_owner = jnp.max(jnp.where(padded > 0, jnp.arange(N_EXPERTS, dtype=I32), 0))
    block_expert = jnp.minimum(jnp.sum((pad_end[None, :] <= block_start[:, None]).astype(I32), axis=1),
                               last_owner).astype(I32)

    experts = jnp.arange(N_EXPERTS, dtype=I32)
    owns = padded > 0
    later = lax.cummin(jnp.where(owns, experts, N_EXPERTS)[::-1])[::-1]
    next_owner = jnp.concatenate([later[1:], jnp.full((1,), N_EXPERTS, I32)])
    next_owner = jnp.where(next_owner < N_EXPERTS, next_owner, experts)
    parity = (jnp.cumsum(owns.astype(I32)) - 1) % 2
    next_expert = next_owner[block_expert].astype(I32)
    block_parity = jnp.maximum(parity[block_expert], 0).astype(I32)

    token_ids = jnp.repeat(jnp.arange(n, dtype=I32), TOP_K)
    slot_token = jnp.zeros((n_slots,), I32).at[slots.reshape(-1)].set(token_ids, unique_indices=True)
    ys = _experts(h2, slot_token, block_expert, next_expert, block_parity, w_gate_e, w_up_e, w_down_e, layer)
    return _combine_ple(x1, rw, slots, ys, p, layer, ple_norm[None], w_ple_gate.astype(BF16),
                        w_ple.astype(BF16))


def kernel(x, p, positions, attn_norm, w_in, q_norm, k_norm, sinks, gla_wa2, gla_ba, gla_norm, gdn_conv, gdn_a_log, gdn_dt_bias, gdn_norm, w_br_a, w_br_b, w_br_c, w_o, ffn_norm, w_coarse, b_coarse, w_fine, b_fine, w_gate_e, w_up_e, w_down_e, ple_norm, w_ple_gate, w_ple):
    batch, seq, d = x.shape
    n = batch * seq
    depth = p.shape[0]
    cos_t, sin_t = _rope_tables(positions)
    xf = x.reshape(n, d)
    pf = p.reshape(depth, n, p.shape[-1])
    per_layer = (attn_norm, w_in, q_norm, k_norm, sinks, gla_wa2, gla_ba, gla_norm, gdn_conv, gdn_a_log,
                 gdn_dt_bias, gdn_norm, w_br_a, w_br_b, w_br_c, w_o, ffn_norm, w_coarse, b_coarse,
                 w_fine, b_fine, w_gate_e, w_up_e, w_down_e, ple_norm, w_ple_gate, w_ple)
    stacked = (w_gate_e, w_up_e, w_down_e)
    for i in range(depth):
        xf = _layer(xf, pf, i, cos_t, sin_t, batch, seq,
                    *[a if any(a is s for s in stacked) else a[i] for a in per_layer])
    return xf.reshape(batch, seq, d)
```

```python
import functools
import math

import numpy as np
import jax
import jax.numpy as jnp
from jax import lax
from jax.experimental import pallas as pl
from jax.experimental.pallas import tpu as pltpu

F32 = jnp.float32
BF16 = jnp.bfloat16
I32 = jnp.int32

D_MODEL = 1024
PLE_DIM = 256
NORM_EPS = 1e-6
MASK_VALUE = -1e30

SWA_Q_HEADS = 8
SWA_KV_HEADS = 2
SWA_HEAD_DIM = 64
SWA_WINDOW = 128
ROT_DIM = SWA_HEAD_DIM // 4
ROPE_THETA = 500000.0

GLA_HEADS = 4
GLA_DK = 64
GLA_DV = 128
GLA_GATE_RANK = 16
GLA_GATE_NORM = 16.0
CHUNK = 64

GDN_HEADS = 4
GDN_DK = 128
GDN_DV = 128
GDN_CONV = 4

N_GROUPS = 4
EXPERTS_PER_GROUP = 8
N_EXPERTS = N_GROUPS * EXPERTS_PER_GROUP
EXPERT_FF = 512
TOP_K = 2

SWA_Q = SWA_Q_HEADS * SWA_HEAD_DIM
SWA_KV = SWA_KV_HEADS * SWA_HEAD_DIM
GLA_K = GLA_HEADS * GLA_DK
GLA_V = GLA_HEADS * GLA_DV
GDN_K = GDN_HEADS * GDN_DK
GDN_V = GDN_HEADS * GDN_DV
GDN_QKV = 2 * GDN_K + GDN_V
IN_SPLITS = (SWA_Q, SWA_KV, SWA_KV, GLA_K, GLA_K, GLA_V, GLA_GATE_RANK, GLA_V,
             GDN_QKV, GDN_HEADS, GDN_HEADS, GDN_V, 3 * D_MODEL)
IN_OFFSETS = tuple(int(o) for o in np.cumsum((0,) + IN_SPLITS))

LANES = 128
SMALL_LR = 0
SMALL_BETA = GLA_GATE_RANK
SMALL_A = GLA_GATE_RANK + GDN_HEADS

OUT_WIDTHS = (SWA_Q, 2 * SWA_KV, 2 * GLA_K, GLA_V, GLA_V, GDN_QKV, GDN_V, 3 * D_MODEL, LANES)

MOE_ROWS = 256
VMEM_LIMIT = 56 * 1024 * 1024


def _params(n_axes):
    return pltpu.CompilerParams(dimension_semantics=("arbitrary",) * n_axes,
                                vmem_limit_bytes=VMEM_LIMIT)


def _dot(a, b):
    return jnp.dot(a.astype(BF16), b.astype(BF16), preferred_element_type=F32)


def _dot_nt(a, b):
    return lax.dot_general(a.astype(BF16), b.astype(BF16), (((1,), (1,)), ((), ())),
                           preferred_element_type=F32)


def _dot_tn(a, b):
    return lax.dot_general(a.astype(BF16), b.astype(BF16), (((0,), (0,)), ((), ())),
                           preferred_element_type=F32)


def _split2(x):
    hi = x.astype(BF16)
    lo = (x - hi.astype(F32)).astype(BF16)
    return hi, lo


def _dot_exact_lhs(a, x):
    hi = x.astype(BF16)
    r = x - hi.astype(F32)
    mid = r.astype(BF16)
    lo = (r - mid.astype(F32)).astype(BF16)
    a = a.astype(BF16)
    return (jnp.dot(a, hi, preferred_element_type=F32) + jnp.dot(a, mid, preferred_element_type=F32)
            + jnp.dot(a, lo, preferred_element_type=F32))


def _dot_hi(a, b):
    ah, al = _split2(a)
    bh, bl = _split2(b)
    return (jnp.dot(ah, bh, preferred_element_type=F32) + jnp.dot(ah, bl, preferred_element_type=F32)
            + jnp.dot(al, bh, preferred_element_type=F32))


def _rms(x, g):
    return x * lax.rsqrt(jnp.mean(x * x, axis=-1, keepdims=True) + NORM_EPS) * g


def _sigmoid(x):
    return 0.5 * jnp.tanh(0.5 * x) + 0.5


def _silu(x):
    return x * _sigmoid(x)


def _softplus(x):
    return jnp.maximum(x, 0.0) + jnp.log(1.0 + jnp.exp(-jnp.abs(x)))


def _log_sigmoid(x):
    return -_softplus(-x)


def _iota(shape, axis):
    return lax.broadcasted_iota(I32, shape, axis)


SUBLANES = 8
ROW_TILES = D_MODEL // LANES


def _store_token_tiles(ref, x):
    t = x.shape[0]
    for s in range(ROW_TILES):
        ref[pl.ds(s, t, stride=ROW_TILES), :] = x[:, s * LANES:(s + 1) * LANES]


def _load_token_tiles(ref, t):
    return jnp.concatenate([ref[pl.ds(s, t, stride=ROW_TILES), :] for s in range(ROW_TILES)], axis=-1)


def _chunk_tril(n, strict=False):
    r = _iota((n, n), 0)
    c = _iota((n, n), 1)
    same = (r // CHUNK) == (c // CHUNK)
    return same & ((c < r) if strict else (c <= r))


def _inproj_kernel(x_ref, g_ref, w_ref, *out_refs):
    h = _rms(x_ref[...], g_ref[...]).astype(BF16)
    off = 0
    for o_ref in out_refs:
        wd = o_ref.shape[-1]
        o_ref[...] = jnp.dot(h, w_ref[:, off:off + wd], preferred_element_type=F32)
        off += wd


def _inproj(x, g, w, tm=256):
    n = x.shape[0]
    nc = w.shape[1]
    return pl.pallas_call(
        _inproj_kernel,
        grid=(n // tm,),
        in_specs=[pl.BlockSpec((tm, D_MODEL), lambda i: (i, 0)),
                  pl.BlockSpec((1, D_MODEL), lambda i: (0, 0)),
                  pl.BlockSpec((D_MODEL, nc), lambda i: (0, 0), pipeline_mode=pl.Buffered(1))],
        out_specs=[pl.BlockSpec((tm, wd), lambda i: (i, 0)) for wd in OUT_WIDTHS],
        out_shape=[jax.ShapeDtypeStruct((n, wd), F32) for wd in OUT_WIDTHS],
        compiler_params=_params(1),
        name="inproj",
    )(x, g, w)


def _rope_kernel(pos_ref, invf_ref, cos_ref, sin_ref):
    ang = pos_ref[...].astype(F32) * invf_ref[...]
    d = _iota(ang.shape, 1) % SWA_HEAD_DIM
    half = ROT_DIM // 2
    cos_ref[...] = jnp.where(d < ROT_DIM, jnp.cos(ang), 1.0)
    s = jnp.sin(ang)
    sin_ref[...] = jnp.where(d < half, -s, jnp.where(d < ROT_DIM, s, 0.0))


def _rope_tables(positions, tm=1024):
    n = positions.size
    tm = min(tm, n)
    inv_freq = 1.0 / (ROPE_THETA ** (jnp.arange(0, ROT_DIM, 2, dtype=F32) / ROT_DIM))
    lane_freq = jnp.tile(jnp.concatenate([inv_freq, inv_freq, jnp.zeros(SWA_HEAD_DIM - ROT_DIM, F32)]),
                         LANES // SWA_HEAD_DIM)[None]
    return pl.pallas_call(
        _rope_kernel,
        grid=(n // tm,),
        in_specs=[pl.BlockSpec((tm, 1), lambda i: (i, 0)),
                  pl.BlockSpec((1, LANES), lambda i: (0, 0))],
        out_specs=[pl.BlockSpec((tm, LANES), lambda i: (i, 0))] * 2,
        out_shape=[jax.ShapeDtypeStruct((n, LANES), F32)] * 2,
        compiler_params=_params(1),
        name="rope_tables",
    )(positions.reshape(n, 1), lane_freq)


def _dot_exact_rhs(x, b):
    hi, lo = _split2(x)
    return jnp.dot(hi, b, preferred_element_type=F32) + jnp.dot(lo, b, preferred_element_type=F32)


def _rope_matrices():
    m = _iota((LANES, LANES), 0)
    l = _iota((LANES, LANES), 1)
    d = l % SWA_HEAD_DIM
    half = ROT_DIM // 2
    same_head = (m // SWA_HEAD_DIM) == (l // SWA_HEAD_DIM)
    partner = ((d < half) & (m == l + half)) | ((d >= half) & (d < ROT_DIM) & (m == l - half))
    return jnp.where(same_head, 1.0, 0.0).astype(BF16), jnp.where(partner, 1.0, 0.0).astype(BF16)


def _norm_rope(x, gain, cos, sin, same_head, partner):
    ms = _dot_exact_rhs(x * x, same_head) * (1.0 / SWA_HEAD_DIM)
    xn = x * lax.rsqrt(ms + NORM_EPS) * gain
    return xn * cos + _dot_exact_rhs(xn, partner) * sin


def _swa_kernel(sink_ref, q_ref, kv_ref, cos_ref, sin_ref, qn_ref, kn_ref, o_ref, kprev_ref, vprev_ref):
    j = pl.program_id(1)
    w = SWA_WINDOW
    hd = SWA_HEAD_DIM
    group = SWA_Q_HEADS // SWA_KV_HEADS

    @pl.when(j == 0)
    def _():
        kprev_ref[...] = jnp.zeros_like(kprev_ref)
        vprev_ref[...] = jnp.zeros_like(vprev_ref)

    cos = cos_ref[...]
    sin = sin_ref[...]
    same_head, partner = _rope_matrices()
    kc = _norm_rope(kv_ref[:, :SWA_KV], kn_ref[...], cos, sin, same_head, partner).astype(BF16)
    vc = kv_ref[:, SWA_KV:].astype(BF16)
    k_all = jnp.concatenate([kprev_ref[...], kc], axis=0)
    v_all = jnp.concatenate([vprev_ref[...], vc], axis=0)
    kprev_ref[...] = kc
    vprev_ref[...] = vc

    rows = group * w
    qi = _iota((rows, 2 * w), 0) % w
    kj = _iota((rows, 2 * w), 1)
    first_key = jnp.where(j > 0, 0, w)
    mask = (kj > qi) & (kj <= qi + w) & (kj >= first_key)
    head_of_row = _iota((rows, 1), 0) // w
    qrs = [_norm_rope(q_ref[:, c * LANES:(c + 1) * LANES], qn_ref[...], cos, sin, same_head, partner)
           for c in range(SWA_Q // LANES)]
    per_block = LANES // hd
    qgs, sinks = [], []
    for g in range(SWA_KV_HEADS):
        heads = range(g * group, (g + 1) * group)
        qgs.append(jnp.concatenate(
            [qrs[h // per_block][:, (h % per_block) * hd:(h % per_block + 1) * hd] for h in heads], axis=0))
        sk = jnp.full((rows, 1), sink_ref[g * group], F32)
        for i in range(1, group):
            sk = jnp.where(head_of_row == i, sink_ref[g * group + i], sk)
        sinks.append(sk)
    ss = [jnp.where(mask, _dot_nt(qgs[g], k_all[:, g * hd:(g + 1) * hd]) * (hd ** -0.5), MASK_VALUE)
          for g in range(SWA_KV_HEADS)]
    ms = [jnp.maximum(jnp.max(s, axis=-1, keepdims=True), sk) for s, sk in zip(ss, sinks)]
    es = [jnp.exp(s - m).astype(BF16) for s, m in zip(ss, ms)]
    ones = jnp.ones((2 * w, hd), BF16)
    dens = [jnp.dot(e, ones, preferred_element_type=F32) + jnp.exp(sk - m) for e, sk, m in zip(es, sinks, ms)]
    for g in range(SWA_KV_HEADS):
        og = jnp.dot(es[g], v_all[:, g * hd:(g + 1) * hd], preferred_element_type=F32) / dens[g]
        for i in range(group):
            h = g * group + i
            o_ref[:, h * hd:(h + 1) * hd] = og[i * w:(i + 1) * w]


def _swa(aq, akv, cos_t, sin_t, q_norm, k_norm, sinks, batch, seq):
    n = aq.shape[0]
    w = SWA_WINDOW
    nq = seq // w
    cur = lambda b, j: (b * nq + j, 0)
    rep = LANES // SWA_HEAD_DIM
    return pl.pallas_call(
        _swa_kernel,
        grid=(batch, nq),
        in_specs=[pl.BlockSpec(memory_space=pltpu.SMEM),
                  pl.BlockSpec((w, SWA_Q), cur),
                  pl.BlockSpec((w, 2 * SWA_KV), cur),
                  pl.BlockSpec((w, LANES), cur),
                  pl.BlockSpec((w, LANES), cur),
                  pl.BlockSpec((1, LANES), lambda b, j: (0, 0)),
                  pl.BlockSpec((1, LANES), lambda b, j: (0, 0))],
        out_specs=pl.BlockSpec((w, SWA_Q), cur),
        out_shape=jax.ShapeDtypeStruct((n, SWA_Q), F32),
        scratch_shapes=[pltpu.VMEM((w, SWA_KV), BF16), pltpu.VMEM((w, SWA_KV), BF16)],
        compiler_params=_params(2),
        name="swa",
    )(sinks, aq, akv, cos_t, sin_t, jnp.tile(q_norm, rep)[None], jnp.tile(k_norm, rep)[None])


def _gla_kernel(qk_ref, v_ref, r_ref, sm_ref, wa_ref, ba_ref, gn_ref, o_ref, state_ref):
    @pl.when(pl.program_id(1) == 0)
    def _():
        state_ref[...] = jnp.zeros_like(state_ref)

    tc = qk_ref.shape[0]
    la = _log_sigmoid(_dot(sm_ref[...], wa_ref[...]) + ba_ref[...]) * (1.0 / GLA_GATE_NORM)
    tri = jnp.where(_chunk_tril(tc), 1.0, 0.0).astype(BF16)
    b = _dot_exact_lhs(tri, la)
    causal = _iota((CHUNK, CHUNK), 1) <= _iota((CHUNK, CHUNK), 0)
    n_chunks = tc // CHUNK
    qes, dcols, intra, upd = [], [], {}, {}
    for c in range(n_chunks):
        rows = slice(c * CHUNK, (c + 1) * CHUNK)
        bc = b[rows]
        b_last = bc[CHUNK - 1:CHUNK]
        k = qk_ref[rows, GLA_K:]
        qe = qk_ref[rows, :GLA_K] * (GLA_DK ** -0.5) * jnp.exp(bc)
        ke = k * jnp.exp(-bc)
        kd = k * jnp.exp(b_last - bc)
        qes.append(qe)
        dcols.append(jnp.transpose(jnp.broadcast_to(jnp.exp(b_last), (GLA_DV, GLA_K))))
        for h in range(GLA_HEADS):
            ks = slice(h * GLA_DK, (h + 1) * GLA_DK)
            vh = v_ref[rows, h * GLA_DV:(h + 1) * GLA_DV]
            att = jnp.where(causal, _dot_nt(qe[:, ks], ke[:, ks]), 0.0)
            intra[c, h] = _dot(att, vh)
            upd[c, h] = _dot_tn(kd[:, ks], vh)
    for h in range(GLA_HEADS):
        ks = slice(h * GLA_DK, (h + 1) * GLA_DK)
        vs = slice(h * GLA_DV, (h + 1) * GLA_DV)
        st = state_ref[h]
        for c in range(n_chunks):
            rows = slice(c * CHUNK, (c + 1) * CHUNK)
            o = intra[c, h] + _dot(qes[c][:, ks], st)
            st = dcols[c][ks] * st + upd[c, h]
            o_ref[rows, vs] = _rms(o, gn_ref[...]) * _silu(r_ref[rows, vs])
        state_ref[h] = st


def _gla(bqk, bv, br, small, wa2, ba, gn, batch, seq, tc=256):
    n = bqk.shape[0]
    nt = seq // tc
    blk = lambda b, j: (b * nt + j, 0)
    const = lambda b, j: (0, 0)
    wa_pad = jnp.zeros((LANES, GLA_K), F32).at[SMALL_LR:SMALL_LR + GLA_GATE_RANK].set(wa2)
    return pl.pallas_call(
        _gla_kernel,
        grid=(batch, nt),
        in_specs=[pl.BlockSpec((tc, 2 * GLA_K), blk),
                  pl.BlockSpec((tc, GLA_V), blk),
                  pl.BlockSpec((tc, GLA_V), blk),
                  pl.BlockSpec((tc, LANES), blk),
                  pl.BlockSpec((LANES, GLA_K), const),
                  pl.BlockSpec((1, GLA_K), const),
                  pl.BlockSpec((1, GLA_DV), const)],
        out_specs=pl.BlockSpec((tc, GLA_V), blk),
        out_shape=jax.ShapeDtypeStruct((n, GLA_V), F32),
        scratch_shapes=[pltpu.VMEM((GLA_HEADS, GLA_DK, GLA_DV), F32)],
        compiler_params=_params(2),
        name="gla",
    )(bqk, bv, br, small, wa_pad, ba[None], gn[None])


def _l2(x):
    return x * lax.rsqrt(jnp.sum(x * x, axis=-1, keepdims=True) + NORM_EPS)


def _gdn_kernel(x_ref, xp_ref, g_ref, sm_ref, cw_ref, al_ref, dt_ref, gn_ref, o_ref, state_ref):
    first = pl.program_id(1) == 0

    @pl.when(first)
    def _():
        state_ref[...] = jnp.zeros_like(state_ref)

    tc = x_ref.shape[0]
    halo = xp_ref.shape[0]
    x = x_ref[...]
    xprev = jnp.where(first, 0.0, xp_ref[...])
    sm = sm_ref[...]
    row = _iota((halo, 1), 0)
    conv = cw_ref[GDN_CONV - 1:GDN_CONV] * x
    for t in range(1, GDN_CONV):
        rolled = pltpu.roll(x, t, 0)
        head = jnp.where(row < t, pltpu.roll(xprev, t, 0), rolled[:halo])
        shifted = jnp.concatenate([head, rolled[halo:]], axis=0)
        conv = conv + cw_ref[GDN_CONV - 1 - t:GDN_CONV - t] * shifted
    qkv = _silu(conv)

    beta_all = _sigmoid(sm)
    g_all = -jnp.exp(al_ref[...]) * _softplus(sm + dt_ref[...])
    tri = jnp.where(_chunk_tril(tc), 1.0, 0.0).astype(BF16)
    big_g = _dot_exact_lhs(tri, g_all)
    g_rows = jnp.transpose(big_g)

    r = _iota((CHUNK, CHUNK), 0)
    cidx = _iota((CHUNK, CHUNK), 1)
    causal = cidx <= r
    strict = cidx < r
    eye = jnp.where(cidx == r, 1.0, 0.0)

    units = [(c, h) for c in range(tc // CHUNK) for h in range(GDN_HEADS)]
    qs, ks, egs, g_lasts, gcs, decays, lows, rhss = [], [], [], [], [], [], [], []
    for c, h in units:
        rows = slice(c * CHUNK, (c + 1) * CHUNK)
        q = _l2(qkv[rows, h * GDN_DK:(h + 1) * GDN_DK]) * (GDN_DK ** -0.5)
        k = _l2(qkv[rows, GDN_K + h * GDN_DK:GDN_K + (h + 1) * GDN_DK])
        v = qkv[rows, 2 * GDN_K + h * GDN_DV:2 * GDN_K + (h + 1) * GDN_DV]
        beta = beta_all[rows, SMALL_BETA + h:SMALL_BETA + h + 1]
        gc = big_g[rows, SMALL_A + h:SMALL_A + h + 1]
        g_row = g_rows[SMALL_A + h:SMALL_A + h + 1, c * CHUNK:(c + 1) * CHUNK]
        decay = jnp.where(causal, jnp.exp(jnp.where(causal, gc - g_row, 0.0)), 0.0)
        eg = jnp.exp(gc)
        qs.append(q)
        ks.append(k)
        egs.append(eg)
        gcs.append(gc)
        g_lasts.append(gc[CHUNK - 1:CHUNK])
        decays.append(decay)
        lows.append(jnp.where(strict, beta * _dot_nt(k, k) * decay, 0.0))
        rhss.append(jnp.concatenate([v * beta, k * (beta * eg)], axis=-1))
    invs = [eye - low for low in lows]
    pws = lows
    for _ in range(int(math.log2(CHUNK)) - 1):
        pws = [_dot(pw, pw) for pw in pws]
        invs = [inv + _dot(inv, pw) for inv, pw in zip(invs, pws)]
    sols = [_dot(inv, rhs) for inv, rhs in zip(invs, rhss)]
    atts = [_dot_nt(q, k) * decay for q, k, decay in zip(qs, ks, decays)]
    k_decs = [k * jnp.exp(gl - gc) for k, gl, gc in zip(ks, g_lasts, gcs)]
    kws = [_dot_tn(k_dec, sol) for k_dec, sol in zip(k_decs, sols)]
    aws = [_dot(att, sol) for att, sol in zip(atts, sols)]

    for i, (c, h) in enumerate(units):
        rows = slice(c * CHUNK, (c + 1) * CHUNK)
        st = state_ref[h]
        lhs = jnp.concatenate([kws[i][:, GDN_DV:], qs[i] * egs[i] - aws[i][:, GDN_DV:]], axis=0)
        prod = _dot(lhs, st)
        o = prod[GDN_DK:] + aws[i][:, :GDN_DV]
        state_ref[h] = jnp.exp(g_lasts[i]) * st - prod[:GDN_DK] + kws[i][:, :GDN_DV]
        vs = slice(h * GDN_DV, (h + 1) * GDN_DV)
        o_ref[rows, vs] = _rms(o, gn_ref[...]) * _silu(g_ref[rows, vs])


def _gdn(cqkv, cg, small, conv_w, a_log, dt_bias, gn, batch, seq, tc=256, halo=8):
    n = cqkv.shape[0]
    nt = seq // tc
    blk = lambda b, j: (b * nt + j, 0)
    const = lambda b, j: (0, 0)
    prev = lambda b, j: (jnp.maximum((b * nt + j) * (tc // halo) - 1, 0), 0)
    al_row = jnp.zeros((1, LANES), F32).at[0, SMALL_A:SMALL_A + GDN_HEADS].set(a_log)
    dt_row = jnp.zeros((1, LANES), F32).at[0, SMALL_A:SMALL_A + GDN_HEADS].set(dt_bias)
    return pl.pallas_call(
        _gdn_kernel,
        grid=(batch, nt),
        in_specs=[pl.BlockSpec((tc, GDN_QKV), blk),
                  pl.BlockSpec((halo, GDN_QKV), prev),
                  pl.BlockSpec((tc, GDN_V), blk),
                  pl.BlockSpec((tc, LANES), blk),
                  pl.BlockSpec((GDN_CONV, GDN_QKV), const),
                  pl.BlockSpec((1, LANES), const),
                  pl.BlockSpec((1, LANES), const),
                  pl.BlockSpec((1, GDN_DV), const)],
        out_specs=pl.BlockSpec((tc, GDN_V), blk),
        out_shape=jax.ShapeDtypeStruct((n, GDN_V), F32),
        scratch_shapes=[pltpu.VMEM((GDN_HEADS, GDN_DK, GDN_DV), F32)],
        compiler_params=_params(2),
        name="gdn",
    )(cqkv, cqkv, cg, small, conv_w, al_row, dt_row, gn[None])


def _first_argmax(x, lane):
    m = jnp.max(x, axis=-1, keepdims=True)
    idx = jnp.min(jnp.where(x == m, lane.astype(F32), float(LANES)), axis=-1, keepdims=True)
    return m, idx.astype(I32)


def _merge_kernel(ya_ref, yb_ref, yc_ref, gt_ref, x_ref, wa_ref, wb_ref, wc_ref, wo_ref, fn_ref,
                  wr_ref, br_ref, x1_ref, h2_ref, ri_ref, rw_ref, cnt_ref, carry_ref):
    @pl.when(pl.program_id(0) == 0)
    def _():
        carry_ref[...] = jnp.zeros_like(carry_ref)

    d = D_MODEL
    ya = ya_ref[...].astype(BF16)
    yb = yb_ref[...].astype(BF16)
    yc = yc_ref[...].astype(BF16)
    cw = 2 * LANES
    merged = []
    for c in range(d // cw):
        lo = c * cw
        merged.append((
            _sigmoid(gt_ref[:, lo:lo + cw]) * jnp.dot(ya, wa_ref[:, lo:lo + cw], preferred_element_type=F32)
            + _sigmoid(gt_ref[:, d + lo:d + lo + cw]) * jnp.dot(yb, wb_ref[:, lo:lo + cw], preferred_element_type=F32)
            + _sigmoid(gt_ref[:, 2 * d + lo:2 * d + lo + cw]) * jnp.dot(yc, wc_ref[:, lo:lo + cw], preferred_element_type=F32)
        ).astype(BF16))
    merged = jnp.concatenate(merged, axis=-1)
    x1 = x_ref[...] + jnp.dot(merged, wo_ref[...], preferred_element_type=F32)
    x1_ref[...] = x1
    h2 = _rms(x1, fn_ref[...])
    _store_token_tiles(h2_ref, h2)

    logits = _dot_hi(h2, wr_ref[...]) + br_ref[...]
    tm = logits.shape[0]
    lane = _iota((tm, LANES), 1)
    neg = -jnp.inf
    is_c = lane < N_GROUPS
    cm, g_idx = _first_argmax(jnp.where(is_c, logits, neg), lane)
    g_prob = 1.0 / jnp.sum(jnp.where(is_c, jnp.exp(logits - cm), 0.0), axis=-1, keepdims=True)
    sel = (lane >= N_GROUPS) & (((lane - N_GROUPS) // EXPERTS_PER_GROUP) == g_idx)
    fm = jnp.max(jnp.where(sel, logits, neg), axis=-1, keepdims=True)
    ef = jnp.where(sel, jnp.exp(logits - fm), 0.0)
    p1, i1 = _first_argmax(jnp.where(sel, ef, neg), lane)
    p2, i2 = _first_argmax(jnp.where(sel & (lane != i1), ef, neg), lane)
    w1 = g_prob * p1 / (p1 + p2)
    w2 = g_prob * p2 / (p1 + p2)
    e1 = i1 - N_GROUPS
    e2 = i2 - N_GROUPS

    oh = jnp.where((lane == e1) | (lane == e2 + N_EXPERTS), 1.0, 0.0)
    stril = jnp.where(_iota((tm, tm), 1) < _iota((tm, tm), 0), 1.0, 0.0).astype(BF16)
    before = jnp.dot(stril, oh.astype(BF16), preferred_element_type=F32)
    tot = jnp.sum(oh, axis=0, keepdims=True)
    tot_first = jnp.where(lane[:1] < N_EXPERTS, tot, 0.0)
    carry = carry_ref[...]
    base = carry + pltpu.roll(carry + tot_first, N_EXPERTS, 1)
    ranks = oh * (before + base)
    rank1 = jnp.sum(jnp.where(lane < N_EXPERTS, ranks, 0.0), axis=-1, keepdims=True)
    rank2 = jnp.sum(jnp.where(lane < N_EXPERTS, 0.0, ranks), axis=-1, keepdims=True)
    new_carry = carry + tot_first + pltpu.roll(tot - tot_first, LANES - N_EXPERTS, 1)
    carry_ref[...] = new_carry
    cnt_ref[...] = jnp.broadcast_to(new_carry, cnt_ref.shape).astype(I32)

    ri_ref[...] = jnp.where(lane == 0, e1, jnp.where(lane == 1, e2, jnp.where(
        lane == 2, rank1.astype(I32), jnp.where(lane == 3, rank2.astype(I32), 0))))
    rw_ref[...] = jnp.where(lane == 0, w1, jnp.where(lane == 1, w2, 0.0))


def _merge(ya, yb, yc, gates, x, wa, wb, wc, wo, fn, wr, br, tm=256):
    n = x.shape[0]
    d = D_MODEL
    blk = lambda i: (i, 0)
    const = lambda i: (0, 0)
    return pl.pallas_call(
        _merge_kernel,
        grid=(n // tm,),
        in_specs=[pl.BlockSpec((tm, SWA_Q), blk), pl.BlockSpec((tm, GLA_V), blk),
                  pl.BlockSpec((tm, GDN_V), blk), pl.BlockSpec((tm, 3 * d), blk),
                  pl.BlockSpec((tm, d), blk),
                  pl.BlockSpec((SWA_Q, d), const), pl.BlockSpec((GLA_V, d), const),
                  pl.BlockSpec((GDN_V, d), const), pl.BlockSpec((d, d), const),
                  pl.BlockSpec((1, d), const), pl.BlockSpec((d, LANES), const),
                  pl.BlockSpec((1, LANES), const)],
        out_specs=[pl.BlockSpec((tm, d), blk), pl.BlockSpec((tm * SUBLANES, LANES), blk),
                   pl.BlockSpec((tm, LANES), blk), pl.BlockSpec((tm, LANES), blk),
                   pl.BlockSpec((8, LANES), const)],
        out_shape=[jax.ShapeDtypeStruct((n, d), F32), jax.ShapeDtypeStruct((n * SUBLANES, LANES), F32),
                   jax.ShapeDtypeStruct((n, LANES), I32), jax.ShapeDtypeStruct((n, LANES), F32),
                   jax.ShapeDtypeStruct((8, LANES), I32)],
        scratch_shapes=[pltpu.VMEM((1, LANES), F32)],
        compiler_params=_params(1),
        name="merge_route",
    )(ya, yb, yc, gates, x, wa, wb, wc, wo, fn, wr, br)


def _dispatch_kernel(fill_ref, slot_ref, h_ref, xs_ref, zbuf, sem, zsem):
    td = h_ref.shape[0] // SUBLANES
    n_blocks = xs_ref.shape[0] // (MOE_ROWS * SUBLANES)

    def zero_fill(start):
        def run(cp):
            if start:
                cp.start()
            else:
                cp.wait()

        def pad_rows(e, carry):
            lo = fill_ref[e]
            n_pad = fill_ref[N_EXPERTS + e] - lo
            bit = MOE_ROWS // 2
            while bit >= 1:
                off = lo + (n_pad // (2 * bit)) * (2 * bit)

                @pl.when((n_pad & bit) != 0)
                def _(off=off, bit=bit):
                    run(pltpu.make_async_copy(
                        zbuf.at[pl.ds(0, bit * SUBLANES)],
                        xs_ref.at[pl.ds(pl.multiple_of(off * SUBLANES, SUBLANES), bit * SUBLANES)], zsem))

                bit //= 2
            return carry

        lax.fori_loop(0, N_EXPERTS, pad_rows, 0)

        def unused_block(b, carry):
            run(pltpu.make_async_copy(
                zbuf, xs_ref.at[pl.ds(pl.multiple_of(b * (MOE_ROWS * SUBLANES), SUBLANES), MOE_ROWS * SUBLANES)],
                zsem))
            return carry

        lax.fori_loop(fill_ref[2 * N_EXPERTS], n_blocks, unused_block, 0)

    @pl.when(pl.program_id(0) == 0)
    def _():
        zbuf[...] = jnp.zeros_like(zbuf)
        zero_fill(True)
        zero_fill(False)

    def rows(start):
        def body(t, carry):
            src = h_ref.at[pl.ds(pl.multiple_of(t * SUBLANES, SUBLANES), SUBLANES)]
            for k in range(TOP_K):
                s = slot_ref[0, 0, TOP_K * t + k]
                cp = pltpu.make_async_copy(
                    src, xs_ref.at[pl.ds(pl.multiple_of(s * SUBLANES, SUBLANES), SUBLANES)], sem)
                if start:
                    cp.start(priority=k % 2)
                else:
                    cp.wait()
            return carry

        lax.fori_loop(0, td, body, 0, unroll=4 if start else 8)

    rows(True)
    rows(False)


def _dispatch(h2, slots, fill, n_slots, td=1024):
    n = h2.shape[0] // SUBLANES
    td = min(td, n)
    grid_spec = pltpu.PrefetchScalarGridSpec(
        num_scalar_prefetch=1,
        grid=(n // td,),
        in_specs=[pl.BlockSpec((1, 1, TOP_K * td), lambda i, fill: (i, 0, 0), memory_space=pltpu.SMEM),
                  pl.BlockSpec((td * SUBLANES, LANES), lambda i, fill: (i, 0))],
        out_specs=pl.BlockSpec(memory_space=pl.ANY),
        scratch_shapes=[pltpu.VMEM((MOE_ROWS * SUBLANES, LANES), F32), pltpu.SemaphoreType.DMA(()),
                        pltpu.SemaphoreType.DMA(())],
    )
    return pl.pallas_call(
        _dispatch_kernel,
        grid_spec=grid_spec,
        out_shape=jax.ShapeDtypeStruct((n_slots * SUBLANES, LANES), F32),
        compiler_params=_params(1),
        name="dispatch",
    )(fill, slots.reshape(n // td, 1, TOP_K * td), h2)


def _expert_kernel(be_ref, nxt_ref, par_ref, xs_ref, wg_hbm, wu_hbm, wd_hbm, ys_ref,
                   wg_f, wu_f, wd_f, wg_s, wu_s, wd_s, sems, *, layer):
    i = pl.program_id(0)
    e = be_ref[i]
    slot = par_ref[i]
    first_of_expert = jnp.logical_or(i == 0, e != be_ref[jnp.maximum(i - 1, 0)])

    def weight_copies(expert, buf):
        return (pltpu.make_async_copy(wg_hbm.at[layer, expert], wg_f.at[buf], sems.at[buf, 0]),
                pltpu.make_async_copy(wu_hbm.at[layer, expert], wu_f.at[buf], sems.at[buf, 1]),
                pltpu.make_async_copy(wd_hbm.at[layer, expert], wd_f.at[buf], sems.at[buf, 2]))

    @pl.when(i == 0)
    def _():
        for cp in weight_copies(e, slot):
            cp.start()

    @pl.when(first_of_expert)
    def _():
        @pl.when(nxt_ref[i] != e)
        def _():
            for cp in weight_copies(nxt_ref[i], 1 - slot):
                cp.start()

        for cp in weight_copies(e, slot):
            cp.wait()
        wg_s[...] = wg_f[slot].astype(BF16)
        wu_s[...] = wu_f[slot].astype(BF16)
        wd_s[...] = wd_f[slot].astype(BF16)

    half = xs_ref.shape[0] // 2
    th = half // SUBLANES
    xs = [_load_token_tiles(xs_ref.at[pl.ds(p * half, half)], th).astype(BF16) for p in range(2)]
    gs = [jnp.dot(x, wg_s[...], preferred_element_type=F32) for x in xs]
    us = [jnp.dot(x, wu_s[...], preferred_element_type=F32) for x in xs]
    acts = [(_silu(g) * u).astype(BF16) for g, u in zip(gs, us)]
    for p in range(2):
        _store_token_tiles(ys_ref.at[pl.ds(p * half, half)],
                           jnp.dot(acts[p], wd_s[...], preferred_element_type=F32))


def _experts(xs, block_expert, next_expert, block_parity, wg, wu, wd, layer):
    n_slots = xs.shape[0] // SUBLANES
    tb = MOE_ROWS
    blk = lambda i, be, nx, par: (i, 0)
    grid_spec = pltpu.PrefetchScalarGridSpec(
        num_scalar_prefetch=3,
        grid=(n_slots // tb,),
        in_specs=[pl.BlockSpec((tb * SUBLANES, LANES), blk),
                  pl.BlockSpec(memory_space=pl.ANY), pl.BlockSpec(memory_space=pl.ANY),
                  pl.BlockSpec(memory_space=pl.ANY)],
        out_specs=pl.BlockSpec((tb * SUBLANES, LANES), blk),
        scratch_shapes=[pltpu.VMEM((2, D_MODEL, EXPERT_FF), F32), pltpu.VMEM((2, D_MODEL, EXPERT_FF), F32),
                        pltpu.VMEM((2, EXPERT_FF, D_MODEL), F32),
                        pltpu.VMEM((D_MODEL, EXPERT_FF), BF16), pltpu.VMEM((D_MODEL, EXPERT_FF), BF16),
                        pltpu.VMEM((EXPERT_FF, D_MODEL), BF16),
                        pltpu.SemaphoreType.DMA((2, 3))],
    )
    return pl.pallas_call(
        functools.partial(_expert_kernel, layer=layer),
        grid_spec=grid_spec,
        out_shape=jax.ShapeDtypeStruct((n_slots * SUBLANES, LANES), F32),
        compiler_params=_params(1),
        name="experts",
    )(block_expert, next_expert, block_parity, xs, wg, wu, wd)


def _combine_kernel(slot_ref, next_slot_ref, x1_ref, rw_ref, p_ref, ys_ref, pn_ref, wpg_ref, wpl_ref, o_ref,
                    ybuf, sems):
    i = pl.program_id(0)
    tp = x1_ref.shape[0]
    cur = i % 2

    def rows(sref, buf, start):
        def body(t, carry):
            for k in range(TOP_K):
                s = sref[0, 0, TOP_K * t + k]
                cp = pltpu.make_async_copy(
                    ys_ref.at[pl.ds(pl.multiple_of(s * SUBLANES, SUBLANES), SUBLANES)],
                    ybuf.at[buf, k, pl.ds(pl.multiple_of(t * SUBLANES, SUBLANES), SUBLANES)], sems.at[buf])
                if start:
                    cp.start(priority=k % 2)
                else:
                    cp.wait()
            return carry

        lax.fori_loop(0, tp, body, 0, unroll=4 if start else 8)

    @pl.when(i == 0)
    def _():
        rows(slot_ref, 0, True)

    @pl.when(i + 1 < pl.num_programs(0))
    def _():
        rows(next_slot_ref, 1 - cur, True)

    rows(slot_ref, cur, False)

    rw = rw_ref[...]
    x2 = (x1_ref[...] + rw[:, 0:1] * _load_token_tiles(ybuf.at[cur, 0], tp)
          + rw[:, 1:2] * _load_token_tiles(ybuf.at[cur, 1], tp))
    h3 = _rms(x2, pn_ref[...]).astype(BF16)
    pe = p_ref[0].astype(BF16)
    cw = 2 * LANES
    for c in range(D_MODEL // cw):
        cols = slice(c * cw, (c + 1) * cw)
        gate = _sigmoid(jnp.dot(h3, wpg_ref[:, cols], preferred_element_type=F32))
        o_ref[:, cols] = x2[:, cols] + gate * jnp.dot(pe, wpl_ref[:, cols], preferred_element_type=F32)


def _combine_ple(x1, rw, slots, ys, p, layer, pn, wpg, wpl, tp=256):
    n = x1.shape[0]
    d = D_MODEL
    blk = lambda i: (i, 0)
    const = lambda i: (0, 0)
    nsteps = n // tp
    slot_blocks = slots.reshape(nsteps, 1, TOP_K * tp)
    return pl.pallas_call(
        _combine_kernel,
        grid=(nsteps,),
        in_specs=[pl.BlockSpec((1, 1, TOP_K * tp), lambda i: (i, 0, 0), memory_space=pltpu.SMEM),
                  pl.BlockSpec((1, 1, TOP_K * tp), lambda i: (jnp.minimum(i + 1, nsteps - 1), 0, 0),
                               memory_space=pltpu.SMEM),
                  pl.BlockSpec((tp, d), blk), pl.BlockSpec((tp, LANES), blk),
                  pl.BlockSpec((1, tp, PLE_DIM), lambda i: (layer, i, 0)),
                  pl.BlockSpec(memory_space=pl.ANY),
                  pl.BlockSpec((1, d), const), pl.BlockSpec((d, d), const),
                  pl.BlockSpec((PLE_DIM, d), const)],
        out_specs=pl.BlockSpec((tp, d), blk),
        out_shape=jax.ShapeDtypeStruct((n, d), F32),
        scratch_shapes=[pltpu.VMEM((2, TOP_K, tp * SUBLANES, LANES), F32), pltpu.SemaphoreType.DMA((2,))],
        compiler_params=_params(1),
        name="combine_ple",
    )(slot_blocks, slot_blocks, x1, rw, p, ys, pn, wpg, wpl)


def _pack_w_in(w_in):
    o = IN_OFFSETS
    w = w_in.astype(BF16)
    pad = jnp.zeros((D_MODEL, LANES - GLA_GATE_RANK - 2 * GDN_HEADS), BF16)
    cols = [w[:, :o[6]], w[:, o[7]:o[9]], w[:, o[11]:], w[:, o[6]:o[7]], w[:, o[9]:o[11]], pad]
    return jnp.concatenate(cols, axis=1)


def _layer(x, p, layer, cos_t, sin_t, batch, seq, attn_norm, w_in, q_norm, k_norm, sinks, gla_wa2, gla_ba,
           gla_norm, gdn_conv, gdn_a_log, gdn_dt_bias, gdn_norm, w_br_a, w_br_b, w_br_c, w_o,
           ffn_norm, w_coarse, b_coarse, w_fine, b_fine, w_gate_e, w_up_e, w_down_e,
           ple_norm, w_ple_gate, w_ple):
    n = x.shape[0]
    aq, akv, bqk, bv, br, cqkv, cg, gates, small = _inproj(x, attn_norm[None], _pack_w_in(w_in))
    ya = _swa(aq, akv, cos_t, sin_t, q_norm, k_norm, sinks, batch, seq)
    yb = _gla(bqk, bv, br, small, gla_wa2, gla_ba, gla_norm, batch, seq)
    yc = _gdn(cqkv, cg, small, gdn_conv, gdn_a_log, gdn_dt_bias, gdn_norm, batch, seq)

    wr = jnp.zeros((D_MODEL, LANES), F32).at[:, :N_GROUPS].set(w_coarse)
    wr = wr.at[:, N_GROUPS:N_GROUPS + N_EXPERTS].set(w_fine)
    brow = jnp.zeros((1, LANES), F32).at[0, :N_GROUPS].set(b_coarse)
    brow = brow.at[0, N_GROUPS:N_GROUPS + N_EXPERTS].set(b_fine)
    x1, h2, ri, rw, cnt = _merge(ya, yb, yc, gates, x, w_br_a.astype(BF16), w_br_b.astype(BF16),
                                 w_br_c.astype(BF16), w_o.astype(BF16), ffn_norm[None], wr, brow)

    counts = cnt[0, :N_EXPERTS]
    padded = (counts + MOE_ROWS - 1) // MOE_ROWS * MOE_ROWS
    pad_end = jnp.cumsum(padded)
    pad_start = pad_end - padded
    slots = (pad_start[ri[:, :TOP_K]] + ri[:, TOP_K:2 * TOP_K]).astype(I32)
    n_slots = n * TOP_K + N_EXPERTS * MOE_ROWS
    block_start = jnp.arange(n_slots // MOE_ROWS, dtype=I32) * MOE_ROWS
    last_owner = jnp.max(jnp.where(padded > 0, jnp.arange(N_EXPERTS, dtype=I32), 0))
    block_expert = jnp.minimum(jnp.sum((pad_end[None, :] <= block_start[:, None]).astype(I32), axis=1),
                               last_owner).astype(I32)

    experts = jnp.arange(N_EXPERTS, dtype=I32)
    owns = padded > 0
    later = lax.cummin(jnp.where(owns, experts, N_EXPERTS)[::-1])[::-1]
    next_owner = jnp.concatenate([later[1:], jnp.full((1,), N_EXPERTS, I32)])
    next_owner = jnp.where(next_owner < N_EXPERTS, next_owner, experts)
    parity = (jnp.cumsum(owns.astype(I32)) - 1) % 2
    next_expert = next_owner[block_expert].astype(I32)
    block_parity = jnp.maximum(parity[block_expert], 0).astype(I32)

    fill = jnp.concatenate([pad_start + counts, pad_end, pad_end[-1:] // MOE_ROWS]).astype(I32)
    xs = _dispatch(h2, slots, fill, n_slots)
    ys = _experts(xs, block_expert, next_expert, block_parity, w_gate_e, w_up_e, w_down_e, layer)
    return _combine_ple(x1, rw, slots, ys, p, layer, ple_norm[None], w_ple_gate.astype(BF16),
                        w_ple.astype(BF16))


def kernel(x, p, positions, attn_norm, w_in, q_norm, k_norm, sinks, gla_wa2, gla_ba, gla_norm, gdn_conv, gdn_a_log, gdn_dt_bias, gdn_norm, w_br_a, w_br_b, w_br_c, w_o, ffn_norm, w_coarse, b_coarse, w_fine, b_fine, w_gate_e, w_up_e, w_down_e, ple_norm, w_ple_gate, w_ple):
    batch, seq, d = x.shape
    n = batch * seq
    depth = p.shape[0]
    cos_t, sin_t = _rope_tables(positions)
    xf = x.reshape(n, d)
    pf = p.reshape(depth, n, p.shape[-1])
    per_layer = (attn_norm, w_in, q_norm, k_norm, sinks, gla_wa2, gla_ba, gla_norm, gdn_conv, gdn_a_log,
                 gdn_dt_bias, gdn_norm, w_br_a, w_br_b, w_br_c, w_o, ffn_norm, w_coarse, b_coarse,
                 w_fine, b_fine, w_gate_e, w_up_e, w_down_e, ple_norm, w_ple_gate, w_ple)
    stacked = (w_gate_e, w_up_e, w_down_e)
    for i in range(depth):
        xf = _layer(xf, pf, i, cos_t, sin_t, batch, seq,
                    *[a if any(a is s for s in stacked) else a[i] for a in per_layer])
    return xf.reshape(batch, seq, d)
```

```python
import functools
import math

import numpy as np
import jax
import jax.numpy as jnp
from jax import lax
from jax.experimental import pallas as pl
from jax.experimental.pallas import tpu as pltpu

F32 = jnp.float32
BF16 = jnp.bfloat16
I32 = jnp.int32

D_MODEL = 1024
PLE_DIM = 256
NORM_EPS = 1e-6
MASK_VALUE = -1e30

SWA_Q_HEADS = 8
SWA_KV_HEADS = 2
SWA_HEAD_DIM = 64
SWA_WINDOW = 128
ROT_DIM = SWA_HEAD_DIM // 4
ROPE_THETA = 500000.0

GLA_HEADS = 4
GLA_DK = 64
GLA_DV = 128
GLA_GATE_RANK = 16
GLA_GATE_NORM = 16.0
CHUNK = 64

GDN_HEADS = 4
GDN_DK = 128
GDN_DV = 128
GDN_CONV = 4

N_GROUPS = 4
EXPERTS_PER_GROUP = 8
N_EXPERTS = N_GROUPS * EXPERTS_PER_GROUP
EXPERT_FF = 512
TOP_K = 2

SWA_Q = SWA_Q_HEADS * SWA_HEAD_DIM
SWA_KV = SWA_KV_HEADS * SWA_HEAD_DIM
GLA_K = GLA_HEADS * GLA_DK
GLA_V = GLA_HEADS * GLA_DV
GDN_K = GDN_HEADS * GDN_DK
GDN_V = GDN_HEADS * GDN_DV
GDN_QKV = 2 * GDN_K + GDN_V
IN_SPLITS = (SWA_Q, SWA_KV, SWA_KV, GLA_K, GLA_K, GLA_V, GLA_GATE_RANK, GLA_V,
             GDN_QKV, GDN_HEADS, GDN_HEADS, GDN_V, 3 * D_MODEL)
IN_OFFSETS = tuple(int(o) for o in np.cumsum((0,) + IN_SPLITS))

LANES = 128
SMALL_LR = 0
SMALL_BETA = GLA_GATE_RANK
SMALL_A = GLA_GATE_RANK + GDN_HEADS

OUT_WIDTHS = (SWA_Q, 2 * SWA_KV, 2 * GLA_K, GLA_V, GLA_V, GDN_QKV, GDN_V, 3 * D_MODEL, LANES)

MOE_ROWS = 256
VMEM_LIMIT = 56 * 1024 * 1024


def _params(n_axes):
    return pltpu.CompilerParams(dimension_semantics=("arbitrary",) * n_axes,
                                vmem_limit_bytes=VMEM_LIMIT)


def _dot(a, b):
    return jnp.dot(a.astype(BF16), b.astype(BF16), preferred_element_type=F32)


def _dot_nt(a, b):
    return lax.dot_general(a.astype(BF16), b.astype(BF16), (((1,), (1,)), ((), ())),
                           preferred_element_type=F32)


def _dot_tn(a, b):
    return lax.dot_general(a.astype(BF16), b.astype(BF16), (((0,), (0,)), ((), ())),
                           preferred_element_type=F32)


def _split2(x):
    hi = x.astype(BF16)
    lo = (x - hi.astype(F32)).astype(BF16)
    return hi, lo


def _dot_exact_lhs(a, x):
    hi = x.astype(BF16)
    r = x - hi.astype(F32)
    mid = r.astype(BF16)
    lo = (r - mid.astype(F32)).astype(BF16)
    a = a.astype(BF16)
    return (jnp.dot(a, hi, preferred_element_type=F32) + jnp.dot(a, mid, preferred_element_type=F32)
            + jnp.dot(a, lo, preferred_element_type=F32))


def _dot_hi(a, b):
    ah, al = _split2(a)
    bh, bl = _split2(b)
    return (jnp.dot(ah, bh, preferred_element_type=F32) + jnp.dot(ah, bl, preferred_element_type=F32)
            + jnp.dot(al, bh, preferred_element_type=F32))


def _rms(x, g):
    return x * lax.rsqrt(jnp.mean(x * x, axis=-1, keepdims=True) + NORM_EPS) * g


def _sigmoid(x):
    return 0.5 * jnp.tanh(0.5 * x) + 0.5


def _silu(x):
    return x * _sigmoid(x)


def _softplus(x):
    return jnp.maximum(x, 0.0) + jnp.log(1.0 + jnp.exp(-jnp.abs(x)))


def _log_sigmoid(x):
    return -_softplus(-x)


def _iota(shape, axis):
    return lax.broadcasted_iota(I32, shape, axis)


SUBLANES = 8
ROW_TILES = D_MODEL // LANES


def _store_token_tiles(ref, x):
    t = x.shape[0]
    for s in range(ROW_TILES):
        ref[pl.ds(s, t, stride=ROW_TILES), :] = x[:, s * LANES:(s + 1) * LANES]


def _load_token_tiles(ref, t):
    return jnp.concatenate([ref[pl.ds(s, t, stride=ROW_TILES), :] for s in range(ROW_TILES)], axis=-1)


def _chunk_tril(n, strict=False):
    r = _iota((n, n), 0)
    c = _iota((n, n), 1)
    same = (r // CHUNK) == (c // CHUNK)
    return same & ((c < r) if strict else (c <= r))


def _inproj_kernel(x_ref, g_ref, w_ref, *out_refs):
    h = _rms(x_ref[...], g_ref[...]).astype(BF16)
    off = 0
    for o_ref in out_refs:
        wd = o_ref.shape[-1]
        o_ref[...] = jnp.dot(h, w_ref[:, off:off + wd], preferred_element_type=F32)
        off += wd


def _inproj(x, g, w, tm=256):
    n = x.shape[0]
    nc = w.shape[1]
    return pl.pallas_call(
        _inproj_kernel,
        grid=(n // tm,),
        in_specs=[pl.BlockSpec((tm, D_MODEL), lambda i: (i, 0)),
                  pl.BlockSpec((1, D_MODEL), lambda i: (0, 0)),
                  pl.BlockSpec((D_MODEL, nc), lambda i: (0, 0), pipeline_mode=pl.Buffered(1))],
        out_specs=[pl.BlockSpec((tm, wd), lambda i: (i, 0)) for wd in OUT_WIDTHS],
        out_shape=[jax.ShapeDtypeStruct((n, wd), F32) for wd in OUT_WIDTHS],
        compiler_params=_params(1),
        name="inproj",
    )(x, g, w)


def _rope_kernel(pos_ref, invf_ref, cos_ref, sin_ref):
    ang = pos_ref[...].astype(F32) * invf_ref[...]
    d = _iota(ang.shape, 1) % SWA_HEAD_DIM
    half = ROT_DIM // 2
    cos_ref[...] = jnp.where(d < ROT_DIM, jnp.cos(ang), 1.0)
    s = jnp.sin(ang)
    sin_ref[...] = jnp.where(d < half, -s, jnp.where(d < ROT_DIM, s, 0.0))


def _rope_tables(positions, tm=1024):
    n = positions.size
    tm = min(tm, n)
    inv_freq = 1.0 / (ROPE_THETA ** (jnp.arange(0, ROT_DIM, 2, dtype=F32) / ROT_DIM))
    lane_freq = jnp.tile(jnp.concatenate([inv_freq, inv_freq, jnp.zeros(SWA_HEAD_DIM - ROT_DIM, F32)]),
                         LANES // SWA_HEAD_DIM)[None]
    return pl.pallas_call(
        _rope_kernel,
        grid=(n // tm,),
        in_specs=[pl.BlockSpec((tm, 1), lambda i: (i, 0)),
                  pl.BlockSpec((1, LANES), lambda i: (0, 0))],
        out_specs=[pl.BlockSpec((tm, LANES), lambda i: (i, 0))] * 2,
        out_shape=[jax.ShapeDtypeStruct((n, LANES), F32)] * 2,
        compiler_params=_params(1),
        name="rope_tables",
    )(positions.reshape(n, 1), lane_freq)


def _dot_exact_rhs(x, b):
    hi, lo = _split2(x)
    return jnp.dot(hi, b, preferred_element_type=F32) + jnp.dot(lo, b, preferred_element_type=F32)


def _rope_matrices():
    m = _iota((LANES, LANES), 0)
    l = _iota((LANES, LANES), 1)
    d = l % SWA_HEAD_DIM
    half = ROT_DIM // 2
    same_head = (m // SWA_HEAD_DIM) == (l // SWA_HEAD_DIM)
    partner = ((d < half) & (m == l + half)) | ((d >= half) & (d < ROT_DIM) & (m == l - half))
    return jnp.where(same_head, 1.0, 0.0).astype(BF16), jnp.where(partner, 1.0, 0.0).astype(BF16)


def _norm_rope(x, gain, cos, sin, same_head, partner):
    ms = _dot_exact_rhs(x * x, same_head) * (1.0 / SWA_HEAD_DIM)
    xn = x * lax.rsqrt(ms + NORM_EPS) * gain
    return xn * cos + _dot_exact_rhs(xn, partner) * sin


def _swa_kernel(sink_ref, q_ref, kv_ref, cos_ref, sin_ref, qn_ref, kn_ref, o_ref, kprev_ref, vprev_ref):
    j = pl.program_id(1)
    w = SWA_WINDOW
    hd = SWA_HEAD_DIM
    group = SWA_Q_HEADS // SWA_KV_HEADS

    @pl.when(j == 0)
    def _():
        kprev_ref[...] = jnp.zeros_like(kprev_ref)
        vprev_ref[...] = jnp.zeros_like(vprev_ref)

    cos = cos_ref[...]
    sin = sin_ref[...]
    same_head, partner = _rope_matrices()
    kc = _norm_rope(kv_ref[:, :SWA_KV], kn_ref[...], cos, sin, same_head, partner).astype(BF16)
    vc = kv_ref[:, SWA_KV:].astype(BF16)
    k_all = jnp.concatenate([kprev_ref[...], kc], axis=0)
    v_all = jnp.concatenate([vprev_ref[...], vc], axis=0)
    kprev_ref[...] = kc
    vprev_ref[...] = vc

    rows = group * w
    qi = _iota((rows, 2 * w), 0) % w
    kj = _iota((rows, 2 * w), 1)
    first_key = jnp.where(j > 0, 0, w)
    mask = (kj > qi) & (kj <= qi + w) & (kj >= first_key)
    head_of_row = _iota((rows, 1), 0) // w
    qrs = [_norm_rope(q_ref[:, c * LANES:(c + 1) * LANES], qn_ref[...], cos, sin, same_head, partner)
           for c in range(SWA_Q // LANES)]
    per_block = LANES // hd
    qgs, sinks = [], []
    for g in range(SWA_KV_HEADS):
        heads = range(g * group, (g + 1) * group)
        qgs.append(jnp.concatenate(
            [qrs[h // per_block][:, (h % per_block) * hd:(h % per_block + 1) * hd] for h in heads], axis=0))
        sk = jnp.full((rows, 1), sink_ref[g * group], F32)
        for i in range(1, group):
            sk = jnp.where(head_of_row == i, sink_ref[g * group + i], sk)
        sinks.append(sk)
    ss = [jnp.where(mask, _dot_nt(qgs[g], k_all[:, g * hd:(g + 1) * hd]) * (hd ** -0.5), MASK_VALUE)
          for g in range(SWA_KV_HEADS)]
    ms = [jnp.maximum(jnp.max(s, axis=-1, keepdims=True), sk) for s, sk in zip(ss, sinks)]
    es = [jnp.exp(s - m).astype(BF16) for s, m in zip(ss, ms)]
    ones = jnp.ones((2 * w, hd), BF16)
    dens = [jnp.dot(e, ones, preferred_element_type=F32) + jnp.exp(sk - m) for e, sk, m in zip(es, sinks, ms)]
    for g in range(SWA_KV_HEADS):
        og = jnp.dot(es[g], v_all[:, g * hd:(g + 1) * hd], preferred_element_type=F32) / dens[g]
        for i in range(group):
            h = g * group + i
            o_ref[:, h * hd:(h + 1) * hd] = og[i * w:(i + 1) * w]


def _swa(aq, akv, cos_t, sin_t, q_norm, k_norm, sinks, batch, seq):
    n = aq.shape[0]
    w = SWA_WINDOW
    nq = seq // w
    cur = lambda b, j: (b * nq + j, 0)
    rep = LANES // SWA_HEAD_DIM
    return pl.pallas_call(
        _swa_kernel,
        grid=(batch, nq),
        in_specs=[pl.BlockSpec(memory_space=pltpu.SMEM),
                  pl.BlockSpec((w, SWA_Q), cur),
                  pl.BlockSpec((w, 2 * SWA_KV), cur),
                  pl.BlockSpec((w, LANES), cur),
                  pl.BlockSpec((w, LANES), cur),
                  pl.BlockSpec((1, LANES), lambda b, j: (0, 0)),
                  pl.BlockSpec((1, LANES), lambda b, j: (0, 0))],
        out_specs=pl.BlockSpec((w, SWA_Q), cur),
        out_shape=jax.ShapeDtypeStruct((n, SWA_Q), F32),
        scratch_shapes=[pltpu.VMEM((w, SWA_KV), BF16), pltpu.VMEM((w, SWA_KV), BF16)],
        compiler_params=_params(2),
        name="swa",
    )(sinks, aq, akv, cos_t, sin_t, jnp.tile(q_norm, rep)[None], jnp.tile(k_norm, rep)[None])


def _gla_kernel(qk_ref, v_ref, r_ref, sm_ref, wa_ref, ba_ref, gn_ref, o_ref, state_ref):
    @pl.when(pl.program_id(1) == 0)
    def _():
        state_ref[...] = jnp.zeros_like(state_ref)

    tc = qk_ref.shape[0]
    la = _log_sigmoid(_dot(sm_ref[...], wa_ref[...]) + ba_ref[...]) * (1.0 / GLA_GATE_NORM)
    tri = jnp.where(_chunk_tril(tc), 1.0, 0.0).astype(BF16)
    b = _dot_exact_lhs(tri, la)
    causal = _iota((CHUNK, CHUNK), 1) <= _iota((CHUNK, CHUNK), 0)
    n_chunks = tc // CHUNK
    qes, dcols, intra, upd = [], [], {}, {}
    for c in range(n_chunks):
        rows = slice(c * CHUNK, (c + 1) * CHUNK)
        bc = b[rows]
        b_last = bc[CHUNK - 1:CHUNK]
        k = qk_ref[rows, GLA_K:]
        qe = qk_ref[rows, :GLA_K] * (GLA_DK ** -0.5) * jnp.exp(bc)
        ke = k * jnp.exp(-bc)
        kd = k * jnp.exp(b_last - bc)
        qes.append(qe)
        dcols.append(jnp.transpose(jnp.broadcast_to(jnp.exp(b_last), (GLA_DV, GLA_K))))
        for h in range(GLA_HEADS):
            ks = slice(h * GLA_DK, (h + 1) * GLA_DK)
            vh = v_ref[rows, h * GLA_DV:(h + 1) * GLA_DV]
            att = jnp.where(causal, _dot_nt(qe[:, ks], ke[:, ks]), 0.0)
            intra[c, h] = _dot(att, vh)
            upd[c, h] = _dot_tn(kd[:, ks], vh)
    for h in range(GLA_HEADS):
        ks = slice(h * GLA_DK, (h + 1) * GLA_DK)
        vs = slice(h * GLA_DV, (h + 1) * GLA_DV)
        st = state_ref[h]
        for c in range(n_chunks):
            rows = slice(c * CHUNK, (c + 1) * CHUNK)
            o = intra[c, h] + _dot(qes[c][:, ks], st)
            st = dcols[c][ks] * st + upd[c, h]
            o_ref[rows, vs] = _rms(o, gn_ref[...]) * _silu(r_ref[rows, vs])
        state_ref[h] = st


def _gla(bqk, bv, br, small, wa2, ba, gn, batch, seq, tc=256):
    n = bqk.shape[0]
    nt = seq // tc
    blk = lambda b, j: (b * nt + j, 0)
    const = lambda b, j: (0, 0)
    wa_pad = jnp.zeros((LANES, GLA_K), F32).at[SMALL_LR:SMALL_LR + GLA_GATE_RANK].set(wa2)
    return pl.pallas_call(
        _gla_kernel,
        grid=(batch, nt),
        in_specs=[pl.BlockSpec((tc, 2 * GLA_K), blk),
                  pl.BlockSpec((tc, GLA_V), blk),
                  pl.BlockSpec((tc, GLA_V), blk),
                  pl.BlockSpec((tc, LANES), blk),
                  pl.BlockSpec((LANES, GLA_K), const),
                  pl.BlockSpec((1, GLA_K), const),
                  pl.BlockSpec((1, GLA_DV), const)],
        out_specs=pl.BlockSpec((tc, GLA_V), blk),
        out_shape=jax.ShapeDtypeStruct((n, GLA_V), F32),
        scratch_shapes=[pltpu.VMEM((GLA_HEADS, GLA_DK, GLA_DV), F32)],
        compiler_params=_params(2),
        name="gla",
    )(bqk, bv, br, small, wa_pad, ba[None], gn[None])


def _l2(x):
    return x * lax.rsqrt(jnp.sum(x * x, axis=-1, keepdims=True) + NORM_EPS)


def _gdn_kernel(x_ref, xp_ref, g_ref, sm_ref, cw_ref, al_ref, dt_ref, gn_ref, o_ref, state_ref):
    first = pl.program_id(1) == 0

    @pl.when(first)
    def _():
        state_ref[...] = jnp.zeros_like(state_ref)

    tc = x_ref.shape[0]
    halo = xp_ref.shape[0]
    x = x_ref[...]
    xprev = jnp.where(first, 0.0, xp_ref[...])
    sm = sm_ref[...]
    row = _iota((halo, 1), 0)
    conv = cw_ref[GDN_CONV - 1:GDN_CONV] * x
    for t in range(1, GDN_CONV):
        rolled = pltpu.roll(x, t, 0)
        head = jnp.where(row < t, pltpu.roll(xprev, t, 0), rolled[:halo])
        shifted = jnp.concatenate([head, rolled[halo:]], axis=0)
        conv = conv + cw_ref[GDN_CONV - 1 - t:GDN_CONV - t] * shifted
    qkv = _silu(conv)

    beta_all = _sigmoid(sm)
    g_all = -jnp.exp(al_ref[...]) * _softplus(sm + dt_ref[...])
    tri = jnp.where(_chunk_tril(tc), 1.0, 0.0).astype(BF16)
    big_g = _dot_exact_lhs(tri, g_all)
    g_rows = jnp.transpose(big_g)

    r = _iota((CHUNK, CHUNK), 0)
    cidx = _iota((CHUNK, CHUNK), 1)
    causal = cidx <= r
    strict = cidx < r
    eye = jnp.where(cidx == r, 1.0, 0.0)

    units = [(c, h) for c in range(tc // CHUNK) for h in range(GDN_HEADS)]
    qs, ks, egs, g_lasts, gcs, decays, lows, rhss = [], [], [], [], [], [], [], []
    for c, h in units:
        rows = slice(c * CHUNK, (c + 1) * CHUNK)
        q = _l2(qkv[rows, h * GDN_DK:(h + 1) * GDN_DK]) * (GDN_DK ** -0.5)
        k = _l2(qkv[rows, GDN_K + h * GDN_DK:GDN_K + (h + 1) * GDN_DK])
        v = qkv[rows, 2 * GDN_K + h * GDN_DV:2 * GDN_K + (h + 1) * GDN_DV]
        beta = beta_all[rows, SMALL_BETA + h:SMALL_BETA + h + 1]
        gc = big_g[rows, SMALL_A + h:SMALL_A + h + 1]
        g_row = g_rows[SMALL_A + h:SMALL_A + h + 1, c * CHUNK:(c + 1) * CHUNK]
        decay = jnp.where(causal, jnp.exp(jnp.where(causal, gc - g_row, 0.0)), 0.0)
        eg = jnp.exp(gc)
        qs.append(q)
        ks.append(k)
        egs.append(eg)
        gcs.append(gc)
        g_lasts.append(gc[CHUNK - 1:CHUNK])
        decays.append(decay)
        lows.append(jnp.where(strict, beta * _dot_nt(k, k) * decay, 0.0))
        rhss.append(jnp.concatenate([v * beta, k * (beta * eg)], axis=-1))
    invs = [eye - low for low in lows]
    pws = lows
    for _ in range(int(math.log2(CHUNK)) - 1):
        pws = [_dot(pw, pw) for pw in pws]
        invs = [inv + _dot(inv, pw) for inv, pw in zip(invs, pws)]
    sols = [_dot(inv, rhs) for inv, rhs in zip(invs, rhss)]
    atts = [_dot_nt(q, k) * decay for q, k, decay in zip(qs, ks, decays)]
    k_decs = [k * jnp.exp(gl - gc) for k, gl, gc in zip(ks, g_lasts, gcs)]
    kws = [_dot_tn(k_dec, sol) for k_dec, sol in zip(k_decs, sols)]
    aws = [_dot(att, sol) for att, sol in zip(atts, sols)]

    for i, (c, h) in enumerate(units):
        rows = slice(c * CHUNK, (c + 1) * CHUNK)
        st = state_ref[h]
        lhs = jnp.concatenate([kws[i][:, GDN_DV:], qs[i] * egs[i] - aws[i][:, GDN_DV:]], axis=0)
        prod = _dot(lhs, st)
        o = prod[GDN_DK:] + aws[i][:, :GDN_DV]
        state_ref[h] = jnp.exp(g_lasts[i]) * st - prod[:GDN_DK] + kws[i][:, :GDN_DV]
        vs = slice(h * GDN_DV, (h + 1) * GDN_DV)
        o_ref[rows, vs] = _rms(o, gn_ref[...]) * _silu(g_ref[rows, vs])


def _gdn(cqkv, cg, small, conv_w, a_log, dt_bias, gn, batch, seq, tc=256, halo=8):
    n = cqkv.shape[0]
    nt = seq // tc
    blk = lambda b, j: (b * nt + j, 0)
    const = lambda b, j: (0, 0)
    prev = lambda b, j: (jnp.maximum((b * nt + j) * (tc // halo) - 1, 0), 0)
    al_row = jnp.zeros((1, LANES), F32).at[0, SMALL_A:SMALL_A + GDN_HEADS].set(a_log)
    dt_row = jnp.zeros((1, LANES), F32).at[0, SMALL_A:SMALL_A + GDN_HEADS].set(dt_bias)
    return pl.pallas_call(
        _gdn_kernel,
        grid=(batch, nt),
        in_specs=[pl.BlockSpec((tc, GDN_QKV), blk),
                  pl.BlockSpec((halo, GDN_QKV), prev),
                  pl.BlockSpec((tc, GDN_V), blk),
                  pl.BlockSpec((tc, LANES), blk),
                  pl.BlockSpec((GDN_CONV, GDN_QKV), const),
                  pl.BlockSpec((1, LANES), const),
                  pl.BlockSpec((1, LANES), const),
                  pl.BlockSpec((1, GDN_DV), const)],
        out_specs=pl.BlockSpec((tc, GDN_V), blk),
        out_shape=jax.ShapeDtypeStruct((n, GDN_V), F32),
        scratch_shapes=[pltpu.VMEM((GDN_HEADS, GDN_DK, GDN_DV), F32)],
        compiler_params=_params(2),
        name="gdn",
    )(cqkv, cqkv, cg, small, conv_w, al_row, dt_row, gn[None])


def _first_argmax(x, lane):
    m = jnp.max(x, axis=-1, keepdims=True)
    idx = jnp.min(jnp.where(x == m, lane.astype(F32), float(LANES)), axis=-1, keepdims=True)
    return m, idx.astype(I32)


def _merge_kernel(ya_ref, yb_ref, yc_ref, gt_ref, x_ref, wa_ref, wb_ref, wc_ref, wo_ref, fn_ref,
                  wr_ref, br_ref, x1_ref, h2_ref, ri_ref, rw_ref, cnt_ref, carry_ref):
    @pl.when(pl.program_id(0) == 0)
    def _():
        carry_ref[...] = jnp.zeros_like(carry_ref)

    d = D_MODEL
    ya = ya_ref[...].astype(BF16)
    yb = yb_ref[...].astype(BF16)
    yc = yc_ref[...].astype(BF16)
    cw = 2 * LANES
    merged = []
    for c in range(d // cw):
        lo = c * cw
        merged.append((
            _sigmoid(gt_ref[:, lo:lo + cw]) * jnp.dot(ya, wa_ref[:, lo:lo + cw], preferred_element_type=F32)
            + _sigmoid(gt_ref[:, d + lo:d + lo + cw]) * jnp.dot(yb, wb_ref[:, lo:lo + cw], preferred_element_type=F32)
            + _sigmoid(gt_ref[:, 2 * d + lo:2 * d + lo + cw]) * jnp.dot(yc, wc_ref[:, lo:lo + cw], preferred_element_type=F32)
        ).astype(BF16))
    merged = jnp.concatenate(merged, axis=-1)
    x1 = x_ref[...] + jnp.dot(merged, wo_ref[...], preferred_element_type=F32)
    x1_ref[...] = x1
    h2 = _rms(x1, fn_ref[...])
    _store_token_tiles(h2_ref, h2)

    logits = _dot_hi(h2, wr_ref[...]) + br_ref[...]
    tm = logits.shape[0]
    lane = _iota((tm, LANES), 1)
    neg = -jnp.inf
    is_c = lane < N_GROUPS
    cm, g_idx = _first_argmax(jnp.where(is_c, logits, neg), lane)
    g_prob = 1.0 / jnp.sum(jnp.where(is_c, jnp.exp(logits - cm), 0.0), axis=-1, keepdims=True)
    sel = (lane >= N_GROUPS) & (((lane - N_GROUPS) // EXPERTS_PER_GROUP) == g_idx)
    fm = jnp.max(jnp.where(sel, logits, neg), axis=-1, keepdims=True)
    ef = jnp.where(sel, jnp.exp(logits - fm), 0.0)
    p1, i1 = _first_argmax(jnp.where(sel, ef, neg), lane)
    p2, i2 = _first_argmax(jnp.where(sel & (lane != i1), ef, neg), lane)
    w1 = g_prob * p1 / (p1 + p2)
    w2 = g_prob * p2 / (p1 + p2)
    e1 = i1 - N_GROUPS
    e2 = i2 - N_GROUPS

    oh = jnp.where((lane == e1) | (lane == e2 + N_EXPERTS), 1.0, 0.0)
    stril = jnp.where(_iota((tm, tm), 1) < _iota((tm, tm), 0), 1.0, 0.0).astype(BF16)
    before = jnp.dot(stril, oh.astype(BF16), preferred_element_type=F32)
    tot = jnp.sum(oh, axis=0, keepdims=True)
    tot_first = jnp.where(lane[:1] < N_EXPERTS, tot, 0.0)
    carry = carry_ref[...]
    base = carry + pltpu.roll(carry + tot_first, N_EXPERTS, 1)
    ranks = oh * (before + base)
    rank1 = jnp.sum(jnp.where(lane < N_EXPERTS, ranks, 0.0), axis=-1, keepdims=True)
    rank2 = jnp.sum(jnp.where(lane < N_EXPERTS, 0.0, ranks), axis=-1, keepdims=True)
    new_carry = carry + tot_first + pltpu.roll(tot - tot_first, LANES - N_EXPERTS, 1)
    carry_ref[...] = new_carry
    cnt_ref[...] = jnp.broadcast_to(new_carry, cnt_ref.shape).astype(I32)

    ri_ref[...] = jnp.where(lane == 0, e1, jnp.where(lane == 1, e2, jnp.where(
        lane == 2, rank1.astype(I32), jnp.where(lane == 3, rank2.astype(I32), 0))))
    rw_ref[...] = jnp.where(lane == 0, w1, jnp.where(lane == 1, w2, 0.0))


def _merge(ya, yb, yc, gates, x, wa, wb, wc, wo, fn, wr, br, tm=256):
    n = x.shape[0]
    d = D_MODEL
    blk = lambda i: (i, 0)
    const = lambda i: (0, 0)
    return pl.pallas_call(
        _merge_kernel,
        grid=(n // tm,),
        in_specs=[pl.BlockSpec((tm, SWA_Q), blk), pl.BlockSpec((tm, GLA_V), blk),
                  pl.BlockSpec((tm, GDN_V), blk), pl.BlockSpec((tm, 3 * d), blk),
                  pl.BlockSpec((tm, d), blk),
                  pl.BlockSpec((SWA_Q, d), const), pl.BlockSpec((GLA_V, d), const),
                  pl.BlockSpec((GDN_V, d), const), pl.BlockSpec((d, d), const),
                  pl.BlockSpec((1, d), const), pl.BlockSpec((d, LANES), const),
                  pl.BlockSpec((1, LANES), const)],
        out_specs=[pl.BlockSpec((tm, d), blk), pl.BlockSpec((tm * SUBLANES, LANES), blk),
                   pl.BlockSpec((tm, LANES), blk), pl.BlockSpec((tm, LANES), blk),
                   pl.BlockSpec((8, LANES), const)],
        out_shape=[jax.ShapeDtypeStruct((n, d), F32), jax.ShapeDtypeStruct((n * SUBLANES, LANES), F32),
                   jax.ShapeDtypeStruct((n, LANES), I32), jax.ShapeDtypeStruct((n, LANES), F32),
                   jax.ShapeDtypeStruct((8, LANES), I32)],
        scratch_shapes=[pltpu.VMEM((1, LANES), F32)],
        compiler_params=_params(1),
        name="merge_route",
    )(ya, yb, yc, gates, x, wa, wb, wc, wo, fn, wr, br)


def _dispatch_kernel(fill_ref, slot_ref, h_ref, xs_ref, zbuf, sem, zsem):
    td = h_ref.shape[0] // SUBLANES
    n_blocks = xs_ref.shape[0] // (MOE_ROWS * SUBLANES)

    def zero_fill(start):
        def run(cp):
            if start:
                cp.start()
            else:
                cp.wait()

        def pad_rows(e, carry):
            lo = fill_ref[e]
            n_pad = fill_ref[N_EXPERTS + e] - lo
            bit = MOE_ROWS // 2
            while bit >= 1:
                off = lo + (n_pad // (2 * bit)) * (2 * bit)

                @pl.when((n_pad & bit) != 0)
                def _(off=off, bit=bit):
                    run(pltpu.make_async_copy(
                        zbuf.at[pl.ds(0, bit * SUBLANES)],
                        xs_ref.at[pl.ds(pl.multiple_of(off * SUBLANES, SUBLANES), bit * SUBLANES)], zsem))

                bit //= 2
            return carry

        lax.fori_loop(0, N_EXPERTS, pad_rows, 0)

        def unused_block(b, carry):
            run(pltpu.make_async_copy(
                zbuf, xs_ref.at[pl.ds(pl.multiple_of(b * (MOE_ROWS * SUBLANES), SUBLANES), MOE_ROWS * SUBLANES)],
                zsem))
            return carry

        lax.fori_loop(fill_ref[2 * N_EXPERTS], n_blocks, unused_block, 0)

    @pl.when(pl.program_id(0) == 0)
    def _():
        zbuf[...] = jnp.zeros_like(zbuf)
        zero_fill(True)
        zero_fill(False)

    def rows(start):
        def body(t, carry):
            src = h_ref.at[pl.ds(pl.multiple_of(t * SUBLANES, SUBLANES), SUBLANES)]
            for k in range(TOP_K):
                s = slot_ref[0, 0, TOP_K * t + k]
                cp = pltpu.make_async_copy(
                    src, xs_ref.at[pl.ds(pl.multiple_of(s * SUBLANES, SUBLANES), SUBLANES)], sem)
                if start:
                    cp.start(priority=k % 2)
                else:
                    cp.wait()
            return carry

        lax.fori_loop(0, td, body, 0, unroll=4 if start else 8)

    rows(True)
    rows(False)


def _dispatch(h2, slots, fill, n_slots, td=1024):
    n = h2.shape[0] // SUBLANES
    td = min(td, n)
    grid_spec = pltpu.PrefetchScalarGridSpec(
        num_scalar_prefetch=1,
        grid=(n // td,),
        in_specs=[pl.BlockSpec((1, 1, TOP_K * td), lambda i, fill: (i, 0, 0), memory_space=pltpu.SMEM),
                  pl.BlockSpec((td * SUBLANES, LANES), lambda i, fill: (i, 0))],
        out_specs=pl.BlockSpec(memory_space=pl.ANY),
        scratch_shapes=[pltpu.VMEM((MOE_ROWS * SUBLANES, LANES), F32), pltpu.SemaphoreType.DMA(()),
                        pltpu.SemaphoreType.DMA(())],
    )
    return pl.pallas_call(
        _dispatch_kernel,
        grid_spec=grid_spec,
        out_shape=jax.ShapeDtypeStruct((n_slots * SUBLANES, LANES), F32),
        compiler_params=_params(1),
        name="dispatch",
    )(fill, slots.reshape(n // td, 1, TOP_K * td), h2)


def _expert_kernel(be_ref, nxt_ref, par_ref, xs_ref, wg_hbm, wu_hbm, wd_hbm, ys_ref,
                   wg_f, wu_f, wd_f, wg_s, wu_s, wd_s, sems, *, layer):
    i = pl.program_id(0)
    e = be_ref[i]
    slot = par_ref[i]
    first_of_expert = jnp.logical_or(i == 0, e != be_ref[jnp.maximum(i - 1, 0)])

    def weight_copies(expert, buf):
        return (pltpu.make_async_copy(wg_hbm.at[layer, expert], wg_f.at[buf], sems.at[buf, 0]),
                pltpu.make_async_copy(wu_hbm.at[layer, expert], wu_f.at[buf], sems.at[buf, 1]),
                pltpu.make_async_copy(wd_hbm.at[layer, expert], wd_f.at[buf], sems.at[buf, 2]))

    @pl.when(i == 0)
    def _():
        for cp in weight_copies(e, slot):
            cp.start()

    @pl.when(first_of_expert)
    def _():
        @pl.when(nxt_ref[i] != e)
        def _():
            for cp in weight_copies(nxt_ref[i], 1 - slot):
                cp.start()

        for cp in weight_copies(e, slot):
            cp.wait()
        wg_s[...] = wg_f[slot].astype(BF16)
        wu_s[...] = wu_f[slot].astype(BF16)
        wd_s[...] = wd_f[slot].astype(BF16)

    half = xs_ref.shape[0] // 2
    th = half // SUBLANES
    xs = [_load_token_tiles(xs_ref.at[pl.ds(p * half, half)], th).astype(BF16) for p in range(2)]
    gs = [jnp.dot(x, wg_s[...], preferred_element_type=F32) for x in xs]
    us = [jnp.dot(x, wu_s[...], preferred_element_type=F32) for x in xs]
    acts = [(_silu(g) * u).astype(BF16) for g, u in zip(gs, us)]
    for p in range(2):
        _store_token_tiles(ys_ref.at[pl.ds(p * half, half)],
                           jnp.dot(acts[p], wd_s[...], preferred_element_type=F32))


def _experts(xs, block_expert, next_expert, block_parity, wg, wu, wd, layer):
    n_slots = xs.shape[0] // SUBLANES
    tb = MOE_ROWS
    blk = lambda i, be, nx, par: (i, 0)
    grid_spec = pltpu.PrefetchScalarGridSpec(
        num_scalar_prefetch=3,
        grid=(n_slots // tb,),
        in_specs=[pl.BlockSpec((tb * SUBLANES, LANES), blk),
                  pl.BlockSpec(memory_space=pl.ANY), pl.BlockSpec(memory_space=pl.ANY),
                  pl.BlockSpec(memory_space=pl.ANY)],
        out_specs=pl.BlockSpec((tb * SUBLANES, LANES), blk),
        scratch_shapes=[pltpu.VMEM((2, D_MODEL, EXPERT_FF), F32), pltpu.VMEM((2, D_MODEL, EXPERT_FF), F32),
                        pltpu.VMEM((2, EXPERT_FF, D_MODEL), F32),
                        pltpu.VMEM((D_MODEL, EXPERT_FF), BF16), pltpu.VMEM((D_MODEL, EXPERT_FF), BF16),
                        pltpu.VMEM((EXPERT_FF, D_MODEL), BF16),
                        pltpu.SemaphoreType.DMA((2, 3))],
    )
    return pl.pallas_call(
        functools.partial(_expert_kernel, layer=layer),
        grid_spec=grid_spec,
        out_shape=jax.ShapeDtypeStruct((n_slots * SUBLANES, LANES), F32),
        compiler_params=_params(1),
        name="experts",
    )(block_expert, next_expert, block_parity, xs, wg, wu, wd)


def _combine_kernel(slot_ref, next_slot_ref, x1_ref, rw_ref, p_ref, ys_ref, pn_ref, wpg_ref, wpl_ref, o_ref,
                    ybuf, sems):
    i = pl.program_id(0)
    tp = x1_ref.shape[0]
    cur = i % 2

    def rows(sref, buf, start):
        def body(t, carry):
            for k in range(TOP_K):
                s = sref[0, 0, TOP_K * t + k]
                cp = pltpu.make_async_copy(
                    ys_ref.at[pl.ds(pl.multiple_of(s * SUBLANES, SUBLANES), SUBLANES)],
                    ybuf.at[buf, k, pl.ds(pl.multiple_of(t * SUBLANES, SUBLANES), SUBLANES)], sems.at[buf])
                if start:
                    cp.start(priority=k % 2)
                else:
                    cp.wait()
            return carry

        lax.fori_loop(0, tp, body, 0, unroll=4 if start else 8)

    @pl.when(i == 0)
    def _():
        rows(slot_ref, 0, True)

    rows(slot_ref, cur, False)

    pending = list(range(tp))

    def issue_next(n_tokens):
        for _ in range(n_tokens):
            t = pending.pop(0)
            for k in range(TOP_K):
                s = next_slot_ref[0, 0, TOP_K * t + k]
                pltpu.make_async_copy(
                    ys_ref.at[pl.ds(pl.multiple_of(s * SUBLANES, SUBLANES), SUBLANES)],
                    ybuf.at[1 - cur, k, pl.ds(t * SUBLANES, SUBLANES)], sems.at[1 - cur]).start(priority=k % 2)

    n_stage = D_MODEL // (2 * LANES) + 1
    rw = rw_ref[...]
    x2 = (x1_ref[...] + rw[:, 0:1] * _load_token_tiles(ybuf.at[cur, 0], tp)
          + rw[:, 1:2] * _load_token_tiles(ybuf.at[cur, 1], tp))
    issue_next(tp // n_stage)
    h3 = _rms(x2, pn_ref[...]).astype(BF16)
    pe = p_ref[0].astype(BF16)
    cw = 2 * LANES
    for c in range(D_MODEL // cw):
        cols = slice(c * cw, (c + 1) * cw)
        gate = _sigmoid(jnp.dot(h3, wpg_ref[:, cols], preferred_element_type=F32))
        o_ref[:, cols] = x2[:, cols] + gate * jnp.dot(pe, wpl_ref[:, cols], preferred_element_type=F32)
        issue_next(tp // n_stage if c + 1 < D_MODEL // cw else len(pending))

    @pl.when(i + 1 == pl.num_programs(0))
    def _():
        rows(next_slot_ref, 1 - cur, False)


def _combine_ple(x1, rw, slots, ys, p, layer, pn, wpg, wpl, tp=256):
    n = x1.shape[0]
    d = D_MODEL
    blk = lambda i: (i, 0)
    const = lambda i: (0, 0)
    nsteps = n // tp
    slot_blocks = slots.reshape(nsteps, 1, TOP_K * tp)
    return pl.pallas_call(
        _combine_kernel,
        grid=(nsteps,),
        in_specs=[pl.BlockSpec((1, 1, TOP_K * tp), lambda i: (i, 0, 0), memory_space=pltpu.SMEM),
                  pl.BlockSpec((1, 1, TOP_K * tp), lambda i: (jnp.minimum(i + 1, nsteps - 1), 0, 0),
                               memory_space=pltpu.SMEM),
                  pl.BlockSpec((tp, d), blk), pl.BlockSpec((tp, LANES), blk),
                  pl.BlockSpec((1, tp, PLE_DIM), lambda i: (layer, i, 0)),
                  pl.BlockSpec(memory_space=pl.ANY),
                  pl.BlockSpec((1, d), const), pl.BlockSpec((d, d), const),
                  pl.BlockSpec((PLE_DIM, d), const)],
        out_specs=pl.BlockSpec((tp, d), blk),
        out_shape=jax.ShapeDtypeStruct((n, d), F32),
        scratch_shapes=[pltpu.VMEM((2, TOP_K, tp * SUBLANES, LANES), F32), pltpu.SemaphoreType.DMA((2,))],
        compiler_params=_params(1),
        name="combine_ple",
    )(slot_blocks, slot_blocks, x1, rw, p, ys, pn, wpg, wpl)


def _pack_w_in(w_in):
    o = IN_OFFSETS
    w = w_in.astype(BF16)
    pad = jnp.zeros((D_MODEL, LANES - GLA_GATE_RANK - 2 * GDN_HEADS), BF16)
    cols = [w[:, :o[6]], w[:, o[7]:o[9]], w[:, o[11]:], w[:, o[6]:o[7]], w[:, o[9]:o[11]], pad]
    return jnp.concatenate(cols, axis=1)


def _layer(x, p, layer, cos_t, sin_t, batch, seq, attn_norm, w_in, q_norm, k_norm, sinks, gla_wa2, gla_ba,
           gla_norm, gdn_conv, gdn_a_log, gdn_dt_bias, gdn_norm, w_br_a, w_br_b, w_br_c, w_o,
           ffn_norm, w_coarse, b_coarse, w_fine, b_fine, w_gate_e, w_up_e, w_down_e,
           ple_norm, w_ple_gate, w_ple):
    n = x.shape[0]
    aq, akv, bqk, bv, br, cqkv, cg, gates, small = _inproj(x, attn_norm[None], _pack_w_in(w_in))
    ya = _swa(aq, akv, cos_t, sin_t, q_norm, k_norm, sinks, batch, seq)
    yb = _gla(bqk, bv, br, small, gla_wa2, gla_ba, gla_norm, batch, seq)
    yc = _gdn(cqkv, cg, small, gdn_conv, gdn_a_log, gdn_dt_bias, gdn_norm, batch, seq)

    wr = jnp.zeros((D_MODEL, LANES), F32).at[:, :N_GROUPS].set(w_coarse)
    wr = wr.at[:, N_GROUPS:N_GROUPS + N_EXPERTS].set(w_fine)
    brow = jnp.zeros((1, LANES), F32).at[0, :N_GROUPS].set(b_coarse)
    brow = brow.at[0, N_GROUPS:N_GROUPS + N_EXPERTS].set(b_fine)
    x1, h2, ri, rw, cnt = _merge(ya, yb, yc, gates, x, w_br_a.astype(BF16), w_br_b.astype(BF16),
                                 w_br_c.astype(BF16), w_o.astype(BF16), ffn_norm[None], wr, brow)

    counts = cnt[0, :N_EXPERTS]
    padded = (counts + MOE_ROWS - 1) // MOE_ROWS * MOE_ROWS
    pad_end = jnp.cumsum(padded)
    pad_start = pad_end - padded
    slots = (pad_start[ri[:, :TOP_K]] + ri[:, TOP_K:2 * TOP_K]).astype(I32)
    n_slots = n * TOP_K + N_EXPERTS * MOE_ROWS
    block_start = jnp.arange(n_slots // MOE_ROWS, dtype=I32) * MOE_ROWS
    last_owner = jnp.max(jnp.where(padded > 0, jnp.arange(N_EXPERTS, dtype=I32), 0))
    block_expert = jnp.minimum(jnp.sum((pad_end[None, :] <= block_start[:, None]).astype(I32), axis=1),
                               last_owner).astype(I32)

    experts = jnp.arange(N_EXPERTS, dtype=I32)
    owns = padded > 0
    later = lax.cummin(jnp.where(owns, experts, N_EXPERTS)[::-1])[::-1]
    next_owner = jnp.concatenate([later[1:], jnp.full((1,), N_EXPERTS, I32)])
    next_owner = jnp.where(next_owner < N_EXPERTS, next_owner, experts)
    parity = (jnp.cumsum(owns.astype(I32)) - 1) % 2
    next_expert = next_owner[block_expert].astype(I32)
    block_parity = jnp.maximum(parity[block_expert], 0).astype(I32)

    fill = jnp.concatenate([pad_start + counts, pad_end, pad_end[-1:] // MOE_ROWS]).astype(I32)
    xs = _dispatch(h2, slots, fill, n_slots)
    ys = _experts(xs, block_expert, next_expert, block_parity, w_gate_e, w_up_e, w_down_e, layer)
    return _combine_ple(x1, rw, slots, ys, p, layer, ple_norm[None], w_ple_gate.astype(BF16),
                        w_ple.astype(BF16))


def kernel(x, p, positions, attn_norm, w_in, q_norm, k_norm, sinks, gla_wa2, gla_ba, gla_norm, gdn_conv, gdn_a_log, gdn_dt_bias, gdn_norm, w_br_a, w_br_b, w_br_c, w_o, ffn_norm, w_coarse, b_coarse, w_fine, b_fine, w_gate_e, w_up_e, w_down_e, ple_norm, w_ple_gate, w_ple):
    batch, seq, d = x.shape
    n = batch * seq
    depth = p.shape[0]
    cos_t, sin_t = _rope_tables(positions)
    xf = x.reshape(n, d)
    pf = p.reshape(depth, n, p.shape[-1])
    per_layer = (attn_norm, w_in, q_norm, k_norm, sinks, gla_wa2, gla_ba, gla_norm, gdn_conv, gdn_a_log,
                 gdn_dt_bias, gdn_norm, w_br_a, w_br_b, w_br_c, w_o, ffn_norm, w_coarse, b_coarse,
                 w_fine, b_fine, w_gate_e, w_up_e, w_down_e, ple_norm, w_ple_gate, w_ple)
    stacked = (w_gate_e, w_up_e, w_down_e)
    for i in range(depth):
        xf = _layer(xf, pf, i, cos_t, sin_t, batch, seq,
                    *[a if any(a is s for s in stacked) else a[i] for a in per_layer])
    return xf.reshape(batch, seq, d)
```

```python
import functools
import math

import numpy as np
import jax
import jax.numpy as jnp
from jax import lax
from jax.experimental import pallas as pl
from jax.experimental.pallas import tpu as pltpu

F32 = jnp.float32
BF16 = jnp.bfloat16
I32 = jnp.int32

D_MODEL = 1024
PLE_DIM = 256
NORM_EPS = 1e-6
MASK_VALUE = -1e30

SWA_Q_HEADS = 8
SWA_KV_HEADS = 2
SWA_HEAD_DIM = 64
SWA_WINDOW = 128
ROT_DIM = SWA_HEAD_DIM // 4
ROPE_THETA = 500000.0

GLA_HEADS = 4
GLA_DK = 64
GLA_DV = 128
GLA_GATE_RANK = 16
GLA_GATE_NORM = 16.0
CHUNK = 64

GDN_HEADS = 4
GDN_DK = 128
GDN_DV = 128
GDN_CONV = 4

N_GROUPS = 4
EXPERTS_PER_GROUP = 8
N_EXPERTS = N_GROUPS * EXPERTS_PER_GROUP
EXPERT_FF = 512
TOP_K = 2

SWA_Q = SWA_Q_HEADS * SWA_HEAD_DIM
SWA_KV = SWA_KV_HEADS * SWA_HEAD_DIM
GLA_K = GLA_HEADS * GLA_DK
GLA_V = GLA_HEADS * GLA_DV
GDN_K = GDN_HEADS * GDN_DK
GDN_V = GDN_HEADS * GDN_DV
GDN_QKV = 2 * GDN_K + GDN_V
IN_SPLITS = (SWA_Q, SWA_KV, SWA_KV, GLA_K, GLA_K, GLA_V, GLA_GATE_RANK, GLA_V,
             GDN_QKV, GDN_HEADS, GDN_HEADS, GDN_V, 3 * D_MODEL)
IN_OFFSETS = tuple(int(o) for o in np.cumsum((0,) + IN_SPLITS))

LANES = 128
SMALL_LR = 0
SMALL_BETA = GLA_GATE_RANK
SMALL_A = GLA_GATE_RANK + GDN_HEADS

OUT_WIDTHS = (SWA_Q, 2 * SWA_KV, 2 * GLA_K, GLA_V, GLA_V, GDN_QKV, GDN_V, 3 * D_MODEL, LANES)

MOE_ROWS = 256
VMEM_LIMIT = 56 * 1024 * 1024


def _params(n_axes):
    return pltpu.CompilerParams(dimension_semantics=("arbitrary",) * n_axes,
                                vmem_limit_bytes=VMEM_LIMIT)


def _dot(a, b):
    return jnp.dot(a.astype(BF16), b.astype(BF16), preferred_element_type=F32)


def _dot_nt(a, b):
    return lax.dot_general(a.astype(BF16), b.astype(BF16), (((1,), (1,)), ((), ())),
                           preferred_element_type=F32)


def _dot_tn(a, b):
    return lax.dot_general(a.astype(BF16), b.astype(BF16), (((0,), (0,)), ((), ())),
                           preferred_element_type=F32)


def _split2(x):
    hi = x.astype(BF16)
    lo = (x - hi.astype(F32)).astype(BF16)
    return hi, lo


def _dot_exact_lhs(a, x):
    hi = x.astype(BF16)
    r = x - hi.astype(F32)
    mid = r.astype(BF16)
    lo = (r - mid.astype(F32)).astype(BF16)
    a = a.astype(BF16)
    return (jnp.dot(a, hi, preferred_element_type=F32) + jnp.dot(a, mid, preferred_element_type=F32)
            + jnp.dot(a, lo, preferred_element_type=F32))


def _dot_hi(a, b):
    ah, al = _split2(a)
    bh, bl = _split2(b)
    return (jnp.dot(ah, bh, preferred_element_type=F32) + jnp.dot(ah, bl, preferred_element_type=F32)
            + jnp.dot(al, bh, preferred_element_type=F32))


def _rms(x, g):
    return x * lax.rsqrt(jnp.mean(x * x, axis=-1, keepdims=True) + NORM_EPS) * g


def _sigmoid(x):
    return 0.5 * jnp.tanh(0.5 * x) + 0.5


def _silu(x):
    return x * _sigmoid(x)


def _softplus(x):
    return jnp.maximum(x, 0.0) + jnp.log(1.0 + jnp.exp(-jnp.abs(x)))


def _log_sigmoid(x):
    return -_softplus(-x)


def _iota(shape, axis):
    return lax.broadcasted_iota(I32, shape, axis)


SUBLANES = 8
ROW_TILES = D_MODEL // LANES


def _store_token_tiles(ref, x):
    t = x.shape[0]
    for s in range(ROW_TILES):
        ref[pl.ds(s, t, stride=ROW_TILES), :] = x[:, s * LANES:(s + 1) * LANES]


def _load_token_tiles(ref, t):
    return jnp.concatenate([ref[pl.ds(s, t, stride=ROW_TILES), :] for s in range(ROW_TILES)], axis=-1)


def _chunk_tril(n, strict=False):
    r = _iota((n, n), 0)
    c = _iota((n, n), 1)
    same = (r // CHUNK) == (c // CHUNK)
    return same & ((c < r) if strict else (c <= r))


def _inproj_kernel(x_ref, g_ref, w_ref, *out_refs):
    h = _rms(x_ref[...], g_ref[...]).astype(BF16)
    off = 0
    for o_ref in out_refs:
        wd = o_ref.shape[-1]
        o_ref[...] = jnp.dot(h, w_ref[:, off:off + wd], preferred_element_type=F32)
        off += wd


def _inproj(x, g, w, tm=256):
    n = x.shape[0]
    nc = w.shape[1]
    return pl.pallas_call(
        _inproj_kernel,
        grid=(n // tm,),
        in_specs=[pl.BlockSpec((tm, D_MODEL), lambda i: (i, 0)),
                  pl.BlockSpec((1, D_MODEL), lambda i: (0, 0)),
                  pl.BlockSpec((D_MODEL, nc), lambda i: (0, 0), pipeline_mode=pl.Buffered(1))],
        out_specs=[pl.BlockSpec((tm, wd), lambda i: (i, 0)) for wd in OUT_WIDTHS],
        out_shape=[jax.ShapeDtypeStruct((n, wd), F32) for wd in OUT_WIDTHS],
        compiler_params=_params(1),
        name="inproj",
    )(x, g, w)


def _rope_kernel(pos_ref, invf_ref, cos_ref, sin_ref):
    ang = pos_ref[...].astype(F32) * invf_ref[...]
    d = _iota(ang.shape, 1) % SWA_HEAD_DIM
    half = ROT_DIM // 2
    cos_ref[...] = jnp.where(d < ROT_DIM, jnp.cos(ang), 1.0)
    s = jnp.sin(ang)
    sin_ref[...] = jnp.where(d < half, -s, jnp.where(d < ROT_DIM, s, 0.0))


def _rope_tables(positions, tm=1024):
    n = positions.size
    tm = min(tm, n)
    inv_freq = 1.0 / (ROPE_THETA ** (jnp.arange(0, ROT_DIM, 2, dtype=F32) / ROT_DIM))
    lane_freq = jnp.tile(jnp.concatenate([inv_freq, inv_freq, jnp.zeros(SWA_HEAD_DIM - ROT_DIM, F32)]),
                         LANES // SWA_HEAD_DIM)[None]
    return pl.pallas_call(
        _rope_kernel,
        grid=(n // tm,),
        in_specs=[pl.BlockSpec((tm, 1), lambda i: (i, 0)),
                  pl.BlockSpec((1, LANES), lambda i: (0, 0))],
        out_specs=[pl.BlockSpec((tm, LANES), lambda i: (i, 0))] * 2,
        out_shape=[jax.ShapeDtypeStruct((n, LANES), F32)] * 2,
        compiler_params=_params(1),
        name="rope_tables",
    )(positions.reshape(n, 1), lane_freq)


def _dot_exact_rhs(x, b):
    hi, lo = _split2(x)
    return jnp.dot(hi, b, preferred_element_type=F32) + jnp.dot(lo, b, preferred_element_type=F32)


def _rope_matrices():
    m = _iota((LANES, LANES), 0)
    l = _iota((LANES, LANES), 1)
    d = l % SWA_HEAD_DIM
    half = ROT_DIM // 2
    same_head = (m // SWA_HEAD_DIM) == (l // SWA_HEAD_DIM)
    partner = ((d < half) & (m == l + half)) | ((d >= half) & (d < ROT_DIM) & (m == l - half))
    return jnp.where(same_head, 1.0, 0.0).astype(BF16), jnp.where(partner, 1.0, 0.0).astype(BF16)


def _norm_rope(x, gain, cos, sin, same_head, partner):
    ms = _dot_exact_rhs(x * x, same_head) * (1.0 / SWA_HEAD_DIM)
    xn = x * lax.rsqrt(ms + NORM_EPS) * gain
    return xn * cos + _dot_exact_rhs(xn, partner) * sin


def _swa_kernel(sink_ref, q_ref, kv_ref, cos_ref, sin_ref, qn_ref, kn_ref, o_ref, kprev_ref, vprev_ref):
    j = pl.program_id(1)
    w = SWA_WINDOW
    hd = SWA_HEAD_DIM
    group = SWA_Q_HEADS // SWA_KV_HEADS

    @pl.when(j == 0)
    def _():
        kprev_ref[...] = jnp.zeros_like(kprev_ref)
        vprev_ref[...] = jnp.zeros_like(vprev_ref)

    cos = cos_ref[...]
    sin = sin_ref[...]
    same_head, partner = _rope_matrices()
    kc = _norm_rope(kv_ref[:, :SWA_KV], kn_ref[...], cos, sin, same_head, partner).astype(BF16)
    vc = kv_ref[:, SWA_KV:].astype(BF16)
    k_all = jnp.concatenate([kprev_ref[...], kc], axis=0)
    v_all = jnp.concatenate([vprev_ref[...], vc], axis=0)
    kprev_ref[...] = kc
    vprev_ref[...] = vc

    rows = group * w
    qi = _iota((rows, 2 * w), 0) % w
    kj = _iota((rows, 2 * w), 1)
    first_key = jnp.where(j > 0, 0, w)
    mask = (kj > qi) & (kj <= qi + w) & (kj >= first_key)
    head_of_row = _iota((rows, 1), 0) // w
    qrs = [_norm_rope(q_ref[:, c * LANES:(c + 1) * LANES], qn_ref[...], cos, sin, same_head, partner)
           for c in range(SWA_Q // LANES)]
    per_block = LANES // hd
    qgs, sinks = [], []
    for g in range(SWA_KV_HEADS):
        heads = range(g * group, (g + 1) * group)
        qgs.append(jnp.concatenate(
            [qrs[h // per_block][:, (h % per_block) * hd:(h % per_block + 1) * hd] for h in heads], axis=0))
        sk = jnp.full((rows, 1), sink_ref[g * group], F32)
        for i in range(1, group):
            sk = jnp.where(head_of_row == i, sink_ref[g * group + i], sk)
        sinks.append(sk)
    ss = [jnp.where(mask, _dot_nt(qgs[g], k_all[:, g * hd:(g + 1) * hd]) * (hd ** -0.5), MASK_VALUE)
          for g in range(SWA_KV_HEADS)]
    ms = [jnp.maximum(jnp.max(s, axis=-1, keepdims=True), sk) for s, sk in zip(ss, sinks)]
    es = [jnp.exp(s - m).astype(BF16) for s, m in zip(ss, ms)]
    ones = jnp.ones((2 * w, hd), BF16)
    dens = [jnp.dot(e, ones, preferred_element_type=F32) + jnp.exp(sk - m) for e, sk, m in zip(es, sinks, ms)]
    for g in range(SWA_KV_HEADS):
        og = jnp.dot(es[g], v_all[:, g * hd:(g + 1) * hd], preferred_element_type=F32) / dens[g]
        for i in range(group):
            h = g * group + i
            o_ref[:, h * hd:(h + 1) * hd] = og[i * w:(i + 1) * w]


def _swa(aq, akv, cos_t, sin_t, q_norm, k_norm, sinks, batch, seq):
    n = aq.shape[0]
    w = SWA_WINDOW
    nq = seq // w
    cur = lambda b, j: (b * nq + j, 0)
    rep = LANES // SWA_HEAD_DIM
    return pl.pallas_call(
        _swa_kernel,
        grid=(batch, nq),
        in_specs=[pl.BlockSpec(memory_space=pltpu.SMEM),
                  pl.BlockSpec((w, SWA_Q), cur),
                  pl.BlockSpec((w, 2 * SWA_KV), cur),
                  pl.BlockSpec((w, LANES), cur),
                  pl.BlockSpec((w, LANES), cur),
                  pl.BlockSpec((1, LANES), lambda b, j: (0, 0)),
                  pl.BlockSpec((1, LANES), lambda b, j: (0, 0))],
        out_specs=pl.BlockSpec((w, SWA_Q), cur),
        out_shape=jax.ShapeDtypeStruct((n, SWA_Q), F32),
        scratch_shapes=[pltpu.VMEM((w, SWA_KV), BF16), pltpu.VMEM((w, SWA_KV), BF16)],
        compiler_params=_params(2),
        name="swa",
    )(sinks, aq, akv, cos_t, sin_t, jnp.tile(q_norm, rep)[None], jnp.tile(k_norm, rep)[None])


def _gla_kernel(qk_ref, v_ref, r_ref, sm_ref, wa_ref, ba_ref, gn_ref, o_ref, state_ref):
    @pl.when(pl.program_id(1) == 0)
    def _():
        state_ref[...] = jnp.zeros_like(state_ref)

    tc = qk_ref.shape[0]
    la = _log_sigmoid(_dot(sm_ref[...], wa_ref[...]) + ba_ref[...]) * (1.0 / GLA_GATE_NORM)
    tri = jnp.where(_chunk_tril(tc), 1.0, 0.0).astype(BF16)
    b = _dot_exact_lhs(tri, la)
    causal = _iota((CHUNK, CHUNK), 1) <= _iota((CHUNK, CHUNK), 0)
    n_chunks = tc // CHUNK
    qes, dcols, intra, upd = [], [], {}, {}
    for c in range(n_chunks):
        rows = slice(c * CHUNK, (c + 1) * CHUNK)
        bc = b[rows]
        b_last = bc[CHUNK - 1:CHUNK]
        k = qk_ref[rows, GLA_K:]
        qe = qk_ref[rows, :GLA_K] * (GLA_DK ** -0.5) * jnp.exp(bc)
        ke = k * jnp.exp(-bc)
        kd = k * jnp.exp(b_last - bc)
        qes.append(qe)
        dcols.append(jnp.transpose(jnp.broadcast_to(jnp.exp(b_last), (GLA_DV, GLA_K))))
        for h in range(GLA_HEADS):
            ks = slice(h * GLA_DK, (h + 1) * GLA_DK)
            vh = v_ref[rows, h * GLA_DV:(h + 1) * GLA_DV]
            att = jnp.where(causal, _dot_nt(qe[:, ks], ke[:, ks]), 0.0)
            intra[c, h] = _dot(att, vh)
            upd[c, h] = _dot_tn(kd[:, ks], vh)
    for h in range(GLA_HEADS):
        ks = slice(h * GLA_DK, (h + 1) * GLA_DK)
        vs = slice(h * GLA_DV, (h + 1) * GLA_DV)
        st = state_ref[h]
        for c in range(n_chunks):
            rows = slice(c * CHUNK, (c + 1) * CHUNK)
            o = intra[c, h] + _dot(qes[c][:, ks], st)
            st = dcols[c][ks] * st + upd[c, h]
            o_ref[rows, vs] = _rms(o, gn_ref[...]) * _silu(r_ref[rows, vs])
        state_ref[h] = st


def _gla(bqk, bv, br, small, wa2, ba, gn, batch, seq, tc=256):
    n = bqk.shape[0]
    nt = seq // tc
    blk = lambda b, j: (b * nt + j, 0)
    const = lambda b, j: (0, 0)
    wa_pad = jnp.zeros((LANES, GLA_K), F32).at[SMALL_LR:SMALL_LR + GLA_GATE_RANK].set(wa2)
    return pl.pallas_call(
        _gla_kernel,
        grid=(batch, nt),
        in_specs=[pl.BlockSpec((tc, 2 * GLA_K), blk),
                  pl.BlockSpec((tc, GLA_V), blk),
                  pl.BlockSpec((tc, GLA_V), blk),
                  pl.BlockSpec((tc, LANES), blk),
                  pl.BlockSpec((LANES, GLA_K), const),
                  pl.BlockSpec((1, GLA_K), const),
                  pl.BlockSpec((1, GLA_DV), const)],
        out_specs=pl.BlockSpec((tc, GLA_V), blk),
        out_shape=jax.ShapeDtypeStruct((n, GLA_V), F32),
        scratch_shapes=[pltpu.VMEM((GLA_HEADS, GLA_DK, GLA_DV), F32)],
        compiler_params=_params(2),
        name="gla",
    )(bqk, bv, br, small, wa_pad, ba[None], gn[None])


def _l2(x):
    return x * lax.rsqrt(jnp.sum(x * x, axis=-1, keepdims=True) + NORM_EPS)


def _gdn_kernel(x_ref, xp_ref, g_ref, sm_ref, cw_ref, al_ref, dt_ref, gn_ref, o_ref, state_ref):
    first = pl.program_id(1) == 0

    @pl.when(first)
    def _():
        state_ref[...] = jnp.zeros_like(state_ref)

    tc = x_ref.shape[0]
    halo = xp_ref.shape[0]
    x = x_ref[...]
    xprev = jnp.where(first, 0.0, xp_ref[...])
    sm = sm_ref[...]
    row = _iota((halo, 1), 0)
    conv = cw_ref[GDN_CONV - 1:GDN_CONV] * x
    for t in range(1, GDN_CONV):
        rolled = pltpu.roll(x, t, 0)
        head = jnp.where(row < t, pltpu.roll(xprev, t, 0), rolled[:halo])
        shifted = jnp.concatenate([head, rolled[halo:]], axis=0)
        conv = conv + cw_ref[GDN_CONV - 1 - t:GDN_CONV - t] * shifted
    qkv = _silu(conv)

    beta_all = _sigmoid(sm)
    g_all = -jnp.exp(al_ref[...]) * _softplus(sm + dt_ref[...])
    tri = jnp.where(_chunk_tril(tc), 1.0, 0.0).astype(BF16)
    big_g = _dot_exact_lhs(tri, g_all)
    g_rows = jnp.transpose(big_g)

    r = _iota((CHUNK, CHUNK), 0)
    cidx = _iota((CHUNK, CHUNK), 1)
    causal = cidx <= r
    strict = cidx < r
    eye = jnp.where(cidx == r, 1.0, 0.0)

    units = [(c, h) for c in range(tc // CHUNK) for h in range(GDN_HEADS)]
    qs, ks, egs, g_lasts, gcs, decays, lows, rhss = [], [], [], [], [], [], [], []
    for c, h in units:
        rows = slice(c * CHUNK, (c + 1) * CHUNK)
        q = _l2(qkv[rows, h * GDN_DK:(h + 1) * GDN_DK]) * (GDN_DK ** -0.5)
        k = _l2(qkv[rows, GDN_K + h * GDN_DK:GDN_K + (h + 1) * GDN_DK])
        v = qkv[rows, 2 * GDN_K + h * GDN_DV:2 * GDN_K + (h + 1) * GDN_DV]
        beta = beta_all[rows, SMALL_BETA + h:SMALL_BETA + h + 1]
        gc = big_g[rows, SMALL_A + h:SMALL_A + h + 1]
        g_row = g_rows[SMALL_A + h:SMALL_A + h + 1, c * CHUNK:(c + 1) * CHUNK]
        decay = jnp.where(causal, jnp.exp(jnp.where(causal, gc - g_row, 0.0)), 0.0)
        eg = jnp.exp(gc)
        qs.append(q)
        ks.append(k)
        egs.append(eg)
        gcs.append(gc)
        g_lasts.append(gc[CHUNK - 1:CHUNK])
        decays.append(decay)
        lows.append(jnp.where(strict, beta * _dot_nt(k, k) * decay, 0.0))
        rhss.append(jnp.concatenate([v * beta, k * (beta * eg)], axis=-1))
    invs = [eye - low for low in lows]
    pws = lows
    for _ in range(int(math.log2(CHUNK)) - 1):
        pws = [_dot(pw, pw) for pw in pws]
        invs = [inv + _dot(inv, pw) for inv, pw in zip(invs, pws)]
    sols = [_dot(inv, rhs) for inv, rhs in zip(invs, rhss)]
    atts = [_dot_nt(q, k) * decay for q, k, decay in zip(qs, ks, decays)]
    k_decs = [k * jnp.exp(gl - gc) for k, gl, gc in zip(ks, g_lasts, gcs)]
    kws = [_dot_tn(k_dec, sol) for k_dec, sol in zip(k_decs, sols)]
    aws = [_dot(att, sol) for att, sol in zip(atts, sols)]

    for i, (c, h) in enumerate(units):
        rows = slice(c * CHUNK, (c + 1) * CHUNK)
        st = state_ref[h]
        lhs = jnp.concatenate([kws[i][:, GDN_DV:], qs[i] * egs[i] - aws[i][:, GDN_DV:]], axis=0)
        prod = _dot(lhs, st)
        o = prod[GDN_DK:] + aws[i][:, :GDN_DV]
        state_ref[h] = jnp.exp(g_lasts[i]) * st - prod[:GDN_DK] + kws[i][:, :GDN_DV]
        vs = slice(h * GDN_DV, (h + 1) * GDN_DV)
        o_ref[rows, vs] = _rms(o, gn_ref[...]) * _silu(g_ref[rows, vs])


def _gdn(cqkv, cg, small, conv_w, a_log, dt_bias, gn, batch, seq, tc=512, halo=8):
    n = cqkv.shape[0]
    nt = seq // tc
    blk = lambda b, j: (b * nt + j, 0)
    const = lambda b, j: (0, 0)
    prev = lambda b, j: (jnp.maximum((b * nt + j) * (tc // halo) - 1, 0), 0)
    al_row = jnp.zeros((1, LANES), F32).at[0, SMALL_A:SMALL_A + GDN_HEADS].set(a_log)
    dt_row = jnp.zeros((1, LANES), F32).at[0, SMALL_A:SMALL_A + GDN_HEADS].set(dt_bias)
    return pl.pallas_call(
        _gdn_kernel,
        grid=(batch, nt),
        in_specs=[pl.BlockSpec((tc, GDN_QKV), blk),
                  pl.BlockSpec((halo, GDN_QKV), prev),
                  pl.BlockSpec((tc, GDN_V), blk),
                  pl.BlockSpec((tc, LANES), blk),
                  pl.BlockSpec((GDN_CONV, GDN_QKV), const),
                  pl.BlockSpec((1, LANES), const),
                  pl.BlockSpec((1, LANES), const),
                  pl.BlockSpec((1, GDN_DV), const)],
        out_specs=pl.BlockSpec((tc, GDN_V), blk),
        out_shape=jax.ShapeDtypeStruct((n, GDN_V), F32),
        scratch_shapes=[pltpu.VMEM((GDN_HEADS, GDN_DK, GDN_DV), F32)],
        compiler_params=_params(2),
        name="gdn",
    )(cqkv, cqkv, cg, small, conv_w, al_row, dt_row, gn[None])


def _first_argmax(x, lane):
    m = jnp.max(x, axis=-1, keepdims=True)
    idx = jnp.min(jnp.where(x == m, lane.astype(F32), float(LANES)), axis=-1, keepdims=True)
    return m, idx.astype(I32)


def _merge_kernel(ya_ref, yb_ref, yc_ref, gt_ref, x_ref, wa_ref, wb_ref, wc_ref, wo_ref, fn_ref,
                  wr_ref, br_ref, x1_ref, h2_ref, ri_ref, rw_ref, cnt_ref, carry_ref):
    @pl.when(pl.program_id(0) == 0)
    def _():
        carry_ref[...] = jnp.zeros_like(carry_ref)

    d = D_MODEL
    ya = ya_ref[...].astype(BF16)
    yb = yb_ref[...].astype(BF16)
    yc = yc_ref[...].astype(BF16)
    cw = 2 * LANES
    merged = []
    for c in range(d // cw):
        lo = c * cw
        merged.append((
            _sigmoid(gt_ref[:, lo:lo + cw]) * jnp.dot(ya, wa_ref[:, lo:lo + cw], preferred_element_type=F32)
            + _sigmoid(gt_ref[:, d + lo:d + lo + cw]) * jnp.dot(yb, wb_ref[:, lo:lo + cw], preferred_element_type=F32)
            + _sigmoid(gt_ref[:, 2 * d + lo:2 * d + lo + cw]) * jnp.dot(yc, wc_ref[:, lo:lo + cw], preferred_element_type=F32)
        ).astype(BF16))
    merged = jnp.concatenate(merged, axis=-1)
    x1 = x_ref[...] + jnp.dot(merged, wo_ref[...], preferred_element_type=F32)
    x1_ref[...] = x1
    h2 = _rms(x1, fn_ref[...])
    _store_token_tiles(h2_ref, h2)

    logits = _dot_hi(h2, wr_ref[...]) + br_ref[...]
    tm = logits.shape[0]
    lane = _iota((tm, LANES), 1)
    neg = -jnp.inf
    is_c = lane < N_GROUPS
    cm, g_idx = _first_argmax(jnp.where(is_c, logits, neg), lane)
    g_prob = 1.0 / jnp.sum(jnp.where(is_c, jnp.exp(logits - cm), 0.0), axis=-1, keepdims=True)
    sel = (lane >= N_GROUPS) & (((lane - N_GROUPS) // EXPERTS_PER_GROUP) == g_idx)
    fm = jnp.max(jnp.where(sel, logits, neg), axis=-1, keepdims=True)
    ef = jnp.where(sel, jnp.exp(logits - fm), 0.0)
    p1, i1 = _first_argmax(jnp.where(sel, ef, neg), lane)
    p2, i2 = _first_argmax(jnp.where(sel & (lane != i1), ef, neg), lane)
    w1 = g_prob * p1 / (p1 + p2)
    w2 = g_prob * p2 / (p1 + p2)
    e1 = i1 - N_GROUPS
    e2 = i2 - N_GROUPS

    oh = jnp.where((lane == e1) | (lane == e2 + N_EXPERTS), 1.0, 0.0)
    stril = jnp.where(_iota((tm, tm), 1) < _iota((tm, tm), 0), 1.0, 0.0).astype(BF16)
    before = jnp.dot(stril, oh.astype(BF16), preferred_element_type=F32)
    tot = jnp.sum(oh, axis=0, keepdims=True)
    tot_first = jnp.where(lane[:1] < N_EXPERTS, tot, 0.0)
    carry = carry_ref[...]
    base = carry + pltpu.roll(carry + tot_first, N_EXPERTS, 1)
    ranks = oh * (before + base)
    rank1 = jnp.sum(jnp.where(lane < N_EXPERTS, ranks, 0.0), axis=-1, keepdims=True)
    rank2 = jnp.sum(jnp.where(lane < N_EXPERTS, 0.0, ranks), axis=-1, keepdims=True)
    new_carry = carry + tot_first + pltpu.roll(tot - tot_first, LANES - N_EXPERTS, 1)
    carry_ref[...] = new_carry
    cnt_ref[...] = jnp.broadcast_to(new_carry, cnt_ref.shape).astype(I32)

    ri_ref[...] = jnp.where(lane == 0, e1, jnp.where(lane == 1, e2, jnp.where(
        lane == 2, rank1.astype(I32), jnp.where(lane == 3, rank2.astype(I32), 0))))
    rw_ref[...] = jnp.where(lane == 0, w1, jnp.where(lane == 1, w2, 0.0))


def _merge(ya, yb, yc, gates, x, wa, wb, wc, wo, fn, wr, br, tm=512):
    n = x.shape[0]
    d = D_MODEL
    blk = lambda i: (i, 0)
    const = lambda i: (0, 0)
    return pl.pallas_call(
        _merge_kernel,
        grid=(n // tm,),
        in_specs=[pl.BlockSpec((tm, SWA_Q), blk), pl.BlockSpec((tm, GLA_V), blk),
                  pl.BlockSpec((tm, GDN_V), blk), pl.BlockSpec((tm, 3 * d), blk),
                  pl.BlockSpec((tm, d), blk),
                  pl.BlockSpec((SWA_Q, d), const), pl.BlockSpec((GLA_V, d), const),
                  pl.BlockSpec((GDN_V, d), const), pl.BlockSpec((d, d), const),
                  pl.BlockSpec((1, d), const), pl.BlockSpec((d, LANES), const),
                  pl.BlockSpec((1, LANES), const)],
        out_specs=[pl.BlockSpec((tm, d), blk), pl.BlockSpec((tm * SUBLANES, LANES), blk),
                   pl.BlockSpec((tm, LANES), blk), pl.BlockSpec((tm, LANES), blk),
                   pl.BlockSpec((8, LANES), const)],
        out_shape=[jax.ShapeDtypeStruct((n, d), F32), jax.ShapeDtypeStruct((n * SUBLANES, LANES), F32),
                   jax.ShapeDtypeStruct((n, LANES), I32), jax.ShapeDtypeStruct((n, LANES), F32),
                   jax.ShapeDtypeStruct((8, LANES), I32)],
        scratch_shapes=[pltpu.VMEM((1, LANES), F32)],
        compiler_params=_params(1),
        name="merge_route",
    )(ya, yb, yc, gates, x, wa, wb, wc, wo, fn, wr, br)


def _dispatch_kernel(fill_ref, slot_ref, h_ref, xs_ref, zbuf, sem, zsem):
    td = h_ref.shape[0] // SUBLANES
    n_blocks = xs_ref.shape[0] // (MOE_ROWS * SUBLANES)

    def zero_fill(start):
        def run(cp):
            if start:
                cp.start()
            else:
                cp.wait()

        def pad_rows(e, carry):
            lo = fill_ref[e]
            n_pad = fill_ref[N_EXPERTS + e] - lo
            bit = MOE_ROWS // 2
            while bit >= 1:
                off = lo + (n_pad // (2 * bit)) * (2 * bit)

                @pl.when((n_pad & bit) != 0)
                def _(off=off, bit=bit):
                    run(pltpu.make_async_copy(
                        zbuf.at[pl.ds(0, bit * SUBLANES)],
                        xs_ref.at[pl.ds(pl.multiple_of(off * SUBLANES, SUBLANES), bit * SUBLANES)], zsem))

                bit //= 2
            return carry

        lax.fori_loop(0, N_EXPERTS, pad_rows, 0)

        def unused_block(b, carry):
            run(pltpu.make_async_copy(
                zbuf, xs_ref.at[pl.ds(pl.multiple_of(b * (MOE_ROWS * SUBLANES), SUBLANES), MOE_ROWS * SUBLANES)],
                zsem))
            return carry

        lax.fori_loop(fill_ref[2 * N_EXPERTS], n_blocks, unused_block, 0)

    @pl.when(pl.program_id(0) == 0)
    def _():
        zbuf[...] = jnp.zeros_like(zbuf)
        zero_fill(True)
        zero_fill(False)

    def rows(start):
        def body(t, carry):
            src = h_ref.at[pl.ds(pl.multiple_of(t * SUBLANES, SUBLANES), SUBLANES)]
            for k in range(TOP_K):
                s = slot_ref[0, 0, TOP_K * t + k]
                cp = pltpu.make_async_copy(
                    src, xs_ref.at[pl.ds(pl.multiple_of(s * SUBLANES, SUBLANES), SUBLANES)], sem)
                if start:
                    cp.start(priority=k % 2)
                else:
                    cp.wait()
            return carry

        lax.fori_loop(0, td, body, 0, unroll=4 if start else 8)

    rows(True)
    rows(False)


def _dispatch(h2, slots, fill, n_slots, td=1024):
    n = h2.shape[0] // SUBLANES
    td = min(td, n)
    grid_spec = pltpu.PrefetchScalarGridSpec(
        num_scalar_prefetch=1,
        grid=(n // td,),
        in_specs=[pl.BlockSpec((1, 1, TOP_K * td), lambda i, fill: (i, 0, 0), memory_space=pltpu.SMEM),
                  pl.BlockSpec((td * SUBLANES, LANES), lambda i, fill: (i, 0))],
        out_specs=pl.BlockSpec(memory_space=pl.ANY),
        scratch_shapes=[pltpu.VMEM((MOE_ROWS * SUBLANES, LANES), F32), pltpu.SemaphoreType.DMA(()),
                        pltpu.SemaphoreType.DMA(())],
    )
    return pl.pallas_call(
        _dispatch_kernel,
        grid_spec=grid_spec,
        out_shape=jax.ShapeDtypeStruct((n_slots * SUBLANES, LANES), F32),
        compiler_params=_params(1),
        name="dispatch",
    )(fill, slots.reshape(n // td, 1, TOP_K * td), h2)


def _expert_kernel(be_ref, nxt_ref, par_ref, xs_ref, wg_hbm, wu_hbm, wd_hbm, ys_ref,
                   wg_f, wu_f, wd_f, wg_s, wu_s, wd_s, sems, *, layer):
    i = pl.program_id(0)
    e = be_ref[i]
    slot = par_ref[i]
    first_of_expert = jnp.logical_or(i == 0, e != be_ref[jnp.maximum(i - 1, 0)])

    def weight_copies(expert, buf):
        return (pltpu.make_async_copy(wg_hbm.at[layer, expert], wg_f.at[buf], sems.at[buf, 0]),
                pltpu.make_async_copy(wu_hbm.at[layer, expert], wu_f.at[buf], sems.at[buf, 1]),
                pltpu.make_async_copy(wd_hbm.at[layer, expert], wd_f.at[buf], sems.at[buf, 2]))

    @pl.when(i == 0)
    def _():
        for cp in weight_copies(e, slot):
            cp.start()

    @pl.when(first_of_expert)
    def _():
        @pl.when(nxt_ref[i] != e)
        def _():
            for cp in weight_copies(nxt_ref[i], 1 - slot):
                cp.start()

        for cp in weight_copies(e, slot):
            cp.wait()
        wg_s[...] = wg_f[slot].astype(BF16)
        wu_s[...] = wu_f[slot].astype(BF16)
        wd_s[...] = wd_f[slot].astype(BF16)

    half = xs_ref.shape[0] // 2
    th = half // SUBLANES
    xs = [_load_token_tiles(xs_ref.at[pl.ds(p * half, half)], th).astype(BF16) for p in range(2)]
    gs = [jnp.dot(x, wg_s[...], preferred_element_type=F32) for x in xs]
    us = [jnp.dot(x, wu_s[...], preferred_element_type=F32) for x in xs]
    acts = [(_silu(g) * u).astype(BF16) for g, u in zip(gs, us)]
    for p in range(2):
        _store_token_tiles(ys_ref.at[pl.ds(p * half, half)],
                           jnp.dot(acts[p], wd_s[...], preferred_element_type=F32))


def _experts(xs, block_expert, next_expert, block_parity, wg, wu, wd, layer):
    n_slots = xs.shape[0] // SUBLANES
    tb = MOE_ROWS
    blk = lambda i, be, nx, par: (i, 0)
    grid_spec = pltpu.PrefetchScalarGridSpec(
        num_scalar_prefetch=3,
        grid=(n_slots // tb,),
        in_specs=[pl.BlockSpec((tb * SUBLANES, LANES), blk),
                  pl.BlockSpec(memory_space=pl.ANY), pl.BlockSpec(memory_space=pl.ANY),
                  pl.BlockSpec(memory_space=pl.ANY)],
        out_specs=pl.BlockSpec((tb * SUBLANES, LANES), blk),
        scratch_shapes=[pltpu.VMEM((2, D_MODEL, EXPERT_FF), F32), pltpu.VMEM((2, D_MODEL, EXPERT_FF), F32),
                        pltpu.VMEM((2, EXPERT_FF, D_MODEL), F32),
                        pltpu.VMEM((D_MODEL, EXPERT_FF), BF16), pltpu.VMEM((D_MODEL, EXPERT_FF), BF16),
                        pltpu.VMEM((EXPERT_FF, D_MODEL), BF16),
                        pltpu.SemaphoreType.DMA((2, 3))],
    )
    return pl.pallas_call(
        functools.partial(_expert_kernel, layer=layer),
        grid_spec=grid_spec,
        out_shape=jax.ShapeDtypeStruct((n_slots * SUBLANES, LANES), F32),
        compiler_params=_params(1),
        name="experts",
    )(block_expert, next_expert, block_parity, xs, wg, wu, wd)


def _combine_kernel(slot_ref, next_slot_ref, x1_ref, rw_ref, p_ref, ys_ref, pn_ref, wpg_ref, wpl_ref, o_ref,
                    ybuf, sems):
    i = pl.program_id(0)
    tp = x1_ref.shape[0]
    cur = i % 2

    def rows(sref, buf, start):
        def body(t, carry):
            for k in range(TOP_K):
                s = sref[0, 0, TOP_K * t + k]
                cp = pltpu.make_async_copy(
                    ys_ref.at[pl.ds(pl.multiple_of(s * SUBLANES, SUBLANES), SUBLANES)],
                    ybuf.at[buf, k, pl.ds(pl.multiple_of(t * SUBLANES, SUBLANES), SUBLANES)], sems.at[buf])
                if start:
                    cp.start(priority=k % 2)
                else:
                    cp.wait()
            return carry

        lax.fori_loop(0, tp, body, 0, unroll=4 if start else 8)

    @pl.when(i == 0)
    def _():
        rows(slot_ref, 0, True)

    rows(slot_ref, cur, False)

    pending = list(range(tp))

    def issue_next(n_tokens):
        for _ in range(n_tokens):
            t = pending.pop(0)
            for k in range(TOP_K):
                s = next_slot_ref[0, 0, TOP_K * t + k]
                pltpu.make_async_copy(
                    ys_ref.at[pl.ds(pl.multiple_of(s * SUBLANES, SUBLANES), SUBLANES)],
                    ybuf.at[1 - cur, k, pl.ds(t * SUBLANES, SUBLANES)], sems.at[1 - cur]).start(priority=k % 2)

    n_stage = D_MODEL // (2 * LANES) + 1
    rw = rw_ref[...]
    x2 = (x1_ref[...] + rw[:, 0:1] * _load_token_tiles(ybuf.at[cur, 0], tp)
          + rw[:, 1:2] * _load_token_tiles(ybuf.at[cur, 1], tp))
    issue_next(tp // n_stage)
    h3 = _rms(x2, pn_ref[...]).astype(BF16)
    pe = p_ref[0].astype(BF16)
    cw = 2 * LANES
    for c in range(D_MODEL // cw):
        cols = slice(c * cw, (c + 1) * cw)
        gate = _sigmoid(jnp.dot(h3, wpg_ref[:, cols], preferred_element_type=F32))
        o_ref[:, cols] = x2[:, cols] + gate * jnp.dot(pe, wpl_ref[:, cols], preferred_element_type=F32)
        issue_next(tp // n_stage if c + 1 < D_MODEL // cw else len(pending))

    @pl.when(i + 1 == pl.num_programs(0))
    def _():
        rows(next_slot_ref, 1 - cur, False)


def _combine_ple(x1, rw, slots, ys, p, layer, pn, wpg, wpl, tp=256):
    n = x1.shape[0]
    d = D_MODEL
    blk = lambda i: (i, 0)
    const = lambda i: (0, 0)
    nsteps = n // tp
    slot_blocks = slots.reshape(nsteps, 1, TOP_K * tp)
    return pl.pallas_call(
        _combine_kernel,
        grid=(nsteps,),
        in_specs=[pl.BlockSpec((1, 1, TOP_K * tp), lambda i: (i, 0, 0), memory_space=pltpu.SMEM),
                  pl.BlockSpec((1, 1, TOP_K * tp), lambda i: (jnp.minimum(i + 1, nsteps - 1), 0, 0),
                               memory_space=pltpu.SMEM),
                  pl.BlockSpec((tp, d), blk), pl.BlockSpec((tp, LANES), blk),
                  pl.BlockSpec((1, tp, PLE_DIM), lambda i: (layer, i, 0)),
                  pl.BlockSpec(memory_space=pl.ANY),
                  pl.BlockSpec((1, d), const), pl.BlockSpec((d, d), const),
                  pl.BlockSpec((PLE_DIM, d), const)],
        out_specs=pl.BlockSpec((tp, d), blk),
        out_shape=jax.ShapeDtypeStruct((n, d), F32),
        scratch_shapes=[pltpu.VMEM((2, TOP_K, tp * SUBLANES, LANES), F32), pltpu.SemaphoreType.DMA((2,))],
        compiler_params=_params(1),
        name="combine_ple",
    )(slot_blocks, slot_blocks, x1, rw, p, ys, pn, wpg, wpl)


def _pack_w_in(w_in):
    o = IN_OFFSETS
    w = w_in.astype(BF16)
    pad = jnp.zeros((D_MODEL, LANES - GLA_GATE_RANK - 2 * GDN_HEADS), BF16)
    cols = [w[:, :o[6]], w[:, o[7]:o[9]], w[:, o[11]:], w[:, o[6]:o[7]], w[:, o[9]:o[11]], pad]
    return jnp.concatenate(cols, axis=1)


def _layer(x, p, layer, cos_t, sin_t, batch, seq, attn_norm, w_in, q_norm, k_norm, sinks, gla_wa2, gla_ba,
           gla_norm, gdn_conv, gdn_a_log, gdn_dt_bias, gdn_norm, w_br_a, w_br_b, w_br_c, w_o,
           ffn_norm, w_coarse, b_coarse, w_fine, b_fine, w_gate_e, w_up_e, w_down_e,
           ple_norm, w_ple_gate, w_ple):
    n = x.shape[0]
    aq, akv, bqk, bv, br, cqkv, cg, gates, small = _inproj(x, attn_norm[None], _pack_w_in(w_in))
    ya = _swa(aq, akv, cos_t, sin_t, q_norm, k_norm, sinks, batch, seq)
    yb = _gla(bqk, bv, br, small, gla_wa2, gla_ba, gla_norm, batch, seq)
    yc = _gdn(cqkv, cg, small, gdn_conv, gdn_a_log, gdn_dt_bias, gdn_norm, batch, seq)

    wr = jnp.zeros((D_MODEL, LANES), F32).at[:, :N_GROUPS].set(w_coarse)
    wr = wr.at[:, N_GROUPS:N_GROUPS + N_EXPERTS].set(w_fine)
    brow = jnp.zeros((1, LANES), F32).at[0, :N_GROUPS].set(b_coarse)
    brow = brow.at[0, N_GROUPS:N_GROUPS + N_EXPERTS].set(b_fine)
    x1, h2, ri, rw, cnt = _merge(ya, yb, yc, gates, x, w_br_a.astype(BF16), w_br_b.astype(BF16),
                                 w_br_c.astype(BF16), w_o.astype(BF16), ffn_norm[None], wr, brow)

    counts = cnt[0, :N_EXPERTS]
    padded = (counts + MOE_ROWS - 1) // MOE_ROWS * MOE_ROWS
    pad_end = jnp.cumsum(padded)
    pad_start = pad_end - padded
    slots = (pad_start[ri[:, :TOP_K]] + ri[:, TOP_K:2 * TOP_K]).astype(I32)
    n_slots = n * TOP_K + N_EXPERTS * MOE_ROWS
    block_start = jnp.arange(n_slots // MOE_ROWS, dtype=I32) * MOE_ROWS
    last_owner = jnp.max(jnp.where(padded > 0, jnp.arange(N_EXPERTS, dtype=I32), 0))
    block_expert = jnp.minimum(jnp.sum((pad_end[None, :] <= block_start[:, None]).astype(I32), axis=1),
                               last_owner).astype(I32)

    experts = jnp.arange(N_EXPERTS, dtype=I32)
    owns = padded > 0
    later = lax.cummin(jnp.where(owns, experts, N_EXPERTS)[::-1])[::-1]
    next_owner = jnp.concatenate([later[1:], jnp.full((1,), N_EXPERTS, I32)])
    next_owner = jnp.where(next_owner < N_EXPERTS, next_owner, experts)
    parity = (jnp.cumsum(owns.astype(I32)) - 1) % 2
    next_expert = next_owner[block_expert].astype(I32)
    block_parity = jnp.maximum(parity[block_expert], 0).astype(I32)

    fill = jnp.concatenate([pad_start + counts, pad_end, pad_end[-1:] // MOE_ROWS]).astype(I32)
    xs = _dispatch(h2, slots, fill, n_slots)
    ys = _experts(xs, block_expert, next_expert, block_parity, w_gate_e, w_up_e, w_down_e, layer)
    return _combine_ple(x1, rw, slots, ys, p, layer, ple_norm[None], w_ple_gate.astype(BF16),
                        w_ple.astype(BF16))


def kernel(x, p, positions, attn_norm, w_in, q_norm, k_norm, sinks, gla_wa2, gla_ba, gla_norm, gdn_conv, gdn_a_log, gdn_dt_bias, gdn_norm, w_br_a, w_br_b, w_br_c, w_o, ffn_norm, w_coarse, b_coarse, w_fine, b_fine, w_gate_e, w_up_e, w_down_e, ple_norm, w_ple_gate, w_ple):
    batch, seq, d = x.shape
    n = batch * seq
    depth = p.shape[0]
    cos_t, sin_t = _rope_tables(positions)
    xf = x.reshape(n, d)
    pf = p.reshape(depth, n, p.shape[-1])
    per_layer = (attn_norm, w_in, q_norm, k_norm, sinks, gla_wa2, gla_ba, gla_norm, gdn_conv, gdn_a_log,
                 gdn_dt_bias, gdn_norm, w_br_a, w_br_b, w_br_c, w_o, ffn_norm, w_coarse, b_coarse,
                 w_fine, b_fine, w_gate_e, w_up_e, w_down_e, ple_norm, w_ple_gate, w_ple)
    stacked = (w_gate_e, w_up_e, w_down_e)
    for i in range(depth):
        xf = _layer(xf, pf, i, cos_t, sin_t, batch, seq,
                    *[a if any(a is s for s in stacked) else a[i] for a in per_layer])
    return xf.reshape(batch, seq, d)
```

```python
import functools
import math

import numpy as np
import jax
import jax.numpy as jnp
from jax import lax
from jax.experimental import pallas as pl
from jax.experimental.pallas import tpu as pltpu

F32 = jnp.float32
BF16 = jnp.bfloat16
I32 = jnp.int32

D_MODEL = 1024
PLE_DIM = 256
NORM_EPS = 1e-6
MASK_VALUE = -1e30

SWA_Q_HEADS = 8
SWA_KV_HEADS = 2
SWA_HEAD_DIM = 64
SWA_WINDOW = 128
ROT_DIM = SWA_HEAD_DIM // 4
ROPE_THETA = 500000.0

GLA_HEADS = 4
GLA_DK = 64
GLA_DV = 128
GLA_GATE_RANK = 16
GLA_GATE_NORM = 16.0
CHUNK = 64

GDN_HEADS = 4
GDN_DK = 128
GDN_DV = 128
GDN_CONV = 4

N_GROUPS = 4
EXPERTS_PER_GROUP = 8
N_EXPERTS = N_GROUPS * EXPERTS_PER_GROUP
EXPERT_FF = 512
TOP_K = 2

SWA_Q = SWA_Q_HEADS * SWA_HEAD_DIM
SWA_KV = SWA_KV_HEADS * SWA_HEAD_DIM
GLA_K = GLA_HEADS * GLA_DK
GLA_V = GLA_HEADS * GLA_DV
GDN_K = GDN_HEADS * GDN_DK
GDN_V = GDN_HEADS * GDN_DV
GDN_QKV = 2 * GDN_K + GDN_V
IN_SPLITS = (SWA_Q, SWA_KV, SWA_KV, GLA_K, GLA_K, GLA_V, GLA_GATE_RANK, GLA_V,
             GDN_QKV, GDN_HEADS, GDN_HEADS, GDN_V, 3 * D_MODEL)
IN_OFFSETS = tuple(int(o) for o in np.cumsum((0,) + IN_SPLITS))

LANES = 128
SMALL_LR = 0
SMALL_BETA = GLA_GATE_RANK
SMALL_A = GLA_GATE_RANK + GDN_HEADS

OUT_WIDTHS = (SWA_Q, 2 * SWA_KV, 2 * GLA_K, GLA_V, GLA_V, GDN_QKV, GDN_V, 3 * D_MODEL, LANES)

MOE_ROWS = 512
VMEM_LIMIT = 56 * 1024 * 1024


def _params(n_axes):
    return pltpu.CompilerParams(dimension_semantics=("arbitrary",) * n_axes,
                                vmem_limit_bytes=VMEM_LIMIT)


def _dot(a, b):
    return jnp.dot(a.astype(BF16), b.astype(BF16), preferred_element_type=F32)


def _dot_nt(a, b):
    return lax.dot_general(a.astype(BF16), b.astype(BF16), (((1,), (1,)), ((), ())),
                           preferred_element_type=F32)


def _dot_tn(a, b):
    return lax.dot_general(a.astype(BF16), b.astype(BF16), (((0,), (0,)), ((), ())),
                           preferred_element_type=F32)


def _split2(x):
    hi = x.astype(BF16)
    lo = (x - hi.astype(F32)).astype(BF16)
    return hi, lo


def _dot_exact_lhs(a, x):
    hi = x.astype(BF16)
    r = x - hi.astype(F32)
    mid = r.astype(BF16)
    lo = (r - mid.astype(F32)).astype(BF16)
    a = a.astype(BF16)
    return (jnp.dot(a, hi, preferred_element_type=F32) + jnp.dot(a, mid, preferred_element_type=F32)
            + jnp.dot(a, lo, preferred_element_type=F32))


def _dot_hi(a, b):
    ah, al = _split2(a)
    bh, bl = _split2(b)
    return (jnp.dot(ah, bh, preferred_element_type=F32) + jnp.dot(ah, bl, preferred_element_type=F32)
            + jnp.dot(al, bh, preferred_element_type=F32))


def _rms(x, g):
    return x * lax.rsqrt(jnp.mean(x * x, axis=-1, keepdims=True) + NORM_EPS) * g


def _sigmoid(x):
    return 0.5 * jnp.tanh(0.5 * x) + 0.5


def _silu(x):
    return x * _sigmoid(x)


def _softplus(x):
    return jnp.maximum(x, 0.0) + jnp.log(1.0 + jnp.exp(-jnp.abs(x)))


def _log_sigmoid(x):
    return -_softplus(-x)


def _iota(shape, axis):
    return lax.broadcasted_iota(I32, shape, axis)


SUBLANES = 8
ROW_TILES = D_MODEL // LANES


def _store_token_tiles(ref, x):
    t = x.shape[0]
    for s in range(ROW_TILES):
        ref[pl.ds(s, t, stride=ROW_TILES), :] = x[:, s * LANES:(s + 1) * LANES]


def _load_token_tiles(ref, t):
    return jnp.concatenate([ref[pl.ds(s, t, stride=ROW_TILES), :] for s in range(ROW_TILES)], axis=-1)


def _chunk_tril(n, strict=False):
    r = _iota((n, n), 0)
    c = _iota((n, n), 1)
    same = (r // CHUNK) == (c // CHUNK)
    return same & ((c < r) if strict else (c <= r))


def _inproj_kernel(x_ref, g_ref, w_ref, *out_refs):
    h = _rms(x_ref[...], g_ref[...]).astype(BF16)
    off = 0
    for o_ref in out_refs:
        wd = o_ref.shape[-1]
        o_ref[...] = jnp.dot(h, w_ref[:, off:off + wd], preferred_element_type=F32)
        off += wd


def _inproj(x, g, w, tm=256):
    n = x.shape[0]
    nc = w.shape[1]
    return pl.pallas_call(
        _inproj_kernel,
        grid=(n // tm,),
        in_specs=[pl.BlockSpec((tm, D_MODEL), lambda i: (i, 0)),
                  pl.BlockSpec((1, D_MODEL), lambda i: (0, 0)),
                  pl.BlockSpec((D_MODEL, nc), lambda i: (0, 0), pipeline_mode=pl.Buffered(1))],
        out_specs=[pl.BlockSpec((tm, wd), lambda i: (i, 0)) for wd in OUT_WIDTHS],
        out_shape=[jax.ShapeDtypeStruct((n, wd), F32) for wd in OUT_WIDTHS],
        compiler_params=_params(1),
        name="inproj",
    )(x, g, w)


def _rope_kernel(pos_ref, invf_ref, cos_ref, sin_ref):
    ang = pos_ref[...].astype(F32) * invf_ref[...]
    d = _iota(ang.shape, 1) % SWA_HEAD_DIM
    half = ROT_DIM // 2
    cos_ref[...] = jnp.where(d < ROT_DIM, jnp.cos(ang), 1.0)
    s = jnp.sin(ang)
    sin_ref[...] = jnp.where(d < half, -s, jnp.where(d < ROT_DIM, s, 0.0))


def _rope_tables(positions, tm=1024):
    n = positions.size
    tm = min(tm, n)
    inv_freq = 1.0 / (ROPE_THETA ** (jnp.arange(0, ROT_DIM, 2, dtype=F32) / ROT_DIM))
    lane_freq = jnp.tile(jnp.concatenate([inv_freq, inv_freq, jnp.zeros(SWA_HEAD_DIM - ROT_DIM, F32)]),
                         LANES // SWA_HEAD_DIM)[None]
    return pl.pallas_call(
        _rope_kernel,
        grid=(n // tm,),
        in_specs=[pl.BlockSpec((tm, 1), lambda i: (i, 0)),
                  pl.BlockSpec((1, LANES), lambda i: (0, 0))],
        out_specs=[pl.BlockSpec((tm, LANES), lambda i: (i, 0))] * 2,
        out_shape=[jax.ShapeDtypeStruct((n, LANES), F32)] * 2,
        compiler_params=_params(1),
        name="rope_tables",
    )(positions.reshape(n, 1), lane_freq)


def _dot_exact_rhs(x, b):
    hi, lo = _split2(x)
    return jnp.dot(hi, b, preferred_element_type=F32) + jnp.dot(lo, b, preferred_element_type=F32)


def _rope_matrices():
    m = _iota((LANES, LANES), 0)
    l = _iota((LANES, LANES), 1)
    d = l % SWA_HEAD_DIM
    half = ROT_DIM // 2
    same_head = (m // SWA_HEAD_DIM) == (l // SWA_HEAD_DIM)
    partner = ((d < half) & (m == l + half)) | ((d >= half) & (d < ROT_DIM) & (m == l - half))
    return jnp.where(same_head, 1.0, 0.0).astype(BF16), jnp.where(partner, 1.0, 0.0).astype(BF16)


def _norm_rope(x, gain, cos, sin, same_head, partner):
    ms = _dot_exact_rhs(x * x, same_head) * (1.0 / SWA_HEAD_DIM)
    xn = x * lax.rsqrt(ms + NORM_EPS) * gain
    return xn * cos + _dot_exact_rhs(xn, partner) * sin


def _swa_kernel(sink_ref, q_ref, kv_ref, cos_ref, sin_ref, qn_ref, kn_ref, o_ref, kprev_ref, vprev_ref):
    j = pl.program_id(1)
    w = SWA_WINDOW
    hd = SWA_HEAD_DIM
    group = SWA_Q_HEADS // SWA_KV_HEADS

    @pl.when(j == 0)
    def _():
        kprev_ref[...] = jnp.zeros_like(kprev_ref)
        vprev_ref[...] = jnp.zeros_like(vprev_ref)

    cos = cos_ref[...]
    sin = sin_ref[...]
    same_head, partner = _rope_matrices()
    kc = _norm_rope(kv_ref[:, :SWA_KV], kn_ref[...], cos, sin, same_head, partner).astype(BF16)
    vc = kv_ref[:, SWA_KV:].astype(BF16)
    k_all = jnp.concatenate([kprev_ref[...], kc], axis=0)
    v_all = jnp.concatenate([vprev_ref[...], vc], axis=0)
    kprev_ref[...] = kc
    vprev_ref[...] = vc

    rows = group * w
    qi = _iota((rows, 2 * w), 0) % w
    kj = _iota((rows, 2 * w), 1)
    first_key = jnp.where(j > 0, 0, w)
    mask = (kj > qi) & (kj <= qi + w) & (kj >= first_key)
    head_of_row = _iota((rows, 1), 0) // w
    qrs = [_norm_rope(q_ref[:, c * LANES:(c + 1) * LANES], qn_ref[...], cos, sin, same_head, partner)
           for c in range(SWA_Q // LANES)]
    per_block = LANES // hd
    qgs, sinks = [], []
    for g in range(SWA_KV_HEADS):
        heads = range(g * group, (g + 1) * group)
        qgs.append(jnp.concatenate(
            [qrs[h // per_block][:, (h % per_block) * hd:(h % per_block + 1) * hd] for h in heads], axis=0))
        sk = jnp.full((rows, 1), sink_ref[g * group], F32)
        for i in range(1, group):
            sk = jnp.where(head_of_row == i, sink_ref[g * group + i], sk)
        sinks.append(sk)
    ss = [jnp.where(mask, _dot_nt(qgs[g], k_all[:, g * hd:(g + 1) * hd]) * (hd ** -0.5), MASK_VALUE)
          for g in range(SWA_KV_HEADS)]
    ms = [jnp.maximum(jnp.max(s, axis=-1, keepdims=True), sk) for s, sk in zip(ss, sinks)]
    es = [jnp.exp(s - m).astype(BF16) for s, m in zip(ss, ms)]
    ones = jnp.ones((2 * w, hd), BF16)
    dens = [jnp.dot(e, ones, preferred_element_type=F32) + jnp.exp(sk - m) for e, sk, m in zip(es, sinks, ms)]
    for g in range(SWA_KV_HEADS):
        og = jnp.dot(es[g], v_all[:, g * hd:(g + 1) * hd], preferred_element_type=F32) / dens[g]
        for i in range(group):
            h = g * group + i
            o_ref[:, h * hd:(h + 1) * hd] = og[i * w:(i + 1) * w]


def _swa(aq, akv, cos_t, sin_t, q_norm, k_norm, sinks, batch, seq):
    n = aq.shape[0]
    w = SWA_WINDOW
    nq = seq // w
    cur = lambda b, j: (b * nq + j, 0)
    rep = LANES // SWA_HEAD_DIM
    return pl.pallas_call(
        _swa_kernel,
        grid=(batch, nq),
        in_specs=[pl.BlockSpec(memory_space=pltpu.SMEM),
                  pl.BlockSpec((w, SWA_Q), cur),
                  pl.BlockSpec((w, 2 * SWA_KV), cur),
                  pl.BlockSpec((w, LANES), cur),
                  pl.BlockSpec((w, LANES), cur),
                  pl.BlockSpec((1, LANES), lambda b, j: (0, 0)),
                  pl.BlockSpec((1, LANES), lambda b, j: (0, 0))],
        out_specs=pl.BlockSpec((w, SWA_Q), cur),
        out_shape=jax.ShapeDtypeStruct((n, SWA_Q), F32),
        scratch_shapes=[pltpu.VMEM((w, SWA_KV), BF16), pltpu.VMEM((w, SWA_KV), BF16)],
        compiler_params=_params(2),
        name="swa",
    )(sinks, aq, akv, cos_t, sin_t, jnp.tile(q_norm, rep)[None], jnp.tile(k_norm, rep)[None])


def _gla_kernel(qk_ref, v_ref, r_ref, sm_ref, wa_ref, ba_ref, gn_ref, o_ref, state_ref):
    @pl.when(pl.program_id(1) == 0)
    def _():
        state_ref[...] = jnp.zeros_like(state_ref)

    tc = qk_ref.shape[0]
    la = _log_sigmoid(_dot(sm_ref[...], wa_ref[...]) + ba_ref[...]) * (1.0 / GLA_GATE_NORM)
    tri = jnp.where(_chunk_tril(tc), 1.0, 0.0).astype(BF16)
    b = _dot_exact_lhs(tri, la)
    causal = _iota((CHUNK, CHUNK), 1) <= _iota((CHUNK, CHUNK), 0)
    n_chunks = tc // CHUNK
    qes, dcols, intra, upd = [], [], {}, {}
    for c in range(n_chunks):
        rows = slice(c * CHUNK, (c + 1) * CHUNK)
        bc = b[rows]
        b_last = bc[CHUNK - 1:CHUNK]
        k = qk_ref[rows, GLA_K:]
        qe = qk_ref[rows, :GLA_K] * (GLA_DK ** -0.5) * jnp.exp(bc)
        ke = k * jnp.exp(-bc)
        kd = k * jnp.exp(b_last - bc)
        qes.append(qe)
        dcols.append(jnp.transpose(jnp.broadcast_to(jnp.exp(b_last), (GLA_DV, GLA_K))))
        for h in range(GLA_HEADS):
            ks = slice(h * GLA_DK, (h + 1) * GLA_DK)
            vh = v_ref[rows, h * GLA_DV:(h + 1) * GLA_DV]
            att = jnp.where(causal, _dot_nt(qe[:, ks], ke[:, ks]), 0.0)
            intra[c, h] = _dot(att, vh)
            upd[c, h] = _dot_tn(kd[:, ks], vh)
    for h in range(GLA_HEADS):
        ks = slice(h * GLA_DK, (h + 1) * GLA_DK)
        vs = slice(h * GLA_DV, (h + 1) * GLA_DV)
        st = state_ref[h]
        for c in range(n_chunks):
            rows = slice(c * CHUNK, (c + 1) * CHUNK)
            o = intra[c, h] + _dot(qes[c][:, ks], st)
            st = dcols[c][ks] * st + upd[c, h]
            o_ref[rows, vs] = _rms(o, gn_ref[...]) * _silu(r_ref[rows, vs])
        state_ref[h] = st


def _gla(bqk, bv, br, small, wa2, ba, gn, batch, seq, tc=256):
    n = bqk.shape[0]
    nt = seq // tc
    blk = lambda b, j: (b * nt + j, 0)
    const = lambda b, j: (0, 0)
    wa_pad = jnp.zeros((LANES, GLA_K), F32).at[SMALL_LR:SMALL_LR + GLA_GATE_RANK].set(wa2)
    return pl.pallas_call(
        _gla_kernel,
        grid=(batch, nt),
        in_specs=[pl.BlockSpec((tc, 2 * GLA_K), blk),
                  pl.BlockSpec((tc, GLA_V), blk),
                  pl.BlockSpec((tc, GLA_V), blk),
                  pl.BlockSpec((tc, LANES), blk),
                  pl.BlockSpec((LANES, GLA_K), const),
                  pl.BlockSpec((1, GLA_K), const),
                  pl.BlockSpec((1, GLA_DV), const)],
        out_specs=pl.BlockSpec((tc, GLA_V), blk),
        out_shape=jax.ShapeDtypeStruct((n, GLA_V), F32),
        scratch_shapes=[pltpu.VMEM((GLA_HEADS, GLA_DK, GLA_DV), F32)],
        compiler_params=_params(2),
        name="gla",
    )(bqk, bv, br, small, wa_pad, ba[None], gn[None])


def _l2(x):
    return x * lax.rsqrt(jnp.sum(x * x, axis=-1, keepdims=True) + NORM_EPS)


def _gdn_kernel(x_ref, xp_ref, g_ref, sm_ref, cw_ref, al_ref, dt_ref, gn_ref, o_ref, state_ref):
    first = pl.program_id(1) == 0

    @pl.when(first)
    def _():
        state_ref[...] = jnp.zeros_like(state_ref)

    tc = x_ref.shape[0]
    halo = xp_ref.shape[0]
    x = x_ref[...]
    xprev = jnp.where(first, 0.0, xp_ref[...])
    sm = sm_ref[...]
    row = _iota((halo, 1), 0)
    conv = cw_ref[GDN_CONV - 1:GDN_CONV] * x
    for t in range(1, GDN_CONV):
        rolled = pltpu.roll(x, t, 0)
        head = jnp.where(row < t, pltpu.roll(xprev, t, 0), rolled[:halo])
        shifted = jnp.concatenate([head, rolled[halo:]], axis=0)
        conv = conv + cw_ref[GDN_CONV - 1 - t:GDN_CONV - t] * shifted
    qkv = _silu(conv)

    beta_all = _sigmoid(sm)
    g_all = -jnp.exp(al_ref[...]) * _softplus(sm + dt_ref[...])
    tri = jnp.where(_chunk_tril(tc), 1.0, 0.0).astype(BF16)
    big_g = _dot_exact_lhs(tri, g_all)
    g_rows = jnp.transpose(big_g)

    r = _iota((CHUNK, CHUNK), 0)
    cidx = _iota((CHUNK, CHUNK), 1)
    causal = cidx <= r
    strict = cidx < r
    eye = jnp.where(cidx == r, 1.0, 0.0)

    units = [(c, h) for c in range(tc // CHUNK) for h in range(GDN_HEADS)]
    qs, ks, egs, g_lasts, gcs, decays, lows, rhss = [], [], [], [], [], [], [], []
    for c, h in units:
        rows = slice(c * CHUNK, (c + 1) * CHUNK)
        q = _l2(qkv[rows, h * GDN_DK:(h + 1) * GDN_DK]) * (GDN_DK ** -0.5)
        k = _l2(qkv[rows, GDN_K + h * GDN_DK:GDN_K + (h + 1) * GDN_DK])
        v = qkv[rows, 2 * GDN_K + h * GDN_DV:2 * GDN_K + (h + 1) * GDN_DV]
        beta = beta_all[rows, SMALL_BETA + h:SMALL_BETA + h + 1]
        gc = big_g[rows, SMALL_A + h:SMALL_A + h + 1]
        g_row = g_rows[SMALL_A + h:SMALL_A + h + 1, c * CHUNK:(c + 1) * CHUNK]
        decay = jnp.where(causal, jnp.exp(jnp.where(causal, gc - g_row, 0.0)), 0.0)
        eg = jnp.exp(gc)
        qs.append(q)
        ks.append(k)
        egs.append(eg)
        gcs.append(gc)
        g_lasts.append(gc[CHUNK - 1:CHUNK])
        decays.append(decay)
        lows.append(jnp.where(strict, beta * _dot_nt(k, k) * decay, 0.0))
        rhss.append(jnp.concatenate([v * beta, k * (beta * eg)], axis=-1))
    invs = [eye - low for low in lows]
    pws = lows
    for _ in range(int(math.log2(CHUNK)) - 1):
        pws = [_dot(pw, pw) for pw in pws]
        invs = [inv + _dot(inv, pw) for inv, pw in zip(invs, pws)]
    sols = [_dot(inv, rhs) for inv, rhs in zip(invs, rhss)]
    atts = [_dot_nt(q, k) * decay for q, k, decay in zip(qs, ks, decays)]
    k_decs = [k * jnp.exp(gl - gc) for k, gl, gc in zip(ks, g_lasts, gcs)]
    kws = [_dot_tn(k_dec, sol) for k_dec, sol in zip(k_decs, sols)]
    aws = [_dot(att, sol) for att, sol in zip(atts, sols)]

    for i, (c, h) in enumerate(units):
        rows = slice(c * CHUNK, (c + 1) * CHUNK)
        st = state_ref[h]
        lhs = jnp.concatenate([kws[i][:, GDN_DV:], qs[i] * egs[i] - aws[i][:, GDN_DV:]], axis=0)
        prod = _dot(lhs, st)
        o = prod[GDN_DK:] + aws[i][:, :GDN_DV]
        state_ref[h] = jnp.exp(g_lasts[i]) * st - prod[:GDN_DK] + kws[i][:, :GDN_DV]
        vs = slice(h * GDN_DV, (h + 1) * GDN_DV)
        o_ref[rows, vs] = _rms(o, gn_ref[...]) * _silu(g_ref[rows, vs])


def _gdn(cqkv, cg, small, conv_w, a_log, dt_bias, gn, batch, seq, tc=512, halo=8):
    n = cqkv.shape[0]
    nt = seq // tc
    blk = lambda b, j: (b * nt + j, 0)
    const = lambda b, j: (0, 0)
    prev = lambda b, j: (jnp.maximum((b * nt + j) * (tc // halo) - 1, 0), 0)
    al_row = jnp.zeros((1, LANES), F32).at[0, SMALL_A:SMALL_A + GDN_HEADS].set(a_log)
    dt_row = jnp.zeros((1, LANES), F32).at[0, SMALL_A:SMALL_A + GDN_HEADS].set(dt_bias)
    return pl.pallas_call(
        _gdn_kernel,
        grid=(batch, nt),
        in_specs=[pl.BlockSpec((tc, GDN_QKV), blk),
                  pl.BlockSpec((halo, GDN_QKV), prev),
                  pl.BlockSpec((tc, GDN_V), blk),
                  pl.BlockSpec((tc, LANES), blk),
                  pl.BlockSpec((GDN_CONV, GDN_QKV), const),
                  pl.BlockSpec((1, LANES), const),
                  pl.BlockSpec((1, LANES), const),
                  pl.BlockSpec((1, GDN_DV), const)],
        out_specs=pl.BlockSpec((tc, GDN_V), blk),
        out_shape=jax.ShapeDtypeStruct((n, GDN_V), F32),
        scratch_shapes=[pltpu.VMEM((GDN_HEADS, GDN_DK, GDN_DV), F32)],
        compiler_params=_params(2),
        name="gdn",
    )(cqkv, cqkv, cg, small, conv_w, al_row, dt_row, gn[None])


def _first_argmax(x, lane):
    m = jnp.max(x, axis=-1, keepdims=True)
    idx = jnp.min(jnp.where(x == m, lane.astype(F32), float(LANES)), axis=-1, keepdims=True)
    return m, idx.astype(I32)


def _merge_kernel(ya_ref, yb_ref, yc_ref, gt_ref, x_ref, wa_ref, wb_ref, wc_ref, wo_ref, fn_ref,
                  wr_ref, br_ref, x1_ref, h2_ref, ri_ref, rw_ref, cnt_ref, carry_ref):
    @pl.when(pl.program_id(0) == 0)
    def _():
        carry_ref[...] = jnp.zeros_like(carry_ref)

    d = D_MODEL
    ya = ya_ref[...].astype(BF16)
    yb = yb_ref[...].astype(BF16)
    yc = yc_ref[...].astype(BF16)
    cw = 2 * LANES
    merged = []
    for c in range(d // cw):
        lo = c * cw
        merged.append((
            _sigmoid(gt_ref[:, lo:lo + cw]) * jnp.dot(ya, wa_ref[:, lo:lo + cw], preferred_element_type=F32)
            + _sigmoid(gt_ref[:, d + lo:d + lo + cw]) * jnp.dot(yb, wb_ref[:, lo:lo + cw], preferred_element_type=F32)
            + _sigmoid(gt_ref[:, 2 * d + lo:2 * d + lo + cw]) * jnp.dot(yc, wc_ref[:, lo:lo + cw], preferred_element_type=F32)
        ).astype(BF16))
    merged = jnp.concatenate(merged, axis=-1)
    x1 = x_ref[...] + jnp.dot(merged, wo_ref[...], preferred_element_type=F32)
    x1_ref[...] = x1
    h2 = _rms(x1, fn_ref[...])
    _store_token_tiles(h2_ref, h2)

    logits = _dot_hi(h2, wr_ref[...]) + br_ref[...]
    tm = logits.shape[0]
    lane = _iota((tm, LANES), 1)
    neg = -jnp.inf
    is_c = lane < N_GROUPS
    cm, g_idx = _first_argmax(jnp.where(is_c, logits, neg), lane)
    g_prob = 1.0 / jnp.sum(jnp.where(is_c, jnp.exp(logits - cm), 0.0), axis=-1, keepdims=True)
    sel = (lane >= N_GROUPS) & (((lane - N_GROUPS) // EXPERTS_PER_GROUP) == g_idx)
    fm = jnp.max(jnp.where(sel, logits, neg), axis=-1, keepdims=True)
    ef = jnp.where(sel, jnp.exp(logits - fm), 0.0)
    p1, i1 = _first_argmax(jnp.where(sel, ef, neg), lane)
    p2, i2 = _first_argmax(jnp.where(sel & (lane != i1), ef, neg), lane)
    w1 = g_prob * p1 / (p1 + p2)
    w2 = g_prob * p2 / (p1 + p2)
    e1 = i1 - N_GROUPS
    e2 = i2 - N_GROUPS

    oh = jnp.where((lane == e1) | (lane == e2 + N_EXPERTS), 1.0, 0.0)
    stril = jnp.where(_iota((tm, tm), 1) < _iota((tm, tm), 0), 1.0, 0.0).astype(BF16)
    before = jnp.dot(stril, oh.astype(BF16), preferred_element_type=F32)
    tot = jnp.sum(oh, axis=0, keepdims=True)
    tot_first = jnp.where(lane[:1] < N_EXPERTS, tot, 0.0)
    carry = carry_ref[...]
    base = carry + pltpu.roll(carry + tot_first, N_EXPERTS, 1)
    ranks = oh * (before + base)
    rank1 = jnp.sum(jnp.where(lane < N_EXPERTS, ranks, 0.0), axis=-1, keepdims=True)
    rank2 = jnp.sum(jnp.where(lane < N_EXPERTS, 0.0, ranks), axis=-1, keepdims=True)
    new_carry = carry + tot_first + pltpu.roll(tot - tot_first, LANES - N_EXPERTS, 1)
    carry_ref[...] = new_carry
    cnt_ref[...] = jnp.broadcast_to(new_carry, cnt_ref.shape).astype(I32)

    ri_ref[...] = jnp.where(lane == 0, e1, jnp.where(lane == 1, e2, jnp.where(
        lane == 2, rank1.astype(I32), jnp.where(lane == 3, rank2.astype(I32), 0))))
    rw_ref[...] = jnp.where(lane == 0, w1, jnp.where(lane == 1, w2, 0.0))


def _merge(ya, yb, yc, gates, x, wa, wb, wc, wo, fn, wr, br, tm=512):
    n = x.shape[0]
    d = D_MODEL
    blk = lambda i: (i, 0)
    const = lambda i: (0, 0)
    return pl.pallas_call(
        _merge_kernel,
        grid=(n // tm,),
        in_specs=[pl.BlockSpec((tm, SWA_Q), blk), pl.BlockSpec((tm, GLA_V), blk),
                  pl.BlockSpec((tm, GDN_V), blk), pl.BlockSpec((tm, 3 * d), blk),
                  pl.BlockSpec((tm, d), blk),
                  pl.BlockSpec((SWA_Q, d), const), pl.BlockSpec((GLA_V, d), const),
                  pl.BlockSpec((GDN_V, d), const), pl.BlockSpec((d, d), const),
                  pl.BlockSpec((1, d), const), pl.BlockSpec((d, LANES), const),
                  pl.BlockSpec((1, LANES), const)],
        out_specs=[pl.BlockSpec((tm, d), blk), pl.BlockSpec((tm * SUBLANES, LANES), blk),
                   pl.BlockSpec((tm, LANES), blk), pl.BlockSpec((tm, LANES), blk),
                   pl.BlockSpec((8, LANES), const)],
        out_shape=[jax.ShapeDtypeStruct((n, d), F32), jax.ShapeDtypeStruct((n * SUBLANES, LANES), F32),
                   jax.ShapeDtypeStruct((n, LANES), I32), jax.ShapeDtypeStruct((n, LANES), F32),
                   jax.ShapeDtypeStruct((8, LANES), I32)],
        scratch_shapes=[pltpu.VMEM((1, LANES), F32)],
        compiler_params=_params(1),
        name="merge_route",
    )(ya, yb, yc, gates, x, wa, wb, wc, wo, fn, wr, br)


def _dispatch_kernel(fill_ref, slot_ref, h_ref, xs_ref, zbuf, sem, zsem):
    td = h_ref.shape[0] // SUBLANES
    n_blocks = xs_ref.shape[0] // (MOE_ROWS * SUBLANES)

    def zero_fill(start):
        def run(cp):
            if start:
                cp.start()
            else:
                cp.wait()

        def pad_rows(e, carry):
            lo = fill_ref[e]
            n_pad = fill_ref[N_EXPERTS + e] - lo
            bit = MOE_ROWS // 2
            while bit >= 1:
                off = lo + (n_pad // (2 * bit)) * (2 * bit)

                @pl.when((n_pad & bit) != 0)
                def _(off=off, bit=bit):
                    run(pltpu.make_async_copy(
                        zbuf.at[pl.ds(0, bit * SUBLANES)],
                        xs_ref.at[pl.ds(pl.multiple_of(off * SUBLANES, SUBLANES), bit * SUBLANES)], zsem))

                bit //= 2
            return carry

        lax.fori_loop(0, N_EXPERTS, pad_rows, 0)

        def unused_block(b, carry):
            run(pltpu.make_async_copy(
                zbuf, xs_ref.at[pl.ds(pl.multiple_of(b * (MOE_ROWS * SUBLANES), SUBLANES), MOE_ROWS * SUBLANES)],
                zsem))
            return carry

        lax.fori_loop(fill_ref[2 * N_EXPERTS], n_blocks, unused_block, 0)

    @pl.when(pl.program_id(0) == 0)
    def _():
        zbuf[...] = jnp.zeros_like(zbuf)
        zero_fill(True)
        zero_fill(False)

    def rows(start):
        def body(t, carry):
            src = h_ref.at[pl.ds(pl.multiple_of(t * SUBLANES, SUBLANES), SUBLANES)]
            for k in range(TOP_K):
                s = slot_ref[0, 0, TOP_K * t + k]
                cp = pltpu.make_async_copy(
                    src, xs_ref.at[pl.ds(pl.multiple_of(s * SUBLANES, SUBLANES), SUBLANES)], sem)
                if start:
                    cp.start(priority=k % 2)
                else:
                    cp.wait()
            return carry

        lax.fori_loop(0, td, body, 0, unroll=4 if start else 8)

    rows(True)
    rows(False)


def _dispatch(h2, slots, fill, n_slots, td=1024):
    n = h2.shape[0] // SUBLANES
    td = min(td, n)
    grid_spec = pltpu.PrefetchScalarGridSpec(
        num_scalar_prefetch=1,
        grid=(n // td,),
        in_specs=[pl.BlockSpec((1, 1, TOP_K * td), lambda i, fill: (i, 0, 0), memory_space=pltpu.SMEM),
                  pl.BlockSpec((td * SUBLANES, LANES), lambda i, fill: (i, 0))],
        out_specs=pl.BlockSpec(memory_space=pl.ANY),
        scratch_shapes=[pltpu.VMEM((MOE_ROWS * SUBLANES, LANES), F32), pltpu.SemaphoreType.DMA(()),
                        pltpu.SemaphoreType.DMA(())],
    )
    return pl.pallas_call(
        _dispatch_kernel,
        grid_spec=grid_spec,
        out_shape=jax.ShapeDtypeStruct((n_slots * SUBLANES, LANES), F32),
        compiler_params=_params(1),
        name="dispatch",
    )(fill, slots.reshape(n // td, 1, TOP_K * td), h2)


def _expert_kernel(be_ref, nxt_ref, par_ref, xs_ref, wg_hbm, wu_hbm, wd_hbm, ys_ref,
                   wg_f, wu_f, wd_f, wg_s, wu_s, wd_s, sems, *, layer):
    i = pl.program_id(0)
    e = be_ref[i]
    slot = par_ref[i]
    first_of_expert = jnp.logical_or(i == 0, e != be_ref[jnp.maximum(i - 1, 0)])

    def weight_copies(expert, buf):
        return (pltpu.make_async_copy(wg_hbm.at[layer, expert], wg_f.at[buf], sems.at[buf, 0]),
                pltpu.make_async_copy(wu_hbm.at[layer, expert], wu_f.at[buf], sems.at[buf, 1]),
                pltpu.make_async_copy(wd_hbm.at[layer, expert], wd_f.at[buf], sems.at[buf, 2]))

    @pl.when(i == 0)
    def _():
        for cp in weight_copies(e, slot):
            cp.start()

    @pl.when(first_of_expert)
    def _():
        @pl.when(nxt_ref[i] != e)
        def _():
            for cp in weight_copies(nxt_ref[i], 1 - slot):
                cp.start()

        for cp in weight_copies(e, slot):
            cp.wait()
        wg_s[...] = wg_f[slot].astype(BF16)
        wu_s[...] = wu_f[slot].astype(BF16)
        wd_s[...] = wd_f[slot].astype(BF16)

    half = xs_ref.shape[0] // 2
    th = half // SUBLANES
    xs = [_load_token_tiles(xs_ref.at[pl.ds(p * half, half)], th).astype(BF16) for p in range(2)]
    gs = [jnp.dot(x, wg_s[...], preferred_element_type=F32) for x in xs]
    us = [jnp.dot(x, wu_s[...], preferred_element_type=F32) for x in xs]
    acts = [(_silu(g) * u).astype(BF16) for g, u in zip(gs, us)]
    for p in range(2):
        _store_token_tiles(ys_ref.at[pl.ds(p * half, half)],
                           jnp.dot(acts[p], wd_s[...], preferred_element_type=F32))


def _experts(xs, block_expert, next_expert, block_parity, wg, wu, wd, layer):
    n_slots = xs.shape[0] // SUBLANES
    tb = MOE_ROWS
    blk = lambda i, be, nx, par: (i, 0)
    grid_spec = pltpu.PrefetchScalarGridSpec(
        num_scalar_prefetch=3,
        grid=(n_slots // tb,),
        in_specs=[pl.BlockSpec((tb * SUBLANES, LANES), blk),
                  pl.BlockSpec(memory_space=pl.ANY), pl.BlockSpec(memory_space=pl.ANY),
                  pl.BlockSpec(memory_space=pl.ANY)],
        out_specs=pl.BlockSpec((tb * SUBLANES, LANES), blk),
        scratch_shapes=[pltpu.VMEM((2, D_MODEL, EXPERT_FF), F32), pltpu.VMEM((2, D_MODEL, EXPERT_FF), F32),
                        pltpu.VMEM((2, EXPERT_FF, D_MODEL), F32),
                        pltpu.VMEM((D_MODEL, EXPERT_FF), BF16), pltpu.VMEM((D_MODEL, EXPERT_FF), BF16),
                        pltpu.VMEM((EXPERT_FF, D_MODEL), BF16),
                        pltpu.SemaphoreType.DMA((2, 3))],
    )
    return pl.pallas_call(
        functools.partial(_expert_kernel, layer=layer),
        grid_spec=grid_spec,
        out_shape=jax.ShapeDtypeStruct((n_slots * SUBLANES, LANES), F32),
        compiler_params=_params(1),
        name="experts",
    )(block_expert, next_expert, block_parity, xs, wg, wu, wd)


def _combine_kernel(slot_ref, next_slot_ref, x1_ref, rw_ref, p_ref, ys_ref, pn_ref, wpg_ref, wpl_ref, o_ref,
                    ybuf, sems):
    i = pl.program_id(0)
    tp = x1_ref.shape[0]
    cur = i % 2

    def rows(sref, buf, start):
        def body(t, carry):
            for k in range(TOP_K):
                s = sref[0, 0, TOP_K * t + k]
                cp = pltpu.make_async_copy(
                    ys_ref.at[pl.ds(pl.multiple_of(s * SUBLANES, SUBLANES), SUBLANES)],
                    ybuf.at[buf, k, pl.ds(pl.multiple_of(t * SUBLANES, SUBLANES), SUBLANES)], sems.at[buf])
                if start:
                    cp.start(priority=k % 2)
                else:
                    cp.wait()
            return carry

        lax.fori_loop(0, tp, body, 0, unroll=4 if start else 8)

    @pl.when(i == 0)
    def _():
        rows(slot_ref, 0, True)

    rows(slot_ref, cur, False)

    pending = list(range(tp))

    def issue_next(n_tokens):
        for _ in range(n_tokens):
            t = pending.pop(0)
            for k in range(TOP_K):
                s = next_slot_ref[0, 0, TOP_K * t + k]
                pltpu.make_async_copy(
                    ys_ref.at[pl.ds(pl.multiple_of(s * SUBLANES, SUBLANES), SUBLANES)],
                    ybuf.at[1 - cur, k, pl.ds(t * SUBLANES, SUBLANES)], sems.at[1 - cur]).start(priority=k % 2)

    n_stage = D_MODEL // (2 * LANES) + 1
    rw = rw_ref[...]
    x2 = (x1_ref[...] + rw[:, 0:1] * _load_token_tiles(ybuf.at[cur, 0], tp)
          + rw[:, 1:2] * _load_token_tiles(ybuf.at[cur, 1], tp))
    issue_next(tp // n_stage)
    h3 = _rms(x2, pn_ref[...]).astype(BF16)
    pe = p_ref[0].astype(BF16)
    cw = 2 * LANES
    for c in range(D_MODEL // cw):
        cols = slice(c * cw, (c + 1) * cw)
        gate = _sigmoid(jnp.dot(h3, wpg_ref[:, cols], preferred_element_type=F32))
        o_ref[:, cols] = x2[:, cols] + gate * jnp.dot(pe, wpl_ref[:, cols], preferred_element_type=F32)
        issue_next(tp // n_stage if c + 1 < D_MODEL // cw else len(pending))

    @pl.when(i + 1 == pl.num_programs(0))
    def _():
        rows(next_slot_ref, 1 - cur, False)


def _combine_ple(x1, rw, slots, ys, p, layer, pn, wpg, wpl, tp=256):
    n = x1.shape[0]
    d = D_MODEL
    blk = lambda i: (i, 0)
    const = lambda i: (0, 0)
    nsteps = n // tp
    slot_blocks = slots.reshape(nsteps, 1, TOP_K * tp)
    return pl.pallas_call(
        _combine_kernel,
        grid=(nsteps,),
        in_specs=[pl.BlockSpec((1, 1, TOP_K * tp), lambda i: (i, 0, 0), memory_space=pltpu.SMEM),
                  pl.BlockSpec((1, 1, TOP_K * tp), lambda i: (jnp.minimum(i + 1, nsteps - 1), 0, 0),
                               memory_space=pltpu.SMEM),
                  pl.BlockSpec((tp, d), blk), pl.BlockSpec((tp, LANES), blk),
                  pl.BlockSpec((1, tp, PLE_DIM), lambda i: (layer, i, 0)),
                  pl.BlockSpec(memory_space=pl.ANY),
                  pl.BlockSpec((1, d), const), pl.BlockSpec((d, d), const),
                  pl.BlockSpec((PLE_DIM, d), const)],
        out_specs=pl.BlockSpec((tp, d), blk),
        out_shape=jax.ShapeDtypeStruct((n, d), F32),
        scratch_shapes=[pltpu.VMEM((2, TOP_K, tp * SUBLANES, LANES), F32), pltpu.SemaphoreType.DMA((2,))],
        compiler_params=_params(1),
        name="combine_ple",
    )(slot_blocks, slot_blocks, x1, rw, p, ys, pn, wpg, wpl)


def _pack_w_in(w_in):
    o = IN_OFFSETS
    w = w_in.astype(BF16)
    pad = jnp.zeros((D_MODEL, LANES - GLA_GATE_RANK - 2 * GDN_HEADS), BF16)
    cols = [w[:, :o[6]], w[:, o[7]:o[9]], w[:, o[11]:], w[:, o[6]:o[7]], w[:, o[9]:o[11]], pad]
    return jnp.concatenate(cols, axis=1)


def _layer(x, p, layer, cos_t, sin_t, batch, seq, attn_norm, w_in, q_norm, k_norm, sinks, gla_wa2, gla_ba,
           gla_norm, gdn_conv, gdn_a_log, gdn_dt_bias, gdn_norm, w_br_a, w_br_b, w_br_c, w_o,
           ffn_norm, w_coarse, b_coarse, w_fine, b_fine, w_gate_e, w_up_e, w_down_e,
           ple_norm, w_ple_gate, w_ple):
    n = x.shape[0]
    aq, akv, bqk, bv, br, cqkv, cg, gates, small = _inproj(x, attn_norm[None], _pack_w_in(w_in))
    ya = _swa(aq, akv, cos_t, sin_t, q_norm, k_norm, sinks, batch, seq)
    yb = _gla(bqk, bv, br, small, gla_wa2, gla_ba, gla_norm, batch, seq)
    yc = _gdn(cqkv, cg, small, gdn_conv, gdn_a_log, gdn_dt_bias, gdn_norm, batch, seq)

    wr = jnp.zeros((D_MODEL, LANES), F32).at[:, :N_GROUPS].set(w_coarse)
    wr = wr.at[:, N_GROUPS:N_GROUPS + N_EXPERTS].set(w_fine)
    brow = jnp.zeros((1, LANES), F32).at[0, :N_GROUPS].set(b_coarse)
    brow = brow.at[0, N_GROUPS:N_GROUPS + N_EXPERTS].set(b_fine)
    x1, h2, ri, rw, cnt = _merge(ya, yb, yc, gates, x, w_br_a.astype(BF16), w_br_b.astype(BF16),
                                 w_br_c.astype(BF16), w_o.astype(BF16), ffn_norm[None], wr, brow)

    counts = cnt[0, :N_EXPERTS]
    padded = (counts + MOE_ROWS - 1) // MOE_ROWS * MOE_ROWS
    pad_end = jnp.cumsum(padded)
    pad_start = pad_end - padded
    slots = (pad_start[ri[:, :TOP_K]] + ri[:, TOP_K:2 * TOP_K]).astype(I32)
    n_slots = n * TOP_K + N_EXPERTS * MOE_ROWS
    block_start = jnp.arange(n_slots // MOE_ROWS, dtype=I32) * MOE_ROWS
    last_owner = jnp.max(jnp.where(padded > 0, jnp.arange(N_EXPERTS, dtype=I32), 0))
    block_expert = jnp.minimum(jnp.sum((pad_end[None, :] <= block_start[:, None]).astype(I32), axis=1),
                               last_owner).astype(I32)

    experts = jnp.arange(N_EXPERTS, dtype=I32)
    owns = padded > 0
    later = lax.cummin(jnp.where(owns, experts, N_EXPERTS)[::-1])[::-1]
    next_owner = jnp.concatenate([later[1:], jnp.full((1,), N_EXPERTS, I32)])
    next_owner = jnp.where(next_owner < N_EXPERTS, next_owner, experts)
    parity = (jnp.cumsum(owns.astype(I32)) - 1) % 2
    next_expert = next_owner[block_expert].astype(I32)
    block_parity = jnp.maximum(parity[block_expert], 0).astype(I32)

    fill = jnp.concatenate([pad_start + counts, pad_end, pad_end[-1:] // MOE_ROWS]).astype(I32)
    xs = _dispatch(h2, slots, fill, n_slots)
    ys = _experts(xs, block_expert, next_expert, block_parity, w_gate_e, w_up_e, w_down_e, layer)
    return _combine_ple(x1, rw, slots, ys, p, layer, ple_norm[None], w_ple_gate.astype(BF16),
                        w_ple.astype(BF16))


def kernel(x, p, positions, attn_norm, w_in, q_norm, k_norm, sinks, gla_wa2, gla_ba, gla_norm, gdn_conv, gdn_a_log, gdn_dt_bias, gdn_norm, w_br_a, w_br_b, w_br_c, w_o, ffn_norm, w_coarse, b_coarse, w_fine, b_fine, w_gate_e, w_up_e, w_down_e, ple_norm, w_ple_gate, w_ple):
    batch, seq, d = x.shape
    n = batch * seq
    depth = p.shape[0]
    cos_t, sin_t = _rope_tables(positions)
    xf = x.reshape(n, d)
    pf = p.reshape(depth, n, p.shape[-1])
    per_layer = (attn_norm, w_in, q_norm, k_norm, sinks, gla_wa2, gla_ba, gla_norm, gdn_conv, gdn_a_log,
                 gdn_dt_bias, gdn_norm, w_br_a, w_br_b, w_br_c, w_o, ffn_norm, w_coarse, b_coarse,
                 w_fine, b_fine, w_gate_e, w_up_e, w_down_e, ple_norm, w_ple_gate, w_ple)
    stacked = (w_gate_e, w_up_e, w_down_e)
    for i in range(depth):
        xf = _layer(xf, pf, i, cos_t, sin_t, batch, seq,
                    *[a if any(a is s for s in stacked) else a[i] for a in per_layer])
    return xf.reshape(batch, seq, d)
```
